```python
import jax, jax.numpy as jnp
from jax import lax
import numpy as np

D_MODEL = 1024
BATCH = 8
SEQ = 4096
DEPTH = 4

N_A_LAYERS = DEPTH // 2
N_B_LAYERS = DEPTH - N_A_LAYERS

NSA_HEADS = 16
NSA_HEAD_DIM = D_MODEL // NSA_HEADS
NSA_KV_GROUPS = 2
NSA_HEADS_PER_GROUP = NSA_HEADS // NSA_KV_GROUPS
NSA_KV_WIDTH = NSA_KV_GROUPS * NSA_HEAD_DIM
NSA_GATE_WIDTH = 3 * NSA_HEADS
NSA_IN_WIDTH = D_MODEL + 6 * NSA_KV_WIDTH + NSA_GATE_WIDTH
CMP_BLOCK = 32
CMP_STRIDE = 16
CMP_MLP_HIDDEN = 128
SLC_BLOCK = 64
SLC_TOPK = 16
WINDOW = 512
NSA_Q_BLOCK = 64
FORCE_BONUS = 1000.0

DIFF_HEADS = 8
DIFF_HEAD_DIM = D_MODEL // (2 * DIFF_HEADS)
DIFF_V_DIM = 2 * DIFF_HEAD_DIM
DIFF_Q_BLOCK = 128

ROPE_THETA = 500000.0
ROPE_FRACTION = 4
D_FF = 2816
DEEPNORM_ALPHA = (2 * DEPTH) ** 0.25
DEEPNORM_BETA = (8 * DEPTH) ** -0.25
NORM_EPS = 1e-5
NEG = -1e30

kernel_name = 'yoco_nsa_diffattn_macaron_deepnorm'


def partial_rope(x, positions):
    d = x.shape[-1]
    r = d // ROPE_FRACTION
    half = r // 2
    inv = ROPE_THETA ** (-jnp.arange(half, dtype=jnp.float32) * 2.0 / r)
    ang = positions.astype(jnp.float32)[:, None] * inv[None, :]
    cos = jnp.cos(ang)[:, None, :].astype(x.dtype)
    sin = jnp.sin(ang)[:, None, :].astype(x.dtype)
    x1 = x[..., :half]
    x2 = x[..., half:r]
    return jnp.concatenate([x1 * cos - x2 * sin, x1 * sin + x2 * cos, x[..., r:]], axis=-1)


def layer_norm(x, g, b):
    xf = x.astype(jnp.float32)
    mu = jnp.mean(xf, axis=-1, keepdims=True)
    var = jnp.mean(jnp.square(xf - mu), axis=-1, keepdims=True)
    return ((xf - mu) * lax.rsqrt(var + NORM_EPS) * g + b).astype(x.dtype)


def rms_norm(x, g):
    xf = x.astype(jnp.float32)
    return (xf * lax.rsqrt(jnp.mean(jnp.square(xf), axis=-1, keepdims=True) + NORM_EPS) * g).astype(x.dtype)


def post_norm(x, f, g, b):
    return layer_norm(DEEPNORM_ALPHA * x + f, g, b)


def swiglu(x, w_gate, w_up, w_down):
    return (jax.nn.silu(x @ w_gate) * (x @ w_up)) @ w_down


def compress(k, pos_emb, w1, b1, w2):
    B, S, G, Dh = k.shape
    n_c = (S - CMP_BLOCK) // CMP_STRIDE + 1
    idx = np.arange(n_c)[:, None] * CMP_STRIDE + np.arange(CMP_BLOCK)[None, :]
    blocks = k[:, idx] + pos_emb[:, None, :]
    blocks = jnp.swapaxes(blocks, 2, 3).reshape(B, n_c, G, CMP_BLOCK * Dh)
    return jax.nn.gelu(blocks @ w1 + b1) @ w2


def cmp_to_slc_matrix(n_c, n_sel):
    c0 = np.arange(n_c)[:, None] * CMP_STRIDE
    s0 = np.arange(n_sel)[None, :] * SLC_BLOCK
    ov = np.minimum(c0 + CMP_BLOCK, s0 + SLC_BLOCK) - np.maximum(c0, s0)
    return (np.clip(ov, 0, None) / CMP_BLOCK).astype(np.float32)


def nsa_mixer(x, w_in, pos_k, w1_k, b1_k, w2_k, pos_v, w1_v, b1_v, w2_v, w_out, positions):
    B, S, _ = x.shape
    H, G, Hg, Dh = NSA_HEADS, NSA_KV_GROUPS, NSA_HEADS_PER_GROUP, NSA_HEAD_DIM
    QB = NSA_Q_BLOCK
    proj = x @ w_in
    cuts = [int(c) for c in np.cumsum([D_MODEL] + [NSA_KV_WIDTH] * 6)]
    q, kc, vc, ks, vs, kw, vw, gates = jnp.split(proj, cuts, axis=-1)
    q = q.reshape(B, S, H, Dh) * (Dh ** -0.5)
    q_rot = partial_rope(q, positions)
    kv = lambda t: t.reshape(B, S, G, Dh)
    k_cmp = compress(kv(kc), pos_k, w1_k, b1_k, w2_k)
    v_cmp = compress(kv(vc), pos_v, w1_v, b1_v, w2_v)
    n_c = k_cmp.shape[1]
    n_sel = S // SLC_BLOCK
    k_blocks = partial_rope(kv(ks), positions).reshape(B, n_sel, SLC_BLOCK, G, Dh).transpose(0, 3, 1, 2, 4)
    v_blocks = kv(vs).reshape(B, n_sel, SLC_BLOCK, G, Dh).transpose(0, 3, 1, 2, 4)
    pad = ((0, 0), (WINDOW, 0), (0, 0), (0, 0))
    kw_pad = jnp.pad(partial_rope(kv(kw), positions), pad)
    vw_pad = jnp.pad(kv(vw), pad)

    cmp_end = jnp.asarray(np.arange(n_c) * CMP_STRIDE + CMP_BLOCK - 1)
    overlap = jnp.asarray(cmp_to_slc_matrix(n_c, n_sel))
    k_top = min(SLC_TOPK, n_sel)
    nqb = S // QB
    bi = jnp.arange(B)[:, None, None, None]
    gi = jnp.arange(G)[None, None, :, None]
    blk = jnp.arange(n_sel)

    def to_blocks(t):
        return jnp.moveaxis(t.reshape(B, nqb, QB, G, Hg, Dh), 1, 0)

    def step(args):
        qc, qr, i = args
        t = i * QB + jnp.arange(QB)
        vis = cmp_end[None, :] <= t[:, None]
        s = jnp.einsum('bqghd,bcgd->bqghc', qc, k_cmp).astype(jnp.float32)
        s = jnp.where(vis[None, :, None, None, :], s, NEG)
        p_cmp = jax.nn.softmax(s, axis=-1) * jnp.any(vis, axis=-1)[None, :, None, None, None]
        o_cmp = jnp.einsum('bqghc,bcgd->bqghd', p_cmp.astype(v_cmp.dtype), v_cmp)
        cur = t // SLC_BLOCK
        imp = jnp.einsum('bqghc,cn->bqgn', p_cmp, overlap)
        forced = (blk[None] == 0) | (blk[None] == cur[:, None]) | (blk[None] == cur[:, None] - 1)
        valid = blk[None] <= cur[:, None]
        imp = jnp.where(valid[None, :, None, :], imp + jnp.where(forced, FORCE_BONUS, 0.0)[None, :, None, :], NEG)
        _, sel = lax.top_k(imp, k_top)
        k_sel = k_blocks[bi, gi, sel]
        v_sel = v_blocks[bi, gi, sel]
        key_pos = sel[..., None] * SLC_BLOCK + jnp.arange(SLC_BLOCK)
        ok = key_pos <= t[None, :, None, None, None]
        s = jnp.einsum('bqghd,bqgnkd->bqghnk', qr, k_sel).astype(jnp.float32)
        s = jnp.where(ok[:, :, :, None], s, NEG)
        p = jax.nn.softmax(s.reshape(s.shape[:4] + (-1,)), axis=-1).reshape(s.shape)
        o_slc = jnp.einsum('bqghnk,bqgnkd->bqghd', p.astype(v_sel.dtype), v_sel)
        start = i * QB
        k_win = lax.dynamic_slice_in_dim(kw_pad, start, QB + WINDOW, axis=1)
        v_win = lax.dynamic_slice_in_dim(vw_pad, start, QB + WINDOW, axis=1)
        kpos = start - WINDOW + jnp.arange(QB + WINDOW)
        okw = (kpos[None] <= t[:, None]) & (kpos[None] > t[:, None] - WINDOW) & (kpos[None] >= 0)
        s = jnp.einsum('bqghd,bkgd->bqghk', qr, k_win).astype(jnp.float32)
        s = jnp.where(okw[None, :, None, None, :], s, NEG)
        p = jax.nn.softmax(s, axis=-1)
        o_win = jnp.einsum('bqghk,bkgd->bqghd', p.astype(v_win.dtype), v_win)
        return o_cmp, o_slc, o_win

    outs = lax.map(step, (to_blocks(q), to_blocks(q_rot), jnp.arange(nqb)))
    o_cmp, o_slc, o_win = [jnp.moveaxis(o, 0, 1).reshape(B, S, H, Dh) for o in outs]
    g = jax.nn.sigmoid(gates.reshape(B, S, 3, H))[..., None]
    o = g[:, :, 0] * o_cmp + g[:, :, 1] * o_slc + g[:, :, 2] * o_win
    return o.reshape(B, S, D_MODEL) @ w_out


def shared_kv(x, w_kv, positions):
    B, S, _ = x.shape
    k, v = jnp.split(x @ w_kv, [D_MODEL], axis=-1)
    k = k.reshape(B, S, 2, DIFF_HEADS, DIFF_HEAD_DIM)
    k1 = partial_rope(k[:, :, 0], positions)
    k2 = partial_rope(k[:, :, 1], positions)
    return k1, k2, v.reshape(B, S, DIFF_HEADS, DIFF_V_DIM)


def diff_mixer(x, k1, k2, v, w_q, lq1, lk1, lq2, lk2, subln_g, w_out, positions, lambda_init):
    B, S, _ = x.shape
    H, Dh, QB = DIFF_HEADS, DIFF_HEAD_DIM, DIFF_Q_BLOCK
    q = (x @ w_q).reshape(B, S, 2, H, Dh) * (Dh ** -0.5)
    q1 = partial_rope(q[:, :, 0], positions)
    q2 = partial_rope(q[:, :, 1], positions)
    lam = (jnp.exp(jnp.sum(lq1.astype(jnp.float32) * lk1.astype(jnp.float32)))
           - jnp.exp(jnp.sum(lq2.astype(jnp.float32) * lk2.astype(jnp.float32))) + lambda_init)
    nqb = S // QB
    to_blocks = lambda t: jnp.moveaxis(t.reshape(B, nqb, QB, H, Dh), 1, 0)
    kpos = jnp.arange(S)

    def step(args):
        q1b, q2b, i = args
        t = i * QB + jnp.arange(QB)
        mask = (kpos[None, :] <= t[:, None])[None, None]

        def attn_map(qb, k):
            s = jnp.einsum('bqhd,bkhd->bhqk', qb, k).astype(jnp.float32)
            return jax.nn.softmax(jnp.where(mask, s, NEG), axis=-1)

        a = attn_map(q1b, k1) - lam * attn_map(q2b, k2)
        return jnp.einsum('bhqk,bkhe->bqhe', a.astype(v.dtype), v)

    o = lax.map(step, (to_blocks(q1), to_blocks(q2), jnp.arange(nqb)))
    o = jnp.moveaxis(o, 0, 1).reshape(B, S, H, DIFF_V_DIM)
    o = rms_norm(o, subln_g) * (1.0 - lambda_init)
    return o.reshape(B, S, D_MODEL) @ w_out


def setup_inputs(seed: int = 0) -> dict:
    key = jax.random.key(seed)
    ks = jax.random.split(key, 32)
    f32 = jnp.float32
    nrm = lambda k, shape, scale: jax.random.normal(k, shape, f32) * scale
    D, F, Dh = D_MODEL, D_FF, NSA_HEAD_DIM
    L_flat = CMP_BLOCK * Dh
    return {
        'x': nrm(ks[0], (BATCH, SEQ, D), 1.0),
        'ffn1_w_gate': nrm(ks[1], (DEPTH, D, F), D ** -0.5),
        'ffn1_w_up': nrm(ks[2], (DEPTH, D, F), D ** -0.5),
        'ffn1_w_down': nrm(ks[3], (DEPTH, F, D), F ** -0.5 * DEEPNORM_BETA),
        'ffn2_w_gate': nrm(ks[4], (DEPTH, D, F), D ** -0.5),
        'ffn2_w_up': nrm(ks[5], (DEPTH, D, F), D ** -0.5),
        'ffn2_w_down': nrm(ks[6], (DEPTH, F, D), F ** -0.5 * DEEPNORM_BETA),
        'ln_g': 1.0 + nrm(ks[7], (DEPTH, 3, D), 0.02),
        'ln_b': nrm(ks[8], (DEPTH, 3, D), 0.02),
        'a_w_in': nrm(ks[9], (N_A_LAYERS, D, NSA_IN_WIDTH), D ** -0.5),
        'a_cmp_pos_k': nrm(ks[10], (N_A_LAYERS, CMP_BLOCK, Dh), 0.02),
        'a_cmp_w1_k': nrm(ks[11], (N_A_LAYERS, L_flat, CMP_MLP_HIDDEN), L_flat ** -0.5),
        'a_cmp_b1_k': nrm(ks[12], (N_A_LAYERS, CMP_MLP_HIDDEN), 0.02),
        'a_cmp_w2_k': nrm(ks[13], (N_A_LAYERS, CMP_MLP_HIDDEN, Dh), CMP_MLP_HIDDEN ** -0.5),
        'a_cmp_pos_v': nrm(ks[14], (N_A_LAYERS, CMP_BLOCK, Dh), 0.02),
        'a_cmp_w1_v': nrm(ks[15], (N_A_LAYERS, L_flat, CMP_MLP_HIDDEN), L_flat ** -0.5),
        'a_cmp_b1_v': nrm(ks[16], (N_A_LAYERS, CMP_MLP_HIDDEN), 0.02),
        'a_cmp_w2_v': nrm(ks[17], (N_A_LAYERS, CMP_MLP_HIDDEN, Dh), CMP_MLP_HIDDEN ** -0.5),
        'a_w_out': nrm(ks[18], (N_A_LAYERS, D, D), D ** -0.5 * DEEPNORM_BETA),
        'b_w_kv_shared': nrm(ks[19], (D, 2 * D), D ** -0.5),
        'b_w_q': nrm(ks[20], (N_B_LAYERS, D, D), D ** -0.5),
        'b_lambda_q1': nrm(ks[21], (N_B_LAYERS, DIFF_HEAD_DIM), 0.1),
        'b_lambda_k1': nrm(ks[22], (N_B_LAYERS, DIFF_HEAD_DIM), 0.1),
        'b_lambda_q2': nrm(ks[23], (N_B_LAYERS, DIFF_HEAD_DIM), 0.1),
        'b_lambda_k2': nrm(ks[24], (N_B_LAYERS, DIFF_HEAD_DIM), 0.1),
        'b_subln_g': 1.0 + nrm(ks[25], (N_B_LAYERS, DIFF_V_DIM), 0.02),
        'b_w_out': nrm(ks[26], (N_B_LAYERS, D, D), D ** -0.5 * DEEPNORM_BETA),
    }


def reference(x, ffn1_w_gate, ffn1_w_up, ffn1_w_down, ffn2_w_gate, ffn2_w_up, ffn2_w_down,
              ln_g, ln_b, a_w_in, a_cmp_pos_k, a_cmp_w1_k, a_cmp_b1_k, a_cmp_w2_k,
              a_cmp_pos_v, a_cmp_w1_v, a_cmp_b1_v, a_cmp_w2_v, a_w_out,
              b_w_kv_shared, b_w_q, b_lambda_q1, b_lambda_k1, b_lambda_q2, b_lambda_k2,
              b_subln_g, b_w_out):
    S = x.shape[1]
    positions = jnp.arange(S)
    k1 = k2 = v = None
    for layer in range(DEPTH):
        x = post_norm(x, 0.5 * swiglu(x, ffn1_w_gate[layer], ffn1_w_up[layer], ffn1_w_down[layer]),
                      ln_g[layer, 0], ln_b[layer, 0])
        if layer < N_A_LAYERS:
            mix = nsa_mixer(x, a_w_in[layer], a_cmp_pos_k[layer], a_cmp_w1_k[layer], a_cmp_b1_k[layer],
                            a_cmp_w2_k[layer], a_cmp_pos_v[layer], a_cmp_w1_v[layer], a_cmp_b1_v[layer],
                            a_cmp_w2_v[layer], a_w_out[layer], positions)
        else:
            j = layer - N_A_LAYERS
            lambda_init = 0.8 - 0.6 * float(np.exp(-0.3 * layer))
            mix = diff_mixer(x, k1, k2, v, b_w_q[j], b_lambda_q1[j], b_lambda_k1[j], b_lambda_q2[j],
                             b_lambda_k2[j], b_subln_g[j], b_w_out[j], positions, lambda_init)
        x = post_norm(x, mix, ln_g[layer, 1], ln_b[layer, 1])
        x = post_norm(x, 0.5 * swiglu(x, ffn2_w_gate[layer], ffn2_w_up[layer], ffn2_w_down[layer]),
                      ln_g[layer, 2], ln_b[layer, 2])
        if layer == N_A_LAYERS - 1:
            k1, k2, v = shared_kv(x, b_w_kv_shared, positions)
    return x
```

```python
import functools

import numpy as np
import jax
import jax.numpy as jnp
from jax import lax
from jax.experimental import pallas as pl
from jax.experimental.pallas import tpu as pltpu

F32 = jnp.float32
BF16 = jnp.bfloat16

D_MODEL = 1024
DEPTH = 4
N_A_LAYERS = DEPTH // 2
NSA_HEADS = 16
NSA_HEAD_DIM = D_MODEL // NSA_HEADS
NSA_KV_GROUPS = 2
NSA_KV_WIDTH = NSA_KV_GROUPS * NSA_HEAD_DIM
NSA_GATE_WIDTH = 3 * NSA_HEADS
CMP_BLOCK = 32
CMP_STRIDE = 16
CMP_MLP_HIDDEN = 128
SLC_BLOCK = 64
SLC_TOPK = 16
WINDOW = 512
FORCE_BONUS = 1000.0
DIFF_HEADS = 8
DIFF_HEAD_DIM = D_MODEL // (2 * DIFF_HEADS)
DIFF_V_DIM = 2 * DIFF_HEAD_DIM
ROPE_THETA = 500000.0
ROPE_FRACTION = 4
D_FF = 2816
DEEPNORM_ALPHA = (2 * DEPTH) ** 0.25
NORM_EPS = 1e-5
NEG = -1e30

LANES = 128
ROW_TILE = 512
FFN_COL_CHUNK = 512
ATT_Q_TILE = 128
ATT_KEY_BLOCK = 256
VMEM_LIMIT_BYTES = 56 * 1024 * 1024
N_CHUNKS = D_MODEL // LANES


def _cparams(n_axes):
    return pltpu.CompilerParams(dimension_semantics=("parallel",) * n_axes,
                                vmem_limit_bytes=VMEM_LIMIT_BYTES)


def _resident(shape):
    zeros = (0,) * len(shape)
    return pl.BlockSpec(shape, lambda *_: zeros, pipeline_mode=pl.Buffered(1))


def _layer_norm_rows(r, g, b):
    mu = jnp.mean(r, axis=-1, keepdims=True)
    xc = r - mu
    var = jnp.mean(xc * xc, axis=-1, keepdims=True)
    return xc * lax.rsqrt(var + NORM_EPS) * g + b


def _ffn_ln_kernel(x_ref, wg_ref, wu_ref, wd_ref, g_ref, b_ref, o_ref, h_ref):
    x = x_ref[...]
    xb = x.astype(BF16)
    d_ff = wg_ref.shape[1]
    for c0 in range(0, d_ff, FFN_COL_CHUNK):
        c1 = min(c0 + FFN_COL_CHUNK, d_ff)
        gate = jnp.dot(xb, wg_ref[:, c0:c1], preferred_element_type=F32)
        up = jnp.dot(xb, wu_ref[:, c0:c1], preferred_element_type=F32)
        silu = gate * (1.0 / (1.0 + jnp.exp(-gate)))
        h_ref[:, c0:c1] = (silu * up).astype(BF16)
    y = jnp.dot(h_ref[...], wd_ref[...], preferred_element_type=F32)
    o_ref[...] = _layer_norm_rows(DEEPNORM_ALPHA * x + 0.5 * y, g_ref[...], b_ref[...])


def _ffn_ln(x2d, wg, wu, wd, g, b):
    n, d = x2d.shape
    d_ff = wg.shape[1]
    return pl.pallas_call(
        _ffn_ln_kernel,
        grid=(n // ROW_TILE,),
        in_specs=[pl.BlockSpec((ROW_TILE, d), lambda i: (i, 0)),
                  _resident((d, d_ff)), _resident((d, d_ff)), _resident((d_ff, d)),
                  _resident((1, d)), _resident((1, d))],
        out_specs=pl.BlockSpec((ROW_TILE, d), lambda i: (i, 0)),
        out_shape=jax.ShapeDtypeStruct((n, d), F32),
        scratch_shapes=[pltpu.VMEM((ROW_TILE, d_ff), BF16)],
        compiler_params=_cparams(1),
        name="ffn_ln",
    )(x2d, wg, wu, wd, g, b)


def _rope_tables(seq):
    r = NSA_HEAD_DIM // ROPE_FRACTION
    half = r // 2
    inv = ROPE_THETA ** (-jnp.arange(half, dtype=F32) * 2.0 / r)
    ang = jnp.arange(seq).astype(F32)[:, None] * inv[None, :]
    cos, sin = jnp.cos(ang), jnp.sin(ang)
    pad = jnp.zeros((seq, NSA_HEAD_DIM - r), F32)
    zero = jnp.zeros_like(sin)
    c_head = jnp.concatenate([cos, cos, pad + 1.0], axis=1)
    s_lo_head = jnp.concatenate([-sin, zero, pad], axis=1)
    s_hi_head = jnp.concatenate([zero, sin, pad], axis=1)
    rep = lambda t: jnp.concatenate([t, t], axis=1)
    return rep(c_head), rep(s_lo_head), rep(s_hi_head)


def _rope_chunk(x, c, s_lo, s_hi):
    half = NSA_HEAD_DIM // ROPE_FRACTION // 2
    return x * c + pltpu.roll(x, LANES - half, 1) * s_lo + pltpu.roll(x, half, 1) * s_hi


def _nsa_proj_kernel(x_ref, w_ref, c_ref, sl_ref, sh_ref,
                     qc_ref, qr_ref, kc_ref, vc_ref, ks_ref, kw_ref, vst_ref, vwt_ref, gt_ref):
    proj = jnp.dot(x_ref[...].astype(BF16), w_ref[...], preferred_element_type=F32)
    c, sl, sh = c_ref[...], sl_ref[...], sh_ref[...]
    col = lambda j: proj[:, j * LANES:(j + 1) * LANES]
    for j in range(N_CHUNKS):
        q = col(j) * (NSA_HEAD_DIM ** -0.5)
        qc_ref[j] = q.astype(BF16)
        qr_ref[j] = _rope_chunk(q, c, sl, sh).astype(BF16)
    base = N_CHUNKS
    kc_ref[...] = col(base)
    vc_ref[...] = col(base + 1)
    for g in range(NSA_KV_GROUPS):
        ks_ref[g] = _rope_chunk(col(base + 2 + g), c, sl, sh).astype(BF16)
        kw_ref[g] = _rope_chunk(col(base + 4 + g), c, sl, sh).astype(BF16)
    n_kb = x_ref.shape[0] // ATT_KEY_BLOCK
    for src, dst in ((base + 6, vst_ref), (base + 7, vwt_ref)):
        vt = col(src).T.astype(BF16)
        for g in range(NSA_KV_GROUPS):
            for kb in range(n_kb):
                dst[g, kb] = vt[g * NSA_HEAD_DIM:(g + 1) * NSA_HEAD_DIM,
                                kb * ATT_KEY_BLOCK:(kb + 1) * ATT_KEY_BLOCK]
    gates = col(base + 8)
    gt_ref[...] = (1.0 / (1.0 + jnp.exp(-gates))).T


def _nsa_proj_weight(w_in):
    d = D_MODEL
    kvw = NSA_KV_WIDTH
    q, kc, vc, ks, vs, kw, vw, gates = jnp.split(
        w_in, [int(c) for c in np.cumsum([d] + [kvw] * 6)], axis=1)
    dh = NSA_HEAD_DIM
    dup = lambda t: jnp.concatenate([t[:, :dh], t[:, :dh], t[:, dh:], t[:, dh:]], axis=1)
    gates = jnp.pad(gates, ((0, 0), (0, LANES - NSA_GATE_WIDTH)))
    return jnp.concatenate([q, kc, vc, dup(ks), dup(kw), vs, vw, gates], axis=1).astype(BF16)


def _nsa_proj(x, w, tables):
    b, s, d = x.shape
    ncol = w.shape[1]
    nkb = s // ATT_KEY_BLOCK
    kb_tile = ROW_TILE // ATT_KEY_BLOCK
    dh = NSA_HEAD_DIM
    tab_spec = pl.BlockSpec((ROW_TILE, LANES), lambda bi, i: (i, 0))
    chunk_spec = lambda n: pl.BlockSpec((None, n, ROW_TILE, LANES), lambda bi, i: (bi, 0, i, 0))
    row_spec = pl.BlockSpec((None, ROW_TILE, LANES), lambda bi, i: (bi, i, 0))
    vt_spec = pl.BlockSpec((None, NSA_KV_GROUPS, kb_tile, dh, ATT_KEY_BLOCK), lambda bi, i: (bi, 0, i, 0, 0))
    return pl.pallas_call(
        _nsa_proj_kernel,
        grid=(b, s // ROW_TILE),
        in_specs=[pl.BlockSpec((None, ROW_TILE, d), lambda bi, i: (bi, i, 0)),
                  _resident((d, ncol)), tab_spec, tab_spec, tab_spec],
        out_specs=[chunk_spec(N_CHUNKS), chunk_spec(N_CHUNKS), row_spec, row_spec,
                   chunk_spec(NSA_KV_GROUPS), chunk_spec(NSA_KV_GROUPS), vt_spec, vt_spec,
                   pl.BlockSpec((None, LANES, ROW_TILE), lambda bi, i: (bi, 0, i))],
        out_shape=[jax.ShapeDtypeStruct((b, N_CHUNKS, s, LANES), BF16),
                   jax.ShapeDtypeStruct((b, N_CHUNKS, s, LANES), BF16),
                   jax.ShapeDtypeStruct((b, s, LANES), F32),
                   jax.ShapeDtypeStruct((b, s, LANES), F32),
                   jax.ShapeDtypeStruct((b, NSA_KV_GROUPS, s, LANES), BF16),
                   jax.ShapeDtypeStruct((b, NSA_KV_GROUPS, s, LANES), BF16),
                   jax.ShapeDtypeStruct((b, NSA_KV_GROUPS, nkb, dh, ATT_KEY_BLOCK), BF16),
                   jax.ShapeDtypeStruct((b, NSA_KV_GROUPS, nkb, dh, ATT_KEY_BLOCK), BF16),
                   jax.ShapeDtypeStruct((b, LANES, s), F32)],
        compiler_params=_cparams(2),
        name="nsa_proj",
    )(x, w, *tables)


def _gelu_tanh(x):
    return 0.5 * x * (1.0 + jnp.tanh(np.sqrt(2.0 / np.pi).astype(np.float32) * (x + 0.044715 * (x * x * x))))


def _compress_kernel(kch_ref, vch_ref, pk_ref, pv_ref, w1k_ref, w1v_ref, b1k_ref, b1v_ref, w2k_ref, w2v_ref,
                     kcd_ref, vct_ref):
    n_rows = kch_ref.shape[0]

    def mlp(ch_ref, pos_ref, w1_ref, b1_ref, w2_ref):
        ch = ch_ref[...]
        lo = jnp.dot((ch + pos_ref[0:1, :]).astype(BF16), w1_ref[0], preferred_element_type=F32)
        hi = jnp.dot((ch + pos_ref[1:2, :]).astype(BF16), w1_ref[1], preferred_element_type=F32)
        hid = lo + pltpu.roll(hi, n_rows - 1, 0) + b1_ref[...]
        return jnp.dot(_gelu_tanh(hid).astype(BF16), w2_ref[...], preferred_element_type=F32)

    kcd = mlp(kch_ref, pk_ref, w1k_ref, b1k_ref, w2k_ref)
    for g in range(NSA_KV_GROUPS):
        kcd_ref[g] = kcd[:, g * LANES:(g + 1) * LANES].astype(BF16)
    vct = mlp(vch_ref, pv_ref, w1v_ref, b1v_ref, w2v_ref).T
    for g in range(NSA_KV_GROUPS):
        vct_ref[g] = vct[g * NSA_HEAD_DIM:(g + 1) * NSA_HEAD_DIM, :].astype(BF16)


def _compress_weights(pos, w1, b1, w2, dup_out):
    g_n, dh, hid = NSA_KV_GROUPS, NSA_HEAD_DIM, CMP_MLP_HIDDEN
    w1r = w1.reshape(2, CMP_STRIDE, dh, hid)
    eye = jnp.eye(g_n, dtype=w1.dtype)
    w1x = jnp.einsum('hldm,gk->hlgdkm', w1r, eye).reshape(2, CMP_STRIDE * g_n * dh, g_n * hid)
    posx = jnp.broadcast_to(pos.reshape(2, CMP_STRIDE, 1, dh), (2, CMP_STRIDE, g_n, dh)).reshape(2, -1)
    b1x = jnp.tile(b1, g_n)[None, :]
    reps = 2 if dup_out else 1
    w2x = jnp.einsum('md,gk->gmkd', w2, eye)
    w2x = jnp.broadcast_to(w2x[:, :, :, None, :], (g_n, hid, g_n, reps, dh)).reshape(g_n * hid, g_n * reps * dh)
    return posx, w1x.astype(BF16), b1x, w2x.astype(BF16)


def _compress(kc, vc, wk, wv):
    b, s, _ = kc.shape
    n_rows = s // CMP_STRIDE
    width = CMP_STRIDE * LANES
    kch = kc.reshape(b, n_rows, width)
    vch = vc.reshape(b, n_rows, width)
    pk, w1k, b1k, w2k = wk
    pv, w1v, b1v, w2v = wv
    ch_spec = pl.BlockSpec((None, n_rows, width), lambda bi: (bi, 0, 0))
    return pl.pallas_call(
        _compress_kernel,
        grid=(b,),
        in_specs=[ch_spec, ch_spec, _resident(pk.shape), _resident(pv.shape), _resident(w1k.shape),
                  _resident(w1v.shape), _resident(b1k.shape), _resident(b1v.shape), _resident(w2k.shape),
                  _resident(w2v.shape)],
        out_specs=[pl.BlockSpec((None, NSA_KV_GROUPS, n_rows, LANES), lambda bi: (bi, 0, 0, 0)),
                   pl.BlockSpec((None, NSA_KV_GROUPS, NSA_HEAD_DIM, n_rows), lambda bi: (bi, 0, 0, 0))],
        out_shape=[jax.ShapeDtypeStruct((b, NSA_KV_GROUPS, n_rows, LANES), BF16),
                   jax.ShapeDtypeStruct((b, NSA_KV_GROUPS, NSA_HEAD_DIM, n_rows), BF16)],
        compiler_params=_cparams(1),
        name="nsa_compress",
    )(kch, vch, pk, pv, w1k, w1v, b1k, b1v, w2k, w2v)


def _split_heads_rhs(q):
    lane = lax.broadcasted_iota(jnp.int32, q.shape, 1)
    zero = jnp.zeros_like(q)
    return jnp.concatenate([jnp.where(lane < LANES // 2, q, zero), jnp.where(lane >= LANES // 2, q, zero)], axis=0)


def _scores_t(k, rhs):
    return lax.dot_general(k, rhs, (((1,), (1,)), ((), ())), preferred_element_type=F32)


def _two(row):
    return jnp.concatenate([row, row], axis=1)


def _nsa_attn_kernel(qc_ref, qr_ref, kcd_ref, vct_ref, ks_ref, kw_ref, vst_ref, vwt_ref, gt_ref, ov_ref,
                     o_ref, ocmp_ref, psum_ref, imp_ref, selb_ref):
    g = pl.program_id(1)
    qi = pl.program_id(2)
    tq = ATT_Q_TILE
    t0 = qi * tq
    n_pairs = qc_ref.shape[0]
    n_cmp = kcd_ref.shape[0]
    seq = ks_ref.shape[0]
    n_sel = seq // SLC_BLOCK
    tok1 = t0 + lax.broadcasted_iota(jnp.int32, (1, tq), 1)
    tok2 = _two(tok1)

    cmp_end = lax.broadcasted_iota(jnp.int32, (n_cmp, 1), 0) * CMP_STRIDE + (CMP_BLOCK - 1)
    vis = cmp_end <= tok2
    any_vis = (tok2 >= CMP_BLOCK - 1).astype(F32)
    psum_ref[...] = jnp.zeros_like(psum_ref)

    def cmp_body(c, carry):
        rhs = _split_heads_rhs(qc_ref[c])
        s = jnp.where(vis, _scores_t(kcd_ref[...], rhs), NEG)
        m = jnp.max(s, axis=0, keepdims=True)
        e = jnp.exp(s - m)
        p = e / jnp.sum(e, axis=0, keepdims=True) * any_vis
        psum_ref[...] += p[:, :tq] + p[:, tq:]
        ocmp_ref[c] = jnp.dot(vct_ref[...], p.astype(BF16), preferred_element_type=F32)
        return carry

    lax.fori_loop(0, n_pairs, cmp_body, 0)

    imp = jnp.dot(ov_ref[...], psum_ref[...], precision=lax.Precision.HIGHEST, preferred_element_type=F32)
    blk = lax.broadcasted_iota(jnp.int32, (n_sel, 1), 0)
    cur = tok1 // SLC_BLOCK
    forced = (blk == 0) | (blk == cur) | (blk == cur - 1)
    valid = blk <= cur
    imp = jnp.where(valid, imp + jnp.where(forced, FORCE_BONUS, 0.0), NEG)
    imp_ref[...] = imp

    def rank_body(m, rank):
        row = imp_ref[pl.ds(m, 1), :]
        beats = (row > imp) | ((row == imp) & (blk > m))
        return rank + beats.astype(jnp.int32)

    rank = lax.fori_loop(0, n_sel, rank_body, jnp.zeros((n_sel, tq), jnp.int32))
    selb_ref[...] = jnp.where((rank < SLC_TOPK) & valid, 0.0, NEG)

    kb_sz = ATT_KEY_BLOCK
    n_kb = (t0 + tq + kb_sz - 1) // kb_sz
    blocks_per_kb = kb_sz // SLC_BLOCK
    win_blocks = WINDOW // kb_sz + 1
    win_start = jnp.minimum(jnp.maximum(t0 - WINDOW, 0) // kb_sz, seq // kb_sz - win_blocks)
    krow = lax.broadcasted_iota(jnp.int32, (kb_sz, 1), 0)

    def pair_body(c, carry):
        rhs = _split_heads_rhs(qr_ref[c])

        def sel_body(kb, st):
            m, l, acc = st
            k0 = pl.multiple_of(kb * kb_sz, kb_sz)
            s = _scores_t(ks_ref[pl.ds(k0, kb_sz), :], rhs)
            bias = jnp.concatenate(
                [jnp.broadcast_to(_two(selb_ref[pl.ds(kb * blocks_per_kb + j, 1), :]), (SLC_BLOCK, 2 * tq))
                 for j in range(blocks_per_kb)], axis=0)
            s = jnp.where(k0 + krow <= tok2, s + bias, NEG)
            m_new = jnp.maximum(m, jnp.max(s, axis=0, keepdims=True))
            alpha = jnp.exp(m - m_new)
            p = jnp.exp(s - m_new)
            l = alpha * l + jnp.sum(p, axis=0, keepdims=True)
            acc = alpha * acc + jnp.dot(vst_ref[kb], p.astype(BF16), preferred_element_type=F32)
            return m_new, l, acc

        init = (jnp.full((1, 2 * tq), NEG, F32), jnp.zeros((1, 2 * tq), F32),
                jnp.zeros((NSA_HEAD_DIM, 2 * tq), F32))
        _, l_sel, acc_sel = lax.fori_loop(0, n_kb, sel_body, init)
        o_sel = acc_sel / l_sel

        w0 = pl.multiple_of(win_start * kb_sz, kb_sz)
        wrow = w0 + lax.broadcasted_iota(jnp.int32, (win_blocks * kb_sz, 1), 0)
        s = _scores_t(kw_ref[pl.ds(w0, win_blocks * kb_sz), :], rhs)
        s = jnp.where((wrow <= tok2) & (wrow > tok2 - WINDOW), s, NEG)
        m = jnp.max(s, axis=0, keepdims=True)
        e = jnp.exp(s - m)
        p = (e / jnp.sum(e, axis=0, keepdims=True)).astype(BF16)
        o_win = jnp.zeros((NSA_HEAD_DIM, 2 * tq), F32)
        for j in range(win_blocks):
            o_win += jnp.dot(vwt_ref[win_start + j], p[j * kb_sz:(j + 1) * kb_sz, :], preferred_element_type=F32)

        head = g * (NSA_HEADS // NSA_KV_GROUPS) + 2 * c
        gate = lambda j: jnp.concatenate([gt_ref[pl.ds(j * NSA_HEADS + head, 1), :],
                                          gt_ref[pl.ds(j * NSA_HEADS + head + 1, 1), :]], axis=1)
        o_t = gate(0) * ocmp_ref[c] + gate(1) * o_sel + gate(2) * o_win
        o_ref[c] = jnp.concatenate([o_t[:, :tq], o_t[:, tq:]], axis=0).T.astype(BF16)
        return carry

    lax.fori_loop(0, n_pairs, pair_body, 0)


def _cmp_to_slc_t(n_cmp_rows, n_sel):
    c0 = np.arange(n_cmp_rows)[None, :] * CMP_STRIDE
    s0 = np.arange(n_sel)[:, None] * SLC_BLOCK
    ov = np.minimum(c0 + CMP_BLOCK, s0 + SLC_BLOCK) - np.maximum(c0, s0)
    return (np.clip(ov, 0, None) / CMP_BLOCK).astype(np.float32)


def _nsa_attn(qc, qr, kcd, vct, ksd, kwd, vst, vwt, gt):
    b, n_chunks, s, _ = qc.shape
    groups = NSA_KV_GROUPS
    pairs = n_chunks // groups
    n_cmp = kcd.shape[2]
    nkb = s // ATT_KEY_BLOCK
    dh = NSA_HEAD_DIM
    tq = ATT_Q_TILE
    n_sel = s // SLC_BLOCK
    ov_t = jnp.asarray(_cmp_to_slc_t(n_cmp, n_sel))
    q_spec = pl.BlockSpec((None, pairs, tq, LANES), lambda bi, g, qi: (bi, g, qi, 0))
    per_group = lambda *shape: pl.BlockSpec((None, None) + shape, lambda bi, g, qi: (bi, g) + (0,) * len(shape))
    return pl.pallas_call(
        _nsa_attn_kernel,
        grid=(b, groups, s // tq),
        in_specs=[q_spec, q_spec, per_group(n_cmp, LANES), per_group(dh, n_cmp), per_group(s, LANES),
                  per_group(s, LANES), per_group(nkb, dh, ATT_KEY_BLOCK), per_group(nkb, dh, ATT_KEY_BLOCK),
                  pl.BlockSpec((None, LANES, tq), lambda bi, g, qi: (bi, 0, qi)),
                  _resident(ov_t.shape)],
        out_specs=q_spec,
        out_shape=jax.ShapeDtypeStruct((b, n_chunks, s, LANES), BF16),
        scratch_shapes=[pltpu.VMEM((pairs, dh, 2 * tq), F32), pltpu.VMEM((n_cmp, tq), F32),
                        pltpu.VMEM((n_sel, tq), F32), pltpu.VMEM((n_sel, tq), F32)],
        compiler_params=_cparams(3),
        name="nsa_attn",
    )(qc, qr, kcd, vct, ksd, kwd, vst, vwt, gt, ov_t)


def _out_ln_kernel(o_ref, x_ref, w_ref, g_ref, b_ref, y_ref):
    o = jnp.concatenate([o_ref[j] for j in range(o_ref.shape[0])], axis=1)
    mix = jnp.dot(o, w_ref[...], preferred_element_type=F32)
    y_ref[...] = _layer_norm_rows(DEEPNORM_ALPHA * x_ref[...] + mix, g_ref[...], b_ref[...])


def _out_ln(o, x, w, g, b):
    bsz, n_chunks, s, _ = o.shape
    d = x.shape[-1]
    x_spec = pl.BlockSpec((None, ROW_TILE, d), lambda bi, i: (bi, i, 0))
    return pl.pallas_call(
        _out_ln_kernel,
        grid=(bsz, s // ROW_TILE),
        in_specs=[pl.BlockSpec((None, n_chunks, ROW_TILE, LANES), lambda bi, i: (bi, 0, i, 0)), x_spec,
                  _resident((d, d)), _resident((1, d)), _resident((1, d))],
        out_specs=x_spec,
        out_shape=jax.ShapeDtypeStruct(x.shape, F32),
        compiler_params=_cparams(2),
        name="mixer_out_ln",
    )(o, x, w, g, b)


def _pair_maps_columns(w):
    d = w.shape[0]
    return w.reshape(d, 2, DIFF_HEADS, DIFF_HEAD_DIM).transpose(0, 2, 1, 3).reshape(d, -1)


def _kv_proj_kernel(x_ref, w_ref, c_ref, sl_ref, sh_ref, k_ref, vt_ref):
    proj = jnp.dot(x_ref[...].astype(BF16), w_ref[...], preferred_element_type=F32)
    c, sl, sh = c_ref[...], sl_ref[...], sh_ref[...]
    n_kb = x_ref.shape[0] // ATT_KEY_BLOCK
    for h in range(DIFF_HEADS):
        k_ref[h] = _rope_chunk(proj[:, h * LANES:(h + 1) * LANES], c, sl, sh).astype(BF16)
        vt = proj[:, D_MODEL + h * DIFF_V_DIM:D_MODEL + (h + 1) * DIFF_V_DIM].T.astype(BF16)
        for kb in range(n_kb):
            vt_ref[h, kb] = vt[:, kb * ATT_KEY_BLOCK:(kb + 1) * ATT_KEY_BLOCK]


def _kv_proj(x, w, tables):
    b, s, d = x.shape
    nkb = s // ATT_KEY_BLOCK
    kb_tile = ROW_TILE // ATT_KEY_BLOCK
    tab_spec = pl.BlockSpec((ROW_TILE, LANES), lambda bi, i: (i, 0))
    return pl.pallas_call(
        _kv_proj_kernel,
        grid=(b, s // ROW_TILE),
        in_specs=[pl.BlockSpec((None, ROW_TILE, d), lambda bi, i: (bi, i, 0)), _resident(w.shape),
                  tab_spec, tab_spec, tab_spec],
        out_specs=[pl.BlockSpec((None, DIFF_HEADS, ROW_TILE, LANES), lambda bi, i: (bi, 0, i, 0)),
                   pl.BlockSpec((None, DIFF_HEADS, kb_tile, DIFF_V_DIM, ATT_KEY_BLOCK),
                                lambda bi, i: (bi, 0, i, 0, 0))],
        out_shape=[jax.ShapeDtypeStruct((b, DIFF_HEADS, s, LANES), BF16),
                   jax.ShapeDtypeStruct((b, DIFF_HEADS, nkb, DIFF_V_DIM, ATT_KEY_BLOCK), BF16)],
        compiler_params=_cparams(2),
        name="diff_kv_proj",
    )(x, w, *tables)


def _dq_proj_kernel(x_ref, w_ref, c_ref, sl_ref, sh_ref, q_ref):
    proj = jnp.dot(x_ref[...].astype(BF16), w_ref[...], preferred_element_type=F32)
    c, sl, sh = c_ref[...], sl_ref[...], sh_ref[...]
    for h in range(DIFF_HEADS):
        q = proj[:, h * LANES:(h + 1) * LANES] * (DIFF_HEAD_DIM ** -0.5)
        q_ref[h] = _rope_chunk(q, c, sl, sh).astype(BF16)


def _dq_proj(x, w, tables):
    b, s, d = x.shape
    tab_spec = pl.BlockSpec((ROW_TILE, LANES), lambda bi, i: (i, 0))
    return pl.pallas_call(
        _dq_proj_kernel,
        grid=(b, s // ROW_TILE),
        in_specs=[pl.BlockSpec((None, ROW_TILE, d), lambda bi, i: (bi, i, 0)), _resident(w.shape),
                  tab_spec, tab_spec, tab_spec],
        out_specs=pl.BlockSpec((None, DIFF_HEADS, ROW_TILE, LANES), lambda bi, i: (bi, 0, i, 0)),
        out_shape=jax.ShapeDtypeStruct((b, DIFF_HEADS, s, LANES), BF16),
        compiler_params=_cparams(2),
        name="diff_q_proj",
    )(x, w, *tables)


def _diff_attn_kernel(q_ref, k_ref, vt_ref, lam_ref, gain_ref, o_ref, *, lambda_init):
    qi = pl.program_id(2)
    tq = ATT_Q_TILE
    t0 = qi * tq
    kb_sz = ATT_KEY_BLOCK
    tok2 = _two(t0 + lax.broadcasted_iota(jnp.int32, (1, tq), 1))
    krow = lax.broadcasted_iota(jnp.int32, (kb_sz, 1), 0)
    rhs = _split_heads_rhs(q_ref[...])
    n_kb = (t0 + tq + kb_sz - 1) // kb_sz

    def body(kb, st):
        m, l, acc = st
        k0 = pl.multiple_of(kb * kb_sz, kb_sz)
        s = _scores_t(k_ref[pl.ds(k0, kb_sz), :], rhs)
        s = jnp.where(k0 + krow <= tok2, s, NEG)
        m_new = jnp.maximum(m, jnp.max(s, axis=0, keepdims=True))
        alpha = jnp.exp(m - m_new)
        p = jnp.exp(s - m_new)
        l = alpha * l + jnp.sum(p, axis=0, keepdims=True)
        acc = alpha * acc + jnp.dot(vt_ref[kb], p.astype(BF16), preferred_element_type=F32)
        return m_new, l, acc

    init = (jnp.full((1, 2 * tq), NEG, F32), jnp.zeros((1, 2 * tq), F32), jnp.zeros((DIFF_V_DIM, 2 * tq), F32))
    _, l, acc = lax.fori_loop(0, n_kb, body, init)
    a = acc / l
    lp = lam_ref[...]
    lam = (jnp.exp(jnp.sum(lp[0:1] * lp[1:2], axis=1, keepdims=True))
           - jnp.exp(jnp.sum(lp[2:3] * lp[3:4], axis=1, keepdims=True)) + lambda_init)
    o = a[:, :tq] - lam * a[:, tq:]
    ms = jnp.mean(o * o, axis=0, keepdims=True)
    o = o * lax.rsqrt(ms + NORM_EPS) * gain_ref[...] * (1.0 - lambda_init)
    o_ref[...] = o.T.astype(BF16)


def _diff_attn(q, k, vt, lam_params, gain, lambda_init):
    b, h, s, _ = q.shape
    nkb = s // ATT_KEY_BLOCK
    tq = ATT_Q_TILE
    per_head = lambda *shape: pl.BlockSpec((None, None) + shape, lambda bi, hi, qi: (bi, hi) + (0,) * len(shape))
    q_spec = pl.BlockSpec((None, None, tq, LANES), lambda bi, hi, qi: (bi, hi, qi, 0))
    return pl.pallas_call(
        functools.partial(_diff_attn_kernel, lambda_init=lambda_init),
        grid=(b, h, s // tq),
        in_specs=[q_spec, per_head(s, LANES), per_head(nkb, DIFF_V_DIM, ATT_KEY_BLOCK),
                  _resident(lam_params.shape), _resident(gain.shape)],
        out_specs=q_spec,
        out_shape=jax.ShapeDtypeStruct((b, h, s, LANES), BF16),
        compiler_params=_cparams(3),
        name="diff_attn",
    )(q, k, vt, lam_params, gain)


def kernel(x, ffn1_w_gate, ffn1_w_up, ffn1_w_down, ffn2_w_gate, ffn2_w_up, ffn2_w_down, ln_g, ln_b, a_w_in,
           a_cmp_pos_k, a_cmp_w1_k, a_cmp_b1_k, a_cmp_w2_k, a_cmp_pos_v, a_cmp_w1_v, a_cmp_b1_v, a_cmp_w2_v,
           a_w_out, b_w_kv_shared, b_w_q, b_lambda_q1, b_lambda_k1, b_lambda_q2, b_lambda_k2, b_subln_g, b_w_out):
    bsz, seq, d = x.shape
    assert d == D_MODEL and seq % ROW_TILE == 0 and seq >= WINDOW + 2 * ATT_KEY_BLOCK
    tables = _rope_tables(seq)
    bf = lambda w: w.astype(BF16)
    row = lambda v: v[None, :]

    def ffn(xv, wg, wu, wd, g, b):
        return _ffn_ln(xv.reshape(bsz * seq, d), bf(wg), bf(wu), bf(wd), row(g), row(b)).reshape(bsz, seq, d)

    k12 = vt = None
    for layer in range(DEPTH):
        x = ffn(x, ffn1_w_gate[layer], ffn1_w_up[layer], ffn1_w_down[layer], ln_g[layer, 0], ln_b[layer, 0])
        if layer < N_A_LAYERS:
            qc, qr, kc, vc, ksd, kwd, vst, vwt, gt = _nsa_proj(x, _nsa_proj_weight(a_w_in[layer]), tables)
            wk = _compress_weights(a_cmp_pos_k[layer], a_cmp_w1_k[layer], a_cmp_b1_k[layer], a_cmp_w2_k[layer], True)
            wv = _compress_weights(a_cmp_pos_v[layer], a_cmp_w1_v[layer], a_cmp_b1_v[layer], a_cmp_w2_v[layer], False)
            kcd, vct = _compress(kc, vc, wk, wv)
            o = _nsa_attn(qc, qr, kcd, vct, ksd, kwd, vst, vwt, gt)
            w_out = a_w_out[layer]
        else:
            j = layer - N_A_LAYERS
            lambda_init = 0.8 - 0.6 * float(np.exp(-0.3 * layer))
            q = _dq_proj(x, bf(_pair_maps_columns(b_w_q[j])), tables)
            lam_params = jnp.pad(jnp.stack([b_lambda_q1[j], b_lambda_k1[j], b_lambda_q2[j], b_lambda_k2[j]]),
                                 ((0, 4), (0, LANES - DIFF_HEAD_DIM)))
            gain = jnp.broadcast_to(b_subln_g[j][:, None], (DIFF_V_DIM, ATT_Q_TILE))
            o = _diff_attn(q, k12, vt, lam_params, gain, lambda_init)
            w_out = b_w_out[j]
        x = _out_ln(o, x, bf(w_out), row(ln_g[layer, 1]), row(ln_b[layer, 1]))
        x = ffn(x, ffn2_w_gate[layer], ffn2_w_up[layer], ffn2_w_down[layer], ln_g[layer, 2], ln_b[layer, 2])
        if layer == N_A_LAYERS - 1:
            w_kv = jnp.concatenate([_pair_maps_columns(b_w_kv_shared[:, :D_MODEL]), b_w_kv_shared[:, D_MODEL:]], axis=1)
            k12, vt = _kv_proj(x, bf(w_kv), tables)
    return x
```

```python
import functools

import numpy as np
import jax
import jax.numpy as jnp
from jax import lax
from jax.experimental import pallas as pl
from jax.experimental.pallas import tpu as pltpu

F32 = jnp.float32
BF16 = jnp.bfloat16

D_MODEL = 1024
DEPTH = 4
N_A_LAYERS = DEPTH // 2
NSA_HEADS = 16
NSA_HEAD_DIM = D_MODEL // NSA_HEADS
NSA_KV_GROUPS = 2
NSA_KV_WIDTH = NSA_KV_GROUPS * NSA_HEAD_DIM
NSA_GATE_WIDTH = 3 * NSA_HEADS
CMP_BLOCK = 32
CMP_STRIDE = 16
CMP_MLP_HIDDEN = 128
SLC_BLOCK = 64
SLC_TOPK = 16
WINDOW = 512
FORCE_BONUS = 1000.0
DIFF_HEADS = 8
DIFF_HEAD_DIM = D_MODEL // (2 * DIFF_HEADS)
DIFF_V_DIM = 2 * DIFF_HEAD_DIM
ROPE_THETA = 500000.0
ROPE_FRACTION = 4
D_FF = 2816
DEEPNORM_ALPHA = (2 * DEPTH) ** 0.25
NORM_EPS = 1e-5
NEG = -1e30

LANES = 128
ROW_TILE = 512
FFN_COL_CHUNK = 512
ATT_Q_TILE = 128
ATT_KEY_BLOCK = 256
DIFF_HEADS_PER_STEP = 4
VMEM_LIMIT_BYTES = 56 * 1024 * 1024
N_CHUNKS = D_MODEL // LANES


def _cparams(n_axes):
    return pltpu.CompilerParams(dimension_semantics=("parallel",) * n_axes,
                                vmem_limit_bytes=VMEM_LIMIT_BYTES)


def _resident(shape):
    zeros = (0,) * len(shape)
    return pl.BlockSpec(shape, lambda *_: zeros, pipeline_mode=pl.Buffered(1))


def _layer_norm_rows(r, g, b):
    mu = jnp.mean(r, axis=-1, keepdims=True)
    xc = r - mu
    var = jnp.mean(xc * xc, axis=-1, keepdims=True)
    return xc * lax.rsqrt(var + NORM_EPS) * g + b


def _ffn_ln_kernel(x_ref, wg_ref, wu_ref, wd_ref, g_ref, b_ref, o_ref, h_ref):
    x = x_ref[...]
    xb = x.astype(BF16)
    d_ff = wg_ref.shape[1]
    for c0 in range(0, d_ff, FFN_COL_CHUNK):
        c1 = min(c0 + FFN_COL_CHUNK, d_ff)
        gate = jnp.dot(xb, wg_ref[:, c0:c1], preferred_element_type=F32)
        up = jnp.dot(xb, wu_ref[:, c0:c1], preferred_element_type=F32)
        silu = gate * (1.0 / (1.0 + jnp.exp(-gate)))
        h_ref[:, c0:c1] = (silu * up).astype(BF16)
    y = jnp.dot(h_ref[...], wd_ref[...], preferred_element_type=F32)
    o_ref[...] = _layer_norm_rows(DEEPNORM_ALPHA * x + 0.5 * y, g_ref[...], b_ref[...])


def _ffn_ln(x2d, wg, wu, wd, g, b):
    n, d = x2d.shape
    d_ff = wg.shape[1]
    return pl.pallas_call(
        _ffn_ln_kernel,
        grid=(n // ROW_TILE,),
        in_specs=[pl.BlockSpec((ROW_TILE, d), lambda i: (i, 0)),
                  _resident((d, d_ff)), _resident((d, d_ff)), _resident((d_ff, d)),
                  _resident((1, d)), _resident((1, d))],
        out_specs=pl.BlockSpec((ROW_TILE, d), lambda i: (i, 0)),
        out_shape=jax.ShapeDtypeStruct((n, d), F32),
        scratch_shapes=[pltpu.VMEM((ROW_TILE, d_ff), BF16)],
        compiler_params=_cparams(1),
        name="ffn_ln",
    )(x2d, wg, wu, wd, g, b)


def _rope_tables(seq):
    r = NSA_HEAD_DIM // ROPE_FRACTION
    half = r // 2
    inv = ROPE_THETA ** (-jnp.arange(half, dtype=F32) * 2.0 / r)
    ang = jnp.arange(seq).astype(F32)[:, None] * inv[None, :]
    cos, sin = jnp.cos(ang), jnp.sin(ang)
    pad = jnp.zeros((seq, NSA_HEAD_DIM - r), F32)
    zero = jnp.zeros_like(sin)
    c_head = jnp.concatenate([cos, cos, pad + 1.0], axis=1)
    s_lo_head = jnp.concatenate([-sin, zero, pad], axis=1)
    s_hi_head = jnp.concatenate([zero, sin, pad], axis=1)
    rep = lambda t: jnp.concatenate([t, t], axis=1)
    return rep(c_head), rep(s_lo_head), rep(s_hi_head)


def _rope_chunk(x, c, s_lo, s_hi):
    half = NSA_HEAD_DIM // ROPE_FRACTION // 2
    return x * c + pltpu.roll(x, LANES - half, 1) * s_lo + pltpu.roll(x, half, 1) * s_hi


def _nsa_proj_kernel(x_ref, w_ref, c_ref, sl_ref, sh_ref,
                     qc_ref, qr_ref, kc_ref, vc_ref, ks_ref, kw_ref, vst_ref, vwt_ref, gt_ref):
    proj = jnp.dot(x_ref[...].astype(BF16), w_ref[...], preferred_element_type=F32)
    c, sl, sh = c_ref[...], sl_ref[...], sh_ref[...]
    col = lambda j: proj[:, j * LANES:(j + 1) * LANES]
    for j in range(N_CHUNKS):
        q = col(j) * (NSA_HEAD_DIM ** -0.5)
        qc_ref[j] = q.astype(BF16)
        qr_ref[j] = _rope_chunk(q, c, sl, sh).astype(BF16)
    base = N_CHUNKS
    kc_ref[...] = col(base)
    vc_ref[...] = col(base + 1)
    for g in range(NSA_KV_GROUPS):
        ks_ref[g] = _rope_chunk(col(base + 2 + g), c, sl, sh).astype(BF16)
        kw_ref[g] = _rope_chunk(col(base + 4 + g), c, sl, sh).astype(BF16)
    n_kb = x_ref.shape[0] // ATT_KEY_BLOCK
    for src, dst in ((base + 6, vst_ref), (base + 7, vwt_ref)):
        vt = col(src).T.astype(BF16)
        for g in range(NSA_KV_GROUPS):
            for kb in range(n_kb):
                dst[g, kb] = vt[g * NSA_HEAD_DIM:(g + 1) * NSA_HEAD_DIM,
                                kb * ATT_KEY_BLOCK:(kb + 1) * ATT_KEY_BLOCK]
    gates = col(base + 8)
    gt_ref[...] = (1.0 / (1.0 + jnp.exp(-gates))).T


def _nsa_proj_weight(w_in):
    d = D_MODEL
    kvw = NSA_KV_WIDTH
    q, kc, vc, ks, vs, kw, vw, gates = jnp.split(
        w_in, [int(c) for c in np.cumsum([d] + [kvw] * 6)], axis=1)
    dh = NSA_HEAD_DIM
    dup = lambda t: jnp.concatenate([t[:, :dh], t[:, :dh], t[:, dh:], t[:, dh:]], axis=1)
    gates = jnp.pad(gates, ((0, 0), (0, LANES - NSA_GATE_WIDTH)))
    return jnp.concatenate([q, kc, vc, dup(ks), dup(kw), vs, vw, gates], axis=1).astype(BF16)


def _nsa_proj(x, w, tables):
    b, s, d = x.shape
    ncol = w.shape[1]
    nkb = s // ATT_KEY_BLOCK
    kb_tile = ROW_TILE // ATT_KEY_BLOCK
    dh = NSA_HEAD_DIM
    tab_spec = pl.BlockSpec((ROW_TILE, LANES), lambda bi, i: (i, 0))
    chunk_spec = lambda n: pl.BlockSpec((None, n, ROW_TILE, LANES), lambda bi, i: (bi, 0, i, 0))
    row_spec = pl.BlockSpec((None, ROW_TILE, LANES), lambda bi, i: (bi, i, 0))
    vt_spec = pl.BlockSpec((None, NSA_KV_GROUPS, kb_tile, dh, ATT_KEY_BLOCK), lambda bi, i: (bi, 0, i, 0, 0))
    return pl.pallas_call(
        _nsa_proj_kernel,
        grid=(b, s // ROW_TILE),
        in_specs=[pl.BlockSpec((None, ROW_TILE, d), lambda bi, i: (bi, i, 0)),
                  _resident((d, ncol)), tab_spec, tab_spec, tab_spec],
        out_specs=[chunk_spec(N_CHUNKS), chunk_spec(N_CHUNKS), row_spec, row_spec,
                   chunk_spec(NSA_KV_GROUPS), chunk_spec(NSA_KV_GROUPS), vt_spec, vt_spec,
                   pl.BlockSpec((None, LANES, ROW_TILE), lambda bi, i: (bi, 0, i))],
        out_shape=[jax.ShapeDtypeStruct((b, N_CHUNKS, s, LANES), BF16),
                   jax.ShapeDtypeStruct((b, N_CHUNKS, s, LANES), BF16),
                   jax.ShapeDtypeStruct((b, s, LANES), F32),
                   jax.ShapeDtypeStruct((b, s, LANES), F32),
                   jax.ShapeDtypeStruct((b, NSA_KV_GROUPS, s, LANES), BF16),
                   jax.ShapeDtypeStruct((b, NSA_KV_GROUPS, s, LANES), BF16),
                   jax.ShapeDtypeStruct((b, NSA_KV_GROUPS, nkb, dh, ATT_KEY_BLOCK), BF16),
                   jax.ShapeDtypeStruct((b, NSA_KV_GROUPS, nkb, dh, ATT_KEY_BLOCK), BF16),
                   jax.ShapeDtypeStruct((b, LANES, s), F32)],
        compiler_params=_cparams(2),
        name="nsa_proj",
    )(x, w, *tables)


def _gelu_tanh(x):
    return 0.5 * x * (1.0 + jnp.tanh(np.sqrt(2.0 / np.pi).astype(np.float32) * (x + 0.044715 * (x * x * x))))


def _compress_kernel(kch_ref, vch_ref, pk_ref, pv_ref, w1k_ref, w1v_ref, b1k_ref, b1v_ref, w2k_ref, w2v_ref,
                     kcd_ref, vct_ref):
    n_rows = kch_ref.shape[0]

    def mlp(ch_ref, pos_ref, w1_ref, b1_ref, w2_ref):
        ch = ch_ref[...]
        lo = jnp.dot((ch + pos_ref[0:1, :]).astype(BF16), w1_ref[0], preferred_element_type=F32)
        hi = jnp.dot((ch + pos_ref[1:2, :]).astype(BF16), w1_ref[1], preferred_element_type=F32)
        hid = lo + pltpu.roll(hi, n_rows - 1, 0) + b1_ref[...]
        return jnp.dot(_gelu_tanh(hid).astype(BF16), w2_ref[...], preferred_element_type=F32)

    kcd = mlp(kch_ref, pk_ref, w1k_ref, b1k_ref, w2k_ref)
    for g in range(NSA_KV_GROUPS):
        kcd_ref[g] = kcd[:, g * LANES:(g + 1) * LANES].astype(BF16)
    vct = mlp(vch_ref, pv_ref, w1v_ref, b1v_ref, w2v_ref).T
    for g in range(NSA_KV_GROUPS):
        vct_ref[g] = vct[g * NSA_HEAD_DIM:(g + 1) * NSA_HEAD_DIM, :].astype(BF16)


def _compress_weights(pos, w1, b1, w2, dup_out):
    g_n, dh, hid = NSA_KV_GROUPS, NSA_HEAD_DIM, CMP_MLP_HIDDEN
    w1r = w1.reshape(2, CMP_STRIDE, dh, hid)
    eye = jnp.eye(g_n, dtype=w1.dtype)
    w1x = jnp.einsum('hldm,gk->hlgdkm', w1r, eye).reshape(2, CMP_STRIDE * g_n * dh, g_n * hid)
    posx = jnp.broadcast_to(pos.reshape(2, CMP_STRIDE, 1, dh), (2, CMP_STRIDE, g_n, dh)).reshape(2, -1)
    b1x = jnp.tile(b1, g_n)[None, :]
    reps = 2 if dup_out else 1
    w2x = jnp.einsum('md,gk->gmkd', w2, eye)
    w2x = jnp.broadcast_to(w2x[:, :, :, None, :], (g_n, hid, g_n, reps, dh)).reshape(g_n * hid, g_n * reps * dh)
    return posx, w1x.astype(BF16), b1x, w2x.astype(BF16)


def _compress(kc, vc, wk, wv):
    b, s, _ = kc.shape
    n_rows = s // CMP_STRIDE
    width = CMP_STRIDE * LANES
    kch = kc.reshape(b, n_rows, width)
    vch = vc.reshape(b, n_rows, width)
    pk, w1k, b1k, w2k = wk
    pv, w1v, b1v, w2v = wv
    ch_spec = pl.BlockSpec((None, n_rows, width), lambda bi: (bi, 0, 0))
    return pl.pallas_call(
        _compress_kernel,
        grid=(b,),
        in_specs=[ch_spec, ch_spec, _resident(pk.shape), _resident(pv.shape), _resident(w1k.shape),
                  _resident(w1v.shape), _resident(b1k.shape), _resident(b1v.shape), _resident(w2k.shape),
                  _resident(w2v.shape)],
        out_specs=[pl.BlockSpec((None, NSA_KV_GROUPS, n_rows, LANES), lambda bi: (bi, 0, 0, 0)),
                   pl.BlockSpec((None, NSA_KV_GROUPS, NSA_HEAD_DIM, n_rows), lambda bi: (bi, 0, 0, 0))],
        out_shape=[jax.ShapeDtypeStruct((b, NSA_KV_GROUPS, n_rows, LANES), BF16),
                   jax.ShapeDtypeStruct((b, NSA_KV_GROUPS, NSA_HEAD_DIM, n_rows), BF16)],
        compiler_params=_cparams(1),
        name="nsa_compress",
    )(kch, vch, pk, pv, w1k, w1v, b1k, b1v, w2k, w2v)


def _split_heads_rhs(q):
    lane = lax.broadcasted_iota(jnp.int32, q.shape, 1)
    zero = jnp.zeros_like(q)
    return jnp.concatenate([jnp.where(lane < LANES // 2, q, zero), jnp.where(lane >= LANES // 2, q, zero)], axis=0)


def _scores_t(k, rhs):
    return lax.dot_general(k, rhs, (((1,), (1,)), ((), ())), preferred_element_type=F32)


def _two(row):
    return jnp.concatenate([row, row], axis=1)


def _flash_update(s, vt_blk, m, l, acc_ref, idx):
    m_new = jnp.maximum(m, jnp.max(s, axis=0, keepdims=True))
    alpha = jnp.exp(m - m_new)
    p = jnp.exp(s - m_new)
    l = alpha * l + jnp.sum(p, axis=0, keepdims=True)
    acc_ref[idx] = alpha * acc_ref[idx] + jnp.dot(vt_blk, p.astype(BF16), preferred_element_type=F32)
    return m_new, l


def _nsa_attn_kernel(qc_ref, qr_ref, kcd_ref, vct_ref, ks_ref, kw_ref, vst_ref, vwt_ref, gt_ref, ov_ref,
                     o_ref, ocmp_ref, acc_ref, imp_ref, selb_ref):
    g = pl.program_id(1)
    qi = pl.program_id(2)
    tq = ATT_Q_TILE
    t0 = qi * tq
    n_pairs = qc_ref.shape[0]
    n_cmp = kcd_ref.shape[0]
    seq = ks_ref.shape[0]
    n_sel = seq // SLC_BLOCK
    tok1 = t0 + lax.broadcasted_iota(jnp.int32, (1, tq), 1)
    tok2 = _two(tok1)

    cmp_end = lax.broadcasted_iota(jnp.int32, (n_cmp, 1), 0) * CMP_STRIDE + (CMP_BLOCK - 1)
    vis = cmp_end <= tok2
    any_vis = (tok2 >= CMP_BLOCK - 1).astype(F32)
    psum = jnp.zeros((n_cmp, tq), F32)
    scores = [_scores_t(kcd_ref[...], _split_heads_rhs(qc_ref[c])) for c in range(n_pairs)]
    for c in range(n_pairs):
        s = jnp.where(vis, scores[c], NEG)
        m = jnp.max(s, axis=0, keepdims=True)
        e = jnp.exp(s - m)
        p = e / jnp.sum(e, axis=0, keepdims=True) * any_vis
        psum += p[:, :tq] + p[:, tq:]
        ocmp_ref[c] = jnp.dot(vct_ref[...], p.astype(BF16), preferred_element_type=F32)

    imp = jnp.dot(ov_ref[...], psum, precision=lax.Precision.HIGHEST, preferred_element_type=F32)
    blk = lax.broadcasted_iota(jnp.int32, (n_sel, 1), 0)
    cur = tok1 // SLC_BLOCK
    forced = (blk == 0) | (blk == cur) | (blk == cur - 1)
    valid = blk <= cur
    imp = jnp.where(valid, imp + jnp.where(forced, FORCE_BONUS, 0.0), NEG)
    imp_ref[...] = imp

    def rank_body(m, rank):
        row = imp_ref[pl.ds(m, 1), :]
        beats = (row > imp) | ((row == imp) & (blk > m))
        return rank + beats.astype(jnp.int32)

    rank = lax.fori_loop(0, n_sel, rank_body, jnp.zeros((n_sel, tq), jnp.int32))
    selb_ref[...] = jnp.where((rank < SLC_TOPK) & valid, 0.0, NEG)

    kb_sz = ATT_KEY_BLOCK
    last = (t0 + tq - 1) // kb_sz
    blocks_per_kb = kb_sz // SLC_BLOCK
    krow = lax.broadcasted_iota(jnp.int32, (kb_sz, 1), 0)
    rhs = [_split_heads_rhs(qr_ref[c]) for c in range(n_pairs)]
    acc_ref[...] = jnp.zeros_like(acc_ref)

    def sel_step(kb, st, causal):
        k0 = pl.multiple_of(kb * kb_sz, kb_sz)
        k_blk = ks_ref[pl.ds(k0, kb_sz), :]
        bias = jnp.concatenate(
            [jnp.broadcast_to(_two(selb_ref[pl.ds(kb * blocks_per_kb + j, 1), :]), (SLC_BLOCK, 2 * tq))
             for j in range(blocks_per_kb)], axis=0)
        if causal:
            bias = jnp.where(k0 + krow <= tok2, bias, NEG)
        scores = [_scores_t(k_blk, rhs[c]) + bias for c in range(n_pairs)]
        return tuple(_flash_update(scores[c], vst_ref[kb], st[c][0], st[c][1], acc_ref, c)
                     for c in range(n_pairs))

    init = tuple((jnp.full((1, 2 * tq), NEG, F32), jnp.zeros((1, 2 * tq), F32)) for _ in range(n_pairs))
    st = lax.fori_loop(0, last, lambda kb, st: sel_step(kb, st, False), init)
    st = sel_step(last, st, True)

    win_blocks = WINDOW // kb_sz + 1
    win_start = jnp.minimum(jnp.maximum(t0 - WINDOW, 0) // kb_sz, seq // kb_sz - win_blocks)
    w0 = pl.multiple_of(win_start * kb_sz, kb_sz)
    wrow = w0 + lax.broadcasted_iota(jnp.int32, (win_blocks * kb_sz, 1), 0)
    in_window = (wrow <= tok2) & (wrow > tok2 - WINDOW)
    k_win = kw_ref[pl.ds(w0, win_blocks * kb_sz), :]
    scores = [_scores_t(k_win, rhs[c]) for c in range(n_pairs)]
    for c in range(n_pairs):
        s = jnp.where(in_window, scores[c], NEG)
        m = jnp.max(s, axis=0, keepdims=True)
        e = jnp.exp(s - m)
        p = (e / jnp.sum(e, axis=0, keepdims=True)).astype(BF16)
        o_win = jnp.zeros((NSA_HEAD_DIM, 2 * tq), F32)
        for j in range(win_blocks):
            o_win += jnp.dot(vwt_ref[win_start + j], p[j * kb_sz:(j + 1) * kb_sz, :], preferred_element_type=F32)
        o_sel = acc_ref[c] / st[c][1]

        head = g * (NSA_HEADS // NSA_KV_GROUPS) + 2 * c
        gate = lambda j: jnp.concatenate([gt_ref[pl.ds(j * NSA_HEADS + head, 1), :],
                                          gt_ref[pl.ds(j * NSA_HEADS + head + 1, 1), :]], axis=1)
        o_t = gate(0) * ocmp_ref[c] + gate(1) * o_sel + gate(2) * o_win
        o_ref[c] = jnp.concatenate([o_t[:, :tq], o_t[:, tq:]], axis=0).T.astype(BF16)


def _cmp_to_slc_t(n_cmp_rows, n_sel):
    c0 = np.arange(n_cmp_rows)[None, :] * CMP_STRIDE
    s0 = np.arange(n_sel)[:, None] * SLC_BLOCK
    ov = np.minimum(c0 + CMP_BLOCK, s0 + SLC_BLOCK) - np.maximum(c0, s0)
    return (np.clip(ov, 0, None) / CMP_BLOCK).astype(np.float32)


def _nsa_attn(qc, qr, kcd, vct, ksd, kwd, vst, vwt, gt):
    b, n_chunks, s, _ = qc.shape
    groups = NSA_KV_GROUPS
    pairs = n_chunks // groups
    n_cmp = kcd.shape[2]
    nkb = s // ATT_KEY_BLOCK
    dh = NSA_HEAD_DIM
    tq = ATT_Q_TILE
    n_sel = s // SLC_BLOCK
    ov_t = jnp.asarray(_cmp_to_slc_t(n_cmp, n_sel))
    q_spec = pl.BlockSpec((None, pairs, tq, LANES), lambda bi, g, qi: (bi, g, qi, 0))
    per_group = lambda *shape: pl.BlockSpec((None, None) + shape, lambda bi, g, qi: (bi, g) + (0,) * len(shape))
    return pl.pallas_call(
        _nsa_attn_kernel,
        grid=(b, groups, s // tq),
        in_specs=[q_spec, q_spec, per_group(n_cmp, LANES), per_group(dh, n_cmp), per_group(s, LANES),
                  per_group(s, LANES), per_group(nkb, dh, ATT_KEY_BLOCK), per_group(nkb, dh, ATT_KEY_BLOCK),
                  pl.BlockSpec((None, LANES, tq), lambda bi, g, qi: (bi, 0, qi)),
                  _resident(ov_t.shape)],
        out_specs=q_spec,
        out_shape=jax.ShapeDtypeStruct((b, n_chunks, s, LANES), BF16),
        scratch_shapes=[pltpu.VMEM((pairs, dh, 2 * tq), F32), pltpu.VMEM((pairs, dh, 2 * tq), F32),
                        pltpu.VMEM((n_sel, tq), F32), pltpu.VMEM((n_sel, tq), F32)],
        compiler_params=_cparams(3),
        name="nsa_attn",
    )(qc, qr, kcd, vct, ksd, kwd, vst, vwt, gt, ov_t)


def _out_ln_kernel(o_ref, x_ref, w_ref, g_ref, b_ref, y_ref):
    o = jnp.concatenate([o_ref[j] for j in range(o_ref.shape[0])], axis=1)
    mix = jnp.dot(o, w_ref[...], preferred_element_type=F32)
    y_ref[...] = _layer_norm_rows(DEEPNORM_ALPHA * x_ref[...] + mix, g_ref[...], b_ref[...])


def _out_ln(o, x, w, g, b):
    bsz, n_chunks, s, _ = o.shape
    d = x.shape[-1]
    x_spec = pl.BlockSpec((None, ROW_TILE, d), lambda bi, i: (bi, i, 0))
    return pl.pallas_call(
        _out_ln_kernel,
        grid=(bsz, s // ROW_TILE),
        in_specs=[pl.BlockSpec((None, n_chunks, ROW_TILE, LANES), lambda bi, i: (bi, 0, i, 0)), x_spec,
                  _resident((d, d)), _resident((1, d)), _resident((1, d))],
        out_specs=x_spec,
        out_shape=jax.ShapeDtypeStruct(x.shape, F32),
        compiler_params=_cparams(2),
        name="mixer_out_ln",
    )(o, x, w, g, b)


def _pair_maps_columns(w):
    d = w.shape[0]
    return w.reshape(d, 2, DIFF_HEADS, DIFF_HEAD_DIM).transpose(0, 2, 1, 3).reshape(d, -1)


def _kv_proj_kernel(x_ref, w_ref, c_ref, sl_ref, sh_ref, k_ref, vt_ref):
    proj = jnp.dot(x_ref[...].astype(BF16), w_ref[...], preferred_element_type=F32)
    c, sl, sh = c_ref[...], sl_ref[...], sh_ref[...]
    n_kb = x_ref.shape[0] // ATT_KEY_BLOCK
    for h in range(DIFF_HEADS):
        k_ref[h] = _rope_chunk(proj[:, h * LANES:(h + 1) * LANES], c, sl, sh).astype(BF16)
        vt = proj[:, D_MODEL + h * DIFF_V_DIM:D_MODEL + (h + 1) * DIFF_V_DIM].T.astype(BF16)
        for kb in range(n_kb):
            vt_ref[h, kb] = vt[:, kb * ATT_KEY_BLOCK:(kb + 1) * ATT_KEY_BLOCK]


def _kv_proj(x, w, tables):
    b, s, d = x.shape
    nkb = s // ATT_KEY_BLOCK
    kb_tile = ROW_TILE // ATT_KEY_BLOCK
    tab_spec = pl.BlockSpec((ROW_TILE, LANES), lambda bi, i: (i, 0))
    return pl.pallas_call(
        _kv_proj_kernel,
        grid=(b, s // ROW_TILE),
        in_specs=[pl.BlockSpec((None, ROW_TILE, d), lambda bi, i: (bi, i, 0)), _resident(w.shape),
                  tab_spec, tab_spec, tab_spec],
        out_specs=[pl.BlockSpec((None, DIFF_HEADS, ROW_TILE, LANES), lambda bi, i: (bi, 0, i, 0)),
                   pl.BlockSpec((None, DIFF_HEADS, kb_tile, DIFF_V_DIM, ATT_KEY_BLOCK),
                                lambda bi, i: (bi, 0, i, 0, 0))],
        out_shape=[jax.ShapeDtypeStruct((b, DIFF_HEADS, s, LANES), BF16),
                   jax.ShapeDtypeStruct((b, DIFF_HEADS, nkb, DIFF_V_DIM, ATT_KEY_BLOCK), BF16)],
        compiler_params=_cparams(2),
        name="diff_kv_proj",
    )(x, w, *tables)


def _dq_proj_kernel(x_ref, w_ref, c_ref, sl_ref, sh_ref, q_ref):
    proj = jnp.dot(x_ref[...].astype(BF16), w_ref[...], preferred_element_type=F32)
    c, sl, sh = c_ref[...], sl_ref[...], sh_ref[...]
    for h in range(DIFF_HEADS):
        q = proj[:, h * LANES:(h + 1) * LANES] * (DIFF_HEAD_DIM ** -0.5)
        q_ref[h] = _rope_chunk(q, c, sl, sh).astype(BF16)


def _dq_proj(x, w, tables):
    b, s, d = x.shape
    tab_spec = pl.BlockSpec((ROW_TILE, LANES), lambda bi, i: (i, 0))
    return pl.pallas_call(
        _dq_proj_kernel,
        grid=(b, s // ROW_TILE),
        in_specs=[pl.BlockSpec((None, ROW_TILE, d), lambda bi, i: (bi, i, 0)), _resident(w.shape),
                  tab_spec, tab_spec, tab_spec],
        out_specs=pl.BlockSpec((None, DIFF_HEADS, ROW_TILE, LANES), lambda bi, i: (bi, 0, i, 0)),
        out_shape=jax.ShapeDtypeStruct((b, DIFF_HEADS, s, LANES), BF16),
        compiler_params=_cparams(2),
        name="diff_q_proj",
    )(x, w, *tables)


def _diff_attn_kernel(q_ref, k_ref, vt_ref, lam_ref, gain_ref, o_ref, acc_ref, *, lambda_init):
    qi = pl.program_id(2)
    n_heads = q_ref.shape[0]
    tq = ATT_Q_TILE
    t0 = qi * tq
    kb_sz = ATT_KEY_BLOCK
    tok2 = _two(t0 + lax.broadcasted_iota(jnp.int32, (1, tq), 1))
    krow = lax.broadcasted_iota(jnp.int32, (kb_sz, 1), 0)
    rhs = [_split_heads_rhs(q_ref[h]) for h in range(n_heads)]
    last = (t0 + tq - 1) // kb_sz
    acc_ref[...] = jnp.zeros_like(acc_ref)

    def step(kb, st, causal):
        k0 = pl.multiple_of(kb * kb_sz, kb_sz)
        scores = [_scores_t(k_ref[h, pl.ds(k0, kb_sz), :], rhs[h]) for h in range(n_heads)]
        if causal:
            visible = k0 + krow <= tok2
            scores = [jnp.where(visible, s, NEG) for s in scores]
        return tuple(_flash_update(scores[h], vt_ref[h, kb], st[h][0], st[h][1], acc_ref, h)
                     for h in range(n_heads))

    init = tuple((jnp.full((1, 2 * tq), NEG, F32), jnp.zeros((1, 2 * tq), F32)) for _ in range(n_heads))
    st = lax.fori_loop(0, last, lambda kb, st: step(kb, st, False), init)
    st = step(last, st, True)

    lp = lam_ref[...]
    lam = (jnp.exp(jnp.sum(lp[0:1] * lp[1:2], axis=1, keepdims=True))
           - jnp.exp(jnp.sum(lp[2:3] * lp[3:4], axis=1, keepdims=True)) + lambda_init)
    for h in range(n_heads):
        a = acc_ref[h] / st[h][1]
        o = a[:, :tq] - lam * a[:, tq:]
        ms = jnp.mean(o * o, axis=0, keepdims=True)
        o = o * lax.rsqrt(ms + NORM_EPS) * gain_ref[...] * (1.0 - lambda_init)
        o_ref[h] = o.T.astype(BF16)


def _diff_attn(q, k, vt, lam_params, gain, lambda_init):
    b, h, s, _ = q.shape
    nkb = s // ATT_KEY_BLOCK
    tq = ATT_Q_TILE
    hps = DIFF_HEADS_PER_STEP
    per_head = lambda *shape: pl.BlockSpec((None, hps) + shape, lambda bi, hi, qi: (bi, hi) + (0,) * len(shape))
    q_spec = pl.BlockSpec((None, hps, tq, LANES), lambda bi, hi, qi: (bi, hi, qi, 0))
    return pl.pallas_call(
        functools.partial(_diff_attn_kernel, lambda_init=lambda_init),
        grid=(b, h // hps, s // tq),
        in_specs=[q_spec, per_head(s, LANES), per_head(nkb, DIFF_V_DIM, ATT_KEY_BLOCK),
                  _resident(lam_params.shape), _resident(gain.shape)],
        out_specs=q_spec,
        out_shape=jax.ShapeDtypeStruct((b, h, s, LANES), BF16),
        scratch_shapes=[pltpu.VMEM((hps, DIFF_V_DIM, 2 * tq), F32)],
        compiler_params=_cparams(3),
        name="diff_attn",
    )(q, k, vt, lam_params, gain)


def kernel(x, ffn1_w_gate, ffn1_w_up, ffn1_w_down, ffn2_w_gate, ffn2_w_up, ffn2_w_down, ln_g, ln_b, a_w_in,
           a_cmp_pos_k, a_cmp_w1_k, a_cmp_b1_k, a_cmp_w2_k, a_cmp_pos_v, a_cmp_w1_v, a_cmp_b1_v, a_cmp_w2_v,
           a_w_out, b_w_kv_shared, b_w_q, b_lambda_q1, b_lambda_k1, b_lambda_q2, b_lambda_k2, b_subln_g, b_w_out):
    bsz, seq, d = x.shape
    assert d == D_MODEL and seq % ROW_TILE == 0 and seq >= WINDOW + 2 * ATT_KEY_BLOCK
    tables = _rope_tables(seq)
    bf = lambda w: w.astype(BF16)
    row = lambda v: v[None, :]

    def ffn(xv, wg, wu, wd, g, b):
        return _ffn_ln(xv.reshape(bsz * seq, d), bf(wg), bf(wu), bf(wd), row(g), row(b)).reshape(bsz, seq, d)

    k12 = vt = None
    for layer in range(DEPTH):
        x = ffn(x, ffn1_w_gate[layer], ffn1_w_up[layer], ffn1_w_down[layer], ln_g[layer, 0], ln_b[layer, 0])
        if layer < N_A_LAYERS:
            qc, qr, kc, vc, ksd, kwd, vst, vwt, gt = _nsa_proj(x, _nsa_proj_weight(a_w_in[layer]), tables)
            wk = _compress_weights(a_cmp_pos_k[layer], a_cmp_w1_k[layer], a_cmp_b1_k[layer], a_cmp_w2_k[layer], True)
            wv = _compress_weights(a_cmp_pos_v[layer], a_cmp_w1_v[layer], a_cmp_b1_v[layer], a_cmp_w2_v[layer], False)
            kcd, vct = _compress(kc, vc, wk, wv)
            o = _nsa_attn(qc, qr, kcd, vct, ksd, kwd, vst, vwt, gt)
            w_out = a_w_out[layer]
        else:
            j = layer - N_A_LAYERS
            lambda_init = 0.8 - 0.6 * float(np.exp(-0.3 * layer))
            q = _dq_proj(x, bf(_pair_maps_columns(b_w_q[j])), tables)
            lam_params = jnp.pad(jnp.stack([b_lambda_q1[j], b_lambda_k1[j], b_lambda_q2[j], b_lambda_k2[j]]),
                                 ((0, 4), (0, LANES - DIFF_HEAD_DIM)))
            gain = jnp.broadcast_to(b_subln_g[j][:, None], (DIFF_V_DIM, ATT_Q_TILE))
            o = _diff_attn(q, k12, vt, lam_params, gain, lambda_init)
            w_out = b_w_out[j]
        x = _out_ln(o, x, bf(w_out), row(ln_g[layer, 1]), row(ln_b[layer, 1]))
        x = ffn(x, ffn2_w_gate[layer], ffn2_w_up[layer], ffn2_w_down[layer], ln_g[layer, 2], ln_b[layer, 2])
        if layer == N_A_LAYERS - 1:
            w_kv = jnp.concatenate([_pair_maps_columns(b_w_kv_shared[:, :D_MODEL]), b_w_kv_shared[:, D_MODEL:]], axis=1)
            k12, vt = _kv_proj(x, bf(w_kv), tables)
    return x
```

```python
import functools

import numpy as np
import jax
import jax.numpy as jnp
from jax import lax
from jax.experimental import pallas as pl
from jax.experimental.pallas import tpu as pltpu

F32 = jnp.float32
BF16 = jnp.bfloat16

D_MODEL = 1024
DEPTH = 4
N_A_LAYERS = DEPTH // 2
NSA_HEADS = 16
NSA_HEAD_DIM = D_MODEL // NSA_HEADS
NSA_KV_GROUPS = 2
NSA_KV_WIDTH = NSA_KV_GROUPS * NSA_HEAD_DIM
NSA_GATE_WIDTH = 3 * NSA_HEADS
CMP_BLOCK = 32
CMP_STRIDE = 16
CMP_MLP_HIDDEN = 128
SLC_BLOCK = 64
SLC_TOPK = 16
WINDOW = 512
FORCE_BONUS = 1000.0
DIFF_HEADS = 8
DIFF_HEAD_DIM = D_MODEL // (2 * DIFF_HEADS)
DIFF_V_DIM = 2 * DIFF_HEAD_DIM
ROPE_THETA = 500000.0
ROPE_FRACTION = 4
D_FF = 2816
DEEPNORM_ALPHA = (2 * DEPTH) ** 0.25
NORM_EPS = 1e-5
NEG = -1e30

LANES = 128
ROW_TILE = 512
FFN_COL_CHUNK = 512
ATT_Q_TILE = 128
ATT_KEY_BLOCK = 256
DIFF_HEADS_PER_STEP = 4
VMEM_LIMIT_BYTES = 56 * 1024 * 1024
N_CHUNKS = D_MODEL // LANES


def _cparams(n_axes):
    return pltpu.CompilerParams(dimension_semantics=("parallel",) * n_axes,
                                vmem_limit_bytes=VMEM_LIMIT_BYTES)


def _resident(shape):
    zeros = (0,) * len(shape)
    return pl.BlockSpec(shape, lambda *_: zeros, pipeline_mode=pl.Buffered(1))


def _layer_norm_rows(r, g, b):
    mu = jnp.mean(r, axis=-1, keepdims=True)
    xc = r - mu
    var = jnp.mean(xc * xc, axis=-1, keepdims=True)
    return xc * lax.rsqrt(var + NORM_EPS) * g + b


def _ffn_ln_kernel(x_ref, wg_ref, wu_ref, wd_ref, g_ref, b_ref, o_ref, h_ref):
    x = x_ref[...]
    xb = x.astype(BF16)
    d_ff = wg_ref.shape[1]
    for c0 in range(0, d_ff, FFN_COL_CHUNK):
        c1 = min(c0 + FFN_COL_CHUNK, d_ff)
        gate = jnp.dot(xb, wg_ref[:, c0:c1], preferred_element_type=F32)
        up = jnp.dot(xb, wu_ref[:, c0:c1], preferred_element_type=F32)
        silu = gate * (1.0 / (1.0 + jnp.exp(-gate)))
        h_ref[:, c0:c1] = (silu * up).astype(BF16)
    y = jnp.dot(h_ref[...], wd_ref[...], preferred_element_type=F32)
    o_ref[...] = _layer_norm_rows(DEEPNORM_ALPHA * x + 0.5 * y, g_ref[...], b_ref[...])


def _ffn_ln(x2d, wg, wu, wd, g, b):
    n, d = x2d.shape
    d_ff = wg.shape[1]
    return pl.pallas_call(
        _ffn_ln_kernel,
        grid=(n // ROW_TILE,),
        in_specs=[pl.BlockSpec((ROW_TILE, d), lambda i: (i, 0)),
                  _resident((d, d_ff)), _resident((d, d_ff)), _resident((d_ff, d)),
                  _resident((1, d)), _resident((1, d))],
        out_specs=pl.BlockSpec((ROW_TILE, d), lambda i: (i, 0)),
        out_shape=jax.ShapeDtypeStruct((n, d), F32),
        scratch_shapes=[pltpu.VMEM((ROW_TILE, d_ff), BF16)],
        compiler_params=_cparams(1),
        name="ffn_ln",
    )(x2d, wg, wu, wd, g, b)


def _rope_tables(seq):
    r = NSA_HEAD_DIM // ROPE_FRACTION
    half = r // 2
    inv = ROPE_THETA ** (-jnp.arange(half, dtype=F32) * 2.0 / r)
    ang = jnp.arange(seq).astype(F32)[:, None] * inv[None, :]
    cos, sin = jnp.cos(ang), jnp.sin(ang)
    pad = jnp.zeros((seq, NSA_HEAD_DIM - r), F32)
    zero = jnp.zeros_like(sin)
    c_head = jnp.concatenate([cos, cos, pad + 1.0], axis=1)
    s_lo_head = jnp.concatenate([-sin, zero, pad], axis=1)
    s_hi_head = jnp.concatenate([zero, sin, pad], axis=1)
    rep = lambda t: jnp.concatenate([t, t], axis=1)
    return rep(c_head), rep(s_lo_head), rep(s_hi_head)


def _rope_chunk(x, c, s_lo, s_hi):
    half = NSA_HEAD_DIM // ROPE_FRACTION // 2
    return x * c + pltpu.roll(x, LANES - half, 1) * s_lo + pltpu.roll(x, half, 1) * s_hi


def _nsa_proj_kernel(x_ref, w_ref, c_ref, sl_ref, sh_ref,
                     qc_ref, qr_ref, kc_ref, vc_ref, ks_ref, kw_ref, vst_ref, vwt_ref, gt_ref):
    proj = jnp.dot(x_ref[...].astype(BF16), w_ref[...], preferred_element_type=F32)
    c, sl, sh = c_ref[...], sl_ref[...], sh_ref[...]
    col = lambda j: proj[:, j * LANES:(j + 1) * LANES]
    for j in range(N_CHUNKS):
        q = col(j) * (NSA_HEAD_DIM ** -0.5)
        qc_ref[j] = q.astype(BF16)
        qr_ref[j] = _rope_chunk(q, c, sl, sh).astype(BF16)
    base = N_CHUNKS
    kc_ref[...] = col(base)
    vc_ref[...] = col(base + 1)
    for g in range(NSA_KV_GROUPS):
        ks_ref[g] = _rope_chunk(col(base + 2 + g), c, sl, sh).astype(BF16)
        kw_ref[g] = _rope_chunk(col(base + 4 + g), c, sl, sh).astype(BF16)
    n_kb = x_ref.shape[0] // ATT_KEY_BLOCK
    for src, dst in ((base + 6, vst_ref), (base + 7, vwt_ref)):
        vt = col(src).T.astype(BF16)
        for g in range(NSA_KV_GROUPS):
            for kb in range(n_kb):
                dst[g, kb] = vt[g * NSA_HEAD_DIM:(g + 1) * NSA_HEAD_DIM,
                                kb * ATT_KEY_BLOCK:(kb + 1) * ATT_KEY_BLOCK]
    gates = col(base + 8)
    gt_ref[...] = (1.0 / (1.0 + jnp.exp(-gates))).T


def _nsa_proj_weight(w_in):
    d = D_MODEL
    kvw = NSA_KV_WIDTH
    q, kc, vc, ks, vs, kw, vw, gates = jnp.split(
        w_in, [int(c) for c in np.cumsum([d] + [kvw] * 6)], axis=1)
    dh = NSA_HEAD_DIM
    dup = lambda t: jnp.concatenate([t[:, :dh], t[:, :dh], t[:, dh:], t[:, dh:]], axis=1)
    gates = jnp.pad(gates, ((0, 0), (0, LANES - NSA_GATE_WIDTH)))
    return jnp.concatenate([q, kc, vc, dup(ks), dup(kw), vs, vw, gates], axis=1).astype(BF16)


def _nsa_proj(x, w, tables):
    b, s, d = x.shape
    ncol = w.shape[1]
    nkb = s // ATT_KEY_BLOCK
    kb_tile = ROW_TILE // ATT_KEY_BLOCK
    dh = NSA_HEAD_DIM
    tab_spec = pl.BlockSpec((ROW_TILE, LANES), lambda bi, i: (i, 0))
    chunk_spec = lambda n: pl.BlockSpec((None, n, ROW_TILE, LANES), lambda bi, i: (bi, 0, i, 0))
    row_spec = pl.BlockSpec((None, ROW_TILE, LANES), lambda bi, i: (bi, i, 0))
    vt_spec = pl.BlockSpec((None, NSA_KV_GROUPS, kb_tile, dh, ATT_KEY_BLOCK), lambda bi, i: (bi, 0, i, 0, 0))
    return pl.pallas_call(
        _nsa_proj_kernel,
        grid=(b, s // ROW_TILE),
        in_specs=[pl.BlockSpec((None, ROW_TILE, d), lambda bi, i: (bi, i, 0)),
                  _resident((d, ncol)), tab_spec, tab_spec, tab_spec],
        out_specs=[chunk_spec(N_CHUNKS), chunk_spec(N_CHUNKS), row_spec, row_spec,
                   chunk_spec(NSA_KV_GROUPS), chunk_spec(NSA_KV_GROUPS), vt_spec, vt_spec,
                   pl.BlockSpec((None, LANES, ROW_TILE), lambda bi, i: (bi, 0, i))],
        out_shape=[jax.ShapeDtypeStruct((b, N_CHUNKS, s, LANES), BF16),
                   jax.ShapeDtypeStruct((b, N_CHUNKS, s, LANES), BF16),
                   jax.ShapeDtypeStruct((b, s, LANES), F32),
                   jax.ShapeDtypeStruct((b, s, LANES), F32),
                   jax.ShapeDtypeStruct((b, NSA_KV_GROUPS, s, LANES), BF16),
                   jax.ShapeDtypeStruct((b, NSA_KV_GROUPS, s, LANES), BF16),
                   jax.ShapeDtypeStruct((b, NSA_KV_GROUPS, nkb, dh, ATT_KEY_BLOCK), BF16),
                   jax.ShapeDtypeStruct((b, NSA_KV_GROUPS, nkb, dh, ATT_KEY_BLOCK), BF16),
                   jax.ShapeDtypeStruct((b, LANES, s), F32)],
        compiler_params=_cparams(2),
        name="nsa_proj",
    )(x, w, *tables)


def _gelu_tanh(x):
    return 0.5 * x * (1.0 + jnp.tanh(np.sqrt(2.0 / np.pi).astype(np.float32) * (x + 0.044715 * (x * x * x))))


def _compress_kernel(kch_ref, vch_ref, pk_ref, pv_ref, w1k_ref, w1v_ref, b1k_ref, b1v_ref, w2k_ref, w2v_ref,
                     kcd_ref, vct_ref):
    n_rows = kch_ref.shape[0]

    def mlp(ch_ref, pos_ref, w1_ref, b1_ref, w2_ref):
        ch = ch_ref[...]
        lo = jnp.dot((ch + pos_ref[0:1, :]).astype(BF16), w1_ref[0], preferred_element_type=F32)
        hi = jnp.dot((ch + pos_ref[1:2, :]).astype(BF16), w1_ref[1], preferred_element_type=F32)
        hid = lo + pltpu.roll(hi, n_rows - 1, 0) + b1_ref[...]
        return jnp.dot(_gelu_tanh(hid).astype(BF16), w2_ref[...], preferred_element_type=F32)

    kcd = mlp(kch_ref, pk_ref, w1k_ref, b1k_ref, w2k_ref)
    for g in range(NSA_KV_GROUPS):
        kcd_ref[g] = kcd[:, g * LANES:(g + 1) * LANES].astype(BF16)
    vct = mlp(vch_ref, pv_ref, w1v_ref, b1v_ref, w2v_ref).T
    for g in range(NSA_KV_GROUPS):
        vct_ref[g] = vct[g * NSA_HEAD_DIM:(g + 1) * NSA_HEAD_DIM, :].astype(BF16)


def _compress_weights(pos, w1, b1, w2, dup_out):
    g_n, dh, hid = NSA_KV_GROUPS, NSA_HEAD_DIM, CMP_MLP_HIDDEN
    w1r = w1.reshape(2, CMP_STRIDE, dh, hid)
    eye = jnp.eye(g_n, dtype=w1.dtype)
    w1x = jnp.einsum('hldm,gk->hlgdkm', w1r, eye).reshape(2, CMP_STRIDE * g_n * dh, g_n * hid)
    posx = jnp.broadcast_to(pos.reshape(2, CMP_STRIDE, 1, dh), (2, CMP_STRIDE, g_n, dh)).reshape(2, -1)
    b1x = jnp.tile(b1, g_n)[None, :]
    reps = 2 if dup_out else 1
    w2x = jnp.einsum('md,gk->gmkd', w2, eye)
    w2x = jnp.broadcast_to(w2x[:, :, :, None, :], (g_n, hid, g_n, reps, dh)).reshape(g_n * hid, g_n * reps * dh)
    return posx, w1x.astype(BF16), b1x, w2x.astype(BF16)


def _compress(kc, vc, wk, wv):
    b, s, _ = kc.shape
    n_rows = s // CMP_STRIDE
    width = CMP_STRIDE * LANES
    kch = kc.reshape(b, n_rows, width)
    vch = vc.reshape(b, n_rows, width)
    pk, w1k, b1k, w2k = wk
    pv, w1v, b1v, w2v = wv
    ch_spec = pl.BlockSpec((None, n_rows, width), lambda bi: (bi, 0, 0))
    return pl.pallas_call(
        _compress_kernel,
        grid=(b,),
        in_specs=[ch_spec, ch_spec, _resident(pk.shape), _resident(pv.shape), _resident(w1k.shape),
                  _resident(w1v.shape), _resident(b1k.shape), _resident(b1v.shape), _resident(w2k.shape),
                  _resident(w2v.shape)],
        out_specs=[pl.BlockSpec((None, NSA_KV_GROUPS, n_rows, LANES), lambda bi: (bi, 0, 0, 0)),
                   pl.BlockSpec((None, NSA_KV_GROUPS, NSA_HEAD_DIM, n_rows), lambda bi: (bi, 0, 0, 0))],
        out_shape=[jax.ShapeDtypeStruct((b, NSA_KV_GROUPS, n_rows, LANES), BF16),
                   jax.ShapeDtypeStruct((b, NSA_KV_GROUPS, NSA_HEAD_DIM, n_rows), BF16)],
        compiler_params=_cparams(1),
        name="nsa_compress",
    )(kch, vch, pk, pv, w1k, w1v, b1k, b1v, w2k, w2v)


def _split_heads_rhs(q):
    lane = lax.broadcasted_iota(jnp.int32, q.shape, 1)
    zero = jnp.zeros_like(q)
    return jnp.concatenate([jnp.where(lane < LANES // 2, q, zero), jnp.where(lane >= LANES // 2, q, zero)], axis=0)


def _scores_t(k, rhs):
    return lax.dot_general(k, rhs, (((1,), (1,)), ((), ())), preferred_element_type=F32)


def _two(row):
    return jnp.concatenate([row, row], axis=1)


def _flash_update(s, vt_blk, m, l, acc_ref, idx):
    m_new = jnp.maximum(m, jnp.max(s, axis=0, keepdims=True))
    alpha = jnp.exp(m - m_new)
    p = jnp.exp(s - m_new)
    l = alpha * l + jnp.sum(p, axis=0, keepdims=True)
    acc_ref[idx] = alpha * acc_ref[idx] + jnp.dot(vt_blk, p.astype(BF16), preferred_element_type=F32)
    return m_new, l


def _nsa_attn_kernel(qc_ref, qr_ref, kcd_ref, vct_ref, ks_ref, kw_ref, vst_ref, vwt_ref, gt_ref, ov_ref,
                     o_ref, ocmp_ref, acc_ref, selb_ref, sa_ref, sb_ref):
    g = pl.program_id(1)
    qi = pl.program_id(2)
    tq = ATT_Q_TILE
    t0 = qi * tq
    n_pairs = qc_ref.shape[0]
    n_cmp = kcd_ref.shape[0]
    seq = ks_ref.shape[0]
    n_sel = seq // SLC_BLOCK
    tok1 = t0 + lax.broadcasted_iota(jnp.int32, (1, tq), 1)
    tok2 = _two(tok1)

    cmp_end = lax.broadcasted_iota(jnp.int32, (n_cmp, 1), 0) * CMP_STRIDE + (CMP_BLOCK - 1)
    vis = cmp_end <= tok2
    any_vis = (tok2 >= CMP_BLOCK - 1).astype(F32)
    psum = jnp.zeros((n_cmp, tq), F32)
    scores = [_scores_t(kcd_ref[...], _split_heads_rhs(qc_ref[c])) for c in range(n_pairs)]
    for c in range(n_pairs):
        s = jnp.where(vis, scores[c], NEG)
        m = jnp.max(s, axis=0, keepdims=True)
        e = jnp.exp(s - m)
        p = e / jnp.sum(e, axis=0, keepdims=True) * any_vis
        psum += p[:, :tq] + p[:, tq:]
        ocmp_ref[c] = jnp.dot(vct_ref[...], p.astype(BF16), preferred_element_type=F32)

    imp = jnp.dot(ov_ref[...], psum, precision=lax.Precision.HIGHEST, preferred_element_type=F32)
    blk = lax.broadcasted_iota(jnp.int32, (n_sel, 1), 0)
    cur = tok1 // SLC_BLOCK
    forced = (blk == 0) | (blk == cur) | (blk == cur - 1)
    valid = blk <= cur
    imp = jnp.where(valid, imp + jnp.where(forced, FORCE_BONUS, 0.0), NEG)
    rows8 = 8
    grp = [imp[rows8 * j:rows8 * (j + 1), :] for j in range(n_sel // rows8)]
    sub = lax.broadcasted_iota(jnp.int32, (rows8, 1), 0)
    rank = [jnp.zeros((rows8, tq), jnp.int32) for _ in grp]
    for m in range(n_sel):
        row = imp[m:m + 1, :]
        for j in range(len(grp)):
            ge = lambda: (row >= grp[j]).astype(jnp.int32)
            gt = lambda: (row > grp[j]).astype(jnp.int32)
            if rows8 * j > m:
                beats = ge()
            elif rows8 * j + rows8 - 1 <= m:
                beats = gt()
            else:
                beats = jnp.where(sub + rows8 * j > m, ge(), gt())
            rank[j] = rank[j] + beats
    rank = jnp.concatenate(rank, axis=0)
    selb_ref[...] = jnp.where((rank < SLC_TOPK) & valid, 0.0, NEG)

    kb_sz = ATT_KEY_BLOCK
    last = (t0 + tq - 1) // kb_sz
    blocks_per_kb = kb_sz // SLC_BLOCK
    krow = lax.broadcasted_iota(jnp.int32, (kb_sz, 1), 0)
    rhs = [_split_heads_rhs(qr_ref[c]) for c in range(n_pairs)]
    acc_ref[...] = jnp.zeros_like(acc_ref)

    def sel_scores(kb, dst_ref):
        k_blk = ks_ref[pl.ds(pl.multiple_of(kb * kb_sz, kb_sz), kb_sz), :]
        for c in range(n_pairs):
            dst_ref[c] = _scores_t(k_blk, rhs[c])

    def sel_consume(kb, src_ref, st):
        bias = jnp.concatenate(
            [jnp.broadcast_to(_two(selb_ref[pl.ds(kb * blocks_per_kb + j, 1), :]), (SLC_BLOCK, 2 * tq))
             for j in range(blocks_per_kb)], axis=0)
        bias = jnp.where(kb * kb_sz + krow <= tok2, bias, NEG)
        return tuple(_flash_update(src_ref[c] + bias, vst_ref[kb], st[c][0], st[c][1], acc_ref, c)
                     for c in range(n_pairs))

    def sel_two_blocks(i, st):
        kb = 2 * i
        sel_scores(kb + 1, sb_ref)
        st = sel_consume(kb, sa_ref, st)
        sel_scores(jnp.minimum(kb + 2, last), sa_ref)
        return sel_consume(kb + 1, sb_ref, st)

    init = tuple((jnp.full((1, 2 * tq), NEG, F32), jnp.zeros((1, 2 * tq), F32)) for _ in range(n_pairs))
    sel_scores(0, sa_ref)
    st = lax.fori_loop(0, (last + 1) // 2, sel_two_blocks, init)
    st = lax.cond(last % 2 == 0, lambda st: sel_consume(last, sa_ref, st), lambda st: st, st)

    win_blocks = WINDOW // kb_sz + 1
    win_start = jnp.minimum(jnp.maximum(t0 - WINDOW, 0) // kb_sz, seq // kb_sz - win_blocks)
    w0 = pl.multiple_of(win_start * kb_sz, kb_sz)
    wrow = w0 + lax.broadcasted_iota(jnp.int32, (win_blocks * kb_sz, 1), 0)
    in_window = (wrow <= tok2) & (wrow > tok2 - WINDOW)
    k_win = kw_ref[pl.ds(w0, win_blocks * kb_sz), :]
    scores = [_scores_t(k_win, rhs[c]) for c in range(n_pairs)]
    for c in range(n_pairs):
        s = jnp.where(in_window, scores[c], NEG)
        m = jnp.max(s, axis=0, keepdims=True)
        e = jnp.exp(s - m)
        p = (e / jnp.sum(e, axis=0, keepdims=True)).astype(BF16)
        o_win = jnp.zeros((NSA_HEAD_DIM, 2 * tq), F32)
        for j in range(win_blocks):
            o_win += jnp.dot(vwt_ref[win_start + j], p[j * kb_sz:(j + 1) * kb_sz, :], preferred_element_type=F32)
        o_sel = acc_ref[c] / st[c][1]

        head = g * (NSA_HEADS // NSA_KV_GROUPS) + 2 * c
        gate = lambda j: jnp.concatenate([gt_ref[pl.ds(j * NSA_HEADS + head, 1), :],
                                          gt_ref[pl.ds(j * NSA_HEADS + head + 1, 1), :]], axis=1)
        o_t = gate(0) * ocmp_ref[c] + gate(1) * o_sel + gate(2) * o_win
        o_ref[c] = jnp.concatenate([o_t[:, :tq], o_t[:, tq:]], axis=0).T.astype(BF16)


def _cmp_to_slc_t(n_cmp_rows, n_sel):
    c0 = np.arange(n_cmp_rows)[None, :] * CMP_STRIDE
    s0 = np.arange(n_sel)[:, None] * SLC_BLOCK
    ov = np.minimum(c0 + CMP_BLOCK, s0 + SLC_BLOCK) - np.maximum(c0, s0)
    return (np.clip(ov, 0, None) / CMP_BLOCK).astype(np.float32)


def _nsa_attn(qc, qr, kcd, vct, ksd, kwd, vst, vwt, gt):
    b, n_chunks, s, _ = qc.shape
    groups = NSA_KV_GROUPS
    pairs = n_chunks // groups
    n_cmp = kcd.shape[2]
    nkb = s // ATT_KEY_BLOCK
    dh = NSA_HEAD_DIM
    tq = ATT_Q_TILE
    n_sel = s // SLC_BLOCK
    ov_t = jnp.asarray(_cmp_to_slc_t(n_cmp, n_sel))
    q_spec = pl.BlockSpec((None, pairs, tq, LANES), lambda bi, g, qi: (bi, g, qi, 0))
    per_group = lambda *shape: pl.BlockSpec((None, None) + shape, lambda bi, g, qi: (bi, g) + (0,) * len(shape))
    return pl.pallas_call(
        _nsa_attn_kernel,
        grid=(b, groups, s // tq),
        in_specs=[q_spec, q_spec, per_group(n_cmp, LANES), per_group(dh, n_cmp), per_group(s, LANES),
                  per_group(s, LANES), per_group(nkb, dh, ATT_KEY_BLOCK), per_group(nkb, dh, ATT_KEY_BLOCK),
                  pl.BlockSpec((None, LANES, tq), lambda bi, g, qi: (bi, 0, qi)),
                  _resident(ov_t.shape)],
        out_specs=q_spec,
        out_shape=jax.ShapeDtypeStruct((b, n_chunks, s, LANES), BF16),
        scratch_shapes=[pltpu.VMEM((pairs, dh, 2 * tq), F32), pltpu.VMEM((pairs, dh, 2 * tq), F32),
                        pltpu.VMEM((n_sel, tq), F32),
                        pltpu.VMEM((pairs, ATT_KEY_BLOCK, 2 * tq), F32), pltpu.VMEM((pairs, ATT_KEY_BLOCK, 2 * tq), F32)],
        compiler_params=_cparams(3),
        name="nsa_attn",
    )(qc, qr, kcd, vct, ksd, kwd, vst, vwt, gt, ov_t)


def _out_ln_kernel(o_ref, x_ref, w_ref, g_ref, b_ref, y_ref):
    o = jnp.concatenate([o_ref[j] for j in range(o_ref.shape[0])], axis=1)
    mix = jnp.dot(o, w_ref[...], preferred_element_type=F32)
    y_ref[...] = _layer_norm_rows(DEEPNORM_ALPHA * x_ref[...] + mix, g_ref[...], b_ref[...])


def _out_ln(o, x, w, g, b):
    bsz, n_chunks, s, _ = o.shape
    d = x.shape[-1]
    x_spec = pl.BlockSpec((None, ROW_TILE, d), lambda bi, i: (bi, i, 0))
    return pl.pallas_call(
        _out_ln_kernel,
        grid=(bsz, s // ROW_TILE),
        in_specs=[pl.BlockSpec((None, n_chunks, ROW_TILE, LANES), lambda bi, i: (bi, 0, i, 0)), x_spec,
                  _resident((d, d)), _resident((1, d)), _resident((1, d))],
        out_specs=x_spec,
        out_shape=jax.ShapeDtypeStruct(x.shape, F32),
        compiler_params=_cparams(2),
        name="mixer_out_ln",
    )(o, x, w, g, b)


def _pair_maps_columns(w):
    d = w.shape[0]
    return w.reshape(d, 2, DIFF_HEADS, DIFF_HEAD_DIM).transpose(0, 2, 1, 3).reshape(d, -1)


def _kv_proj_kernel(x_ref, w_ref, c_ref, sl_ref, sh_ref, k_ref, vt_ref):
    proj = jnp.dot(x_ref[...].astype(BF16), w_ref[...], preferred_element_type=F32)
    c, sl, sh = c_ref[...], sl_ref[...], sh_ref[...]
    n_kb = x_ref.shape[0] // ATT_KEY_BLOCK
    for h in range(DIFF_HEADS):
        k_ref[h] = _rope_chunk(proj[:, h * LANES:(h + 1) * LANES], c, sl, sh).astype(BF16)
        vt = proj[:, D_MODEL + h * DIFF_V_DIM:D_MODEL + (h + 1) * DIFF_V_DIM].T.astype(BF16)
        for kb in range(n_kb):
            vt_ref[h, kb] = vt[:, kb * ATT_KEY_BLOCK:(kb + 1) * ATT_KEY_BLOCK]


def _kv_proj(x, w, tables):
    b, s, d = x.shape
    nkb = s // ATT_KEY_BLOCK
    kb_tile = ROW_TILE // ATT_KEY_BLOCK
    tab_spec = pl.BlockSpec((ROW_TILE, LANES), lambda bi, i: (i, 0))
    return pl.pallas_call(
        _kv_proj_kernel,
        grid=(b, s // ROW_TILE),
        in_specs=[pl.BlockSpec((None, ROW_TILE, d), lambda bi, i: (bi, i, 0)), _resident(w.shape),
                  tab_spec, tab_spec, tab_spec],
        out_specs=[pl.BlockSpec((None, DIFF_HEADS, ROW_TILE, LANES), lambda bi, i: (bi, 0, i, 0)),
                   pl.BlockSpec((None, DIFF_HEADS, kb_tile, DIFF_V_DIM, ATT_KEY_BLOCK),
                                lambda bi, i: (bi, 0, i, 0, 0))],
        out_shape=[jax.ShapeDtypeStruct((b, DIFF_HEADS, s, LANES), BF16),
                   jax.ShapeDtypeStruct((b, DIFF_HEADS, nkb, DIFF_V_DIM, ATT_KEY_BLOCK), BF16)],
        compiler_params=_cparams(2),
        name="diff_kv_proj",
    )(x, w, *tables)


def _dq_proj_kernel(x_ref, w_ref, c_ref, sl_ref, sh_ref, q_ref):
    proj = jnp.dot(x_ref[...].astype(BF16), w_ref[...], preferred_element_type=F32)
    c, sl, sh = c_ref[...], sl_ref[...], sh_ref[...]
    for h in range(DIFF_HEADS):
        q = proj[:, h * LANES:(h + 1) * LANES] * (DIFF_HEAD_DIM ** -0.5)
        q_ref[h] = _rope_chunk(q, c, sl, sh).astype(BF16)


def _dq_proj(x, w, tables):
    b, s, d = x.shape
    tab_spec = pl.BlockSpec((ROW_TILE, LANES), lambda bi, i: (i, 0))
    return pl.pallas_call(
        _dq_proj_kernel,
        grid=(b, s // ROW_TILE),
        in_specs=[pl.BlockSpec((None, ROW_TILE, d), lambda bi, i: (bi, i, 0)), _resident(w.shape),
                  tab_spec, tab_spec, tab_spec],
        out_specs=pl.BlockSpec((None, DIFF_HEADS, ROW_TILE, LANES), lambda bi, i: (bi, 0, i, 0)),
        out_shape=jax.ShapeDtypeStruct((b, DIFF_HEADS, s, LANES), BF16),
        compiler_params=_cparams(2),
        name="diff_q_proj",
    )(x, w, *tables)


def _diff_attn_kernel(q_ref, k_ref, vt_ref, lam_ref, gain_ref, o_ref, acc_ref, sa_ref, sb_ref, *, lambda_init):
    qi = pl.program_id(2)
    n_heads = q_ref.shape[0]
    tq = ATT_Q_TILE
    t0 = qi * tq
    kb_sz = ATT_KEY_BLOCK
    tok2 = _two(t0 + lax.broadcasted_iota(jnp.int32, (1, tq), 1))
    krow = lax.broadcasted_iota(jnp.int32, (kb_sz, 1), 0)
    rhs = [_split_heads_rhs(q_ref[h]) for h in range(n_heads)]
    last = (t0 + tq - 1) // kb_sz
    acc_ref[...] = jnp.zeros_like(acc_ref)

    def scores_into(kb, dst_ref):
        k0 = pl.multiple_of(kb * kb_sz, kb_sz)
        for h in range(n_heads):
            dst_ref[h] = _scores_t(k_ref[h, pl.ds(k0, kb_sz), :], rhs[h])

    def consume(kb, src_ref, st, maybe_diagonal):
        if maybe_diagonal:
            visible = kb * kb_sz + krow <= tok2
        out = []
        for h in range(n_heads):
            s = src_ref[h]
            if maybe_diagonal:
                s = jnp.where(visible, s, NEG)
            out.append(_flash_update(s, vt_ref[h, kb], st[h][0], st[h][1], acc_ref, h))
        return tuple(out)

    def two_blocks(i, st):
        kb = 2 * i
        scores_into(kb + 1, sb_ref)
        st = consume(kb, sa_ref, st, False)
        scores_into(jnp.minimum(kb + 2, last), sa_ref)
        return consume(kb + 1, sb_ref, st, True)

    init = tuple((jnp.full((1, 2 * tq), NEG, F32), jnp.zeros((1, 2 * tq), F32)) for _ in range(n_heads))
    scores_into(0, sa_ref)
    st = lax.fori_loop(0, (last + 1) // 2, two_blocks, init)
    st = lax.cond(last % 2 == 0, lambda st: consume(last, sa_ref, st, True), lambda st: st, st)

    lp = lam_ref[...]
    lam = (jnp.exp(jnp.sum(lp[0:1] * lp[1:2], axis=1, keepdims=True))
           - jnp.exp(jnp.sum(lp[2:3] * lp[3:4], axis=1, keepdims=True)) + lambda_init)
    for h in range(n_heads):
        a = acc_ref[h] / st[h][1]
        o = a[:, :tq] - lam * a[:, tq:]
        ms = jnp.mean(o * o, axis=0, keepdims=True)
        o = o * lax.rsqrt(ms + NORM_EPS) * gain_ref[...] * (1.0 - lambda_init)
        o_ref[h] = o.T.astype(BF16)


def _diff_attn(q, k, vt, lam_params, gain, lambda_init):
    b, h, s, _ = q.shape
    nkb = s // ATT_KEY_BLOCK
    tq = ATT_Q_TILE
    hps = DIFF_HEADS_PER_STEP
    per_head = lambda *shape: pl.BlockSpec((None, hps) + shape, lambda bi, hi, qi: (bi, hi) + (0,) * len(shape))
    q_spec = pl.BlockSpec((None, hps, tq, LANES), lambda bi, hi, qi: (bi, hi, qi, 0))
    return pl.pallas_call(
        functools.partial(_diff_attn_kernel, lambda_init=lambda_init),
        grid=(b, h // hps, s // tq),
        in_specs=[q_spec, per_head(s, LANES), per_head(nkb, DIFF_V_DIM, ATT_KEY_BLOCK),
                  _resident(lam_params.shape), _resident(gain.shape)],
        out_specs=q_spec,
        out_shape=jax.ShapeDtypeStruct((b, h, s, LANES), BF16),
        scratch_shapes=[pltpu.VMEM((hps, DIFF_V_DIM, 2 * tq), F32),
                        pltpu.VMEM((hps, ATT_KEY_BLOCK, 2 * tq), F32), pltpu.VMEM((hps, ATT_KEY_BLOCK, 2 * tq), F32)],
        compiler_params=_cparams(3),
        name="diff_attn",
    )(q, k, vt, lam_params, gain)


def kernel(x, ffn1_w_gate, ffn1_w_up, ffn1_w_down, ffn2_w_gate, ffn2_w_up, ffn2_w_down, ln_g, ln_b, a_w_in,
           a_cmp_pos_k, a_cmp_w1_k, a_cmp_b1_k, a_cmp_w2_k, a_cmp_pos_v, a_cmp_w1_v, a_cmp_b1_v, a_cmp_w2_v,
           a_w_out, b_w_kv_shared, b_w_q, b_lambda_q1, b_lambda_k1, b_lambda_q2, b_lambda_k2, b_subln_g, b_w_out):
    bsz, seq, d = x.shape
    assert d == D_MODEL and seq % ROW_TILE == 0 and seq >= WINDOW + 2 * ATT_KEY_BLOCK
    tables = _rope_tables(seq)
    bf = lambda w: w.astype(BF16)
    row = lambda v: v[None, :]

    def ffn(xv, wg, wu, wd, g, b):
        return _ffn_ln(xv.reshape(bsz * seq, d), bf(wg), bf(wu), bf(wd), row(g), row(b)).reshape(bsz, seq, d)

    k12 = vt = None
    for layer in range(DEPTH):
        x = ffn(x, ffn1_w_gate[layer], ffn1_w_up[layer], ffn1_w_down[layer], ln_g[layer, 0], ln_b[layer, 0])
        if layer < N_A_LAYERS:
            qc, qr, kc, vc, ksd, kwd, vst, vwt, gt = _nsa_proj(x, _nsa_proj_weight(a_w_in[layer]), tables)
            wk = _compress_weights(a_cmp_pos_k[layer], a_cmp_w1_k[layer], a_cmp_b1_k[layer], a_cmp_w2_k[layer], True)
            wv = _compress_weights(a_cmp_pos_v[layer], a_cmp_w1_v[layer], a_cmp_b1_v[layer], a_cmp_w2_v[layer], False)
            kcd, vct = _compress(kc, vc, wk, wv)
            o = _nsa_attn(qc, qr, kcd, vct, ksd, kwd, vst, vwt, gt)
            w_out = a_w_out[layer]
        else:
            j = layer - N_A_LAYERS
            lambda_init = 0.8 - 0.6 * float(np.exp(-0.3 * layer))
            q = _dq_proj(x, bf(_pair_maps_columns(b_w_q[j])), tables)
            lam_params = jnp.pad(jnp.stack([b_lambda_q1[j], b_lambda_k1[j], b_lambda_q2[j], b_lambda_k2[j]]),
                                 ((0, 4), (0, LANES - DIFF_HEAD_DIM)))
            gain = jnp.broadcast_to(b_subln_g[j][:, None], (DIFF_V_DIM, ATT_Q_TILE))
            o = _diff_attn(q, k12, vt, lam_params, gain, lambda_init)
            w_out = b_w_out[j]
        x = _out_ln(o, x, bf(w_out), row(ln_g[layer, 1]), row(ln_b[layer, 1]))
        x = ffn(x, ffn2_w_gate[layer], ffn2_w_up[layer], ffn2_w_down[layer], ln_g[layer, 2], ln_b[layer, 2])
        if layer == N_A_LAYERS - 1:
            w_kv = jnp.concatenate([_pair_maps_columns(b_w_kv_shared[:, :D_MODEL]), b_w_kv_shared[:, D_MODEL:]], axis=1)
            k12, vt = _kv_proj(x, bf(w_kv), tables)
    return x
```

```python
import functools

import numpy as np
import jax
import jax.numpy as jnp
from jax import lax
from jax.experimental import pallas as pl
from jax.experimental.pallas import tpu as pltpu

F32 = jnp.float32
BF16 = jnp.bfloat16

D_MODEL = 1024
DEPTH = 4
N_A_LAYERS = DEPTH // 2
NSA_HEADS = 16
NSA_HEAD_DIM = D_MODEL // NSA_HEADS
NSA_KV_GROUPS = 2
NSA_KV_WIDTH = NSA_KV_GROUPS * NSA_HEAD_DIM
NSA_GATE_WIDTH = 3 * NSA_HEADS
CMP_BLOCK = 32
CMP_STRIDE = 16
CMP_MLP_HIDDEN = 128
SLC_BLOCK = 64
SLC_TOPK = 16
WINDOW = 512
FORCE_BONUS = 1000.0
DIFF_HEADS = 8
DIFF_HEAD_DIM = D_MODEL // (2 * DIFF_HEADS)
DIFF_V_DIM = 2 * DIFF_HEAD_DIM
ROPE_THETA = 500000.0
ROPE_FRACTION = 4
D_FF = 2816
DEEPNORM_ALPHA = (2 * DEPTH) ** 0.25
NORM_EPS = 1e-5
NEG = -1e30

LANES = 128
ROW_TILE = 512
FFN_COL_CHUNK = 512
ATT_Q_TILE = 128
ATT_KEY_BLOCK = 256
DIFF_HEADS_PER_STEP = 4
SUM_ROWS = 16
LOG2E = 1.4426950408889634
VMEM_LIMIT_BYTES = 56 * 1024 * 1024
N_CHUNKS = D_MODEL // LANES


def _cparams(n_axes):
    return pltpu.CompilerParams(dimension_semantics=("parallel",) * n_axes,
                                vmem_limit_bytes=VMEM_LIMIT_BYTES)


def _resident(shape):
    zeros = (0,) * len(shape)
    return pl.BlockSpec(shape, lambda *_: zeros, pipeline_mode=pl.Buffered(1))


def _layer_norm_rows(r, g, b):
    mu = jnp.mean(r, axis=-1, keepdims=True)
    xc = r - mu
    var = jnp.mean(xc * xc, axis=-1, keepdims=True)
    return xc * lax.rsqrt(var + NORM_EPS) * g + b


def _ffn_ln_kernel(x_ref, wg_ref, wu_ref, wd_ref, g_ref, b_ref, o_ref, h_ref):
    x = x_ref[...]
    xb = x.astype(BF16)
    d_ff = wg_ref.shape[1]
    for c0 in range(0, d_ff, FFN_COL_CHUNK):
        c1 = min(c0 + FFN_COL_CHUNK, d_ff)
        gate = jnp.dot(xb, wg_ref[:, c0:c1], preferred_element_type=F32)
        up = jnp.dot(xb, wu_ref[:, c0:c1], preferred_element_type=F32)
        silu = gate * (1.0 / (1.0 + jnp.exp(-gate)))
        h_ref[:, c0:c1] = (silu * up).astype(BF16)
    y = jnp.dot(h_ref[...], wd_ref[...], preferred_element_type=F32)
    o_ref[...] = _layer_norm_rows(DEEPNORM_ALPHA * x + 0.5 * y, g_ref[...], b_ref[...])


def _ffn_ln(x2d, wg, wu, wd, g, b):
    n, d = x2d.shape
    d_ff = wg.shape[1]
    return pl.pallas_call(
        _ffn_ln_kernel,
        grid=(n // ROW_TILE,),
        in_specs=[pl.BlockSpec((ROW_TILE, d), lambda i: (i, 0)),
                  _resident((d, d_ff)), _resident((d, d_ff)), _resident((d_ff, d)),
                  _resident((1, d)), _resident((1, d))],
        out_specs=pl.BlockSpec((ROW_TILE, d), lambda i: (i, 0)),
        out_shape=jax.ShapeDtypeStruct((n, d), F32),
        scratch_shapes=[pltpu.VMEM((ROW_TILE, d_ff), BF16)],
        compiler_params=_cparams(1),
        name="ffn_ln",
    )(x2d, wg, wu, wd, g, b)


def _rope_tables(seq):
    r = NSA_HEAD_DIM // ROPE_FRACTION
    half = r // 2
    inv = ROPE_THETA ** (-jnp.arange(half, dtype=F32) * 2.0 / r)
    ang = jnp.arange(seq).astype(F32)[:, None] * inv[None, :]
    cos, sin = jnp.cos(ang), jnp.sin(ang)
    pad = jnp.zeros((seq, NSA_HEAD_DIM - r), F32)
    zero = jnp.zeros_like(sin)
    c_head = jnp.concatenate([cos, cos, pad + 1.0], axis=1)
    s_lo_head = jnp.concatenate([-sin, zero, pad], axis=1)
    s_hi_head = jnp.concatenate([zero, sin, pad], axis=1)
    rep = lambda t: jnp.concatenate([t, t], axis=1)
    return rep(c_head), rep(s_lo_head), rep(s_hi_head)


def _rope_chunk(x, c, s_lo, s_hi):
    half = NSA_HEAD_DIM // ROPE_FRACTION // 2
    return x * c + pltpu.roll(x, LANES - half, 1) * s_lo + pltpu.roll(x, half, 1) * s_hi


def _nsa_proj_kernel(x_ref, w_ref, c_ref, sl_ref, sh_ref,
                     qc_ref, qr_ref, kc_ref, vc_ref, ks_ref, kw_ref, vst_ref, vwt_ref, gt_ref):
    proj = jnp.dot(x_ref[...].astype(BF16), w_ref[...], preferred_element_type=F32)
    c, sl, sh = c_ref[...], sl_ref[...], sh_ref[...]
    col = lambda j: proj[:, j * LANES:(j + 1) * LANES]
    for j in range(N_CHUNKS):
        q = col(j) * (NSA_HEAD_DIM ** -0.5 * LOG2E)
        qc_ref[j] = q.astype(BF16)
        qr_ref[j] = _rope_chunk(q, c, sl, sh).astype(BF16)
    base = N_CHUNKS
    kc_ref[...] = col(base)
    vc_ref[...] = col(base + 1)
    for g in range(NSA_KV_GROUPS):
        ks_ref[g] = _rope_chunk(col(base + 2 + g), c, sl, sh).astype(BF16)
        kw_ref[g] = _rope_chunk(col(base + 4 + g), c, sl, sh).astype(BF16)
    n_kb = x_ref.shape[0] // ATT_KEY_BLOCK
    ones = jnp.ones((SUM_ROWS, ATT_KEY_BLOCK), BF16)
    for src, dst in ((base + 6, vst_ref), (base + 7, vwt_ref)):
        vt = col(src).T.astype(BF16)
        for g in range(NSA_KV_GROUPS):
            for kb in range(n_kb):
                dst[g, kb, :NSA_HEAD_DIM, :] = vt[g * NSA_HEAD_DIM:(g + 1) * NSA_HEAD_DIM,
                                                  kb * ATT_KEY_BLOCK:(kb + 1) * ATT_KEY_BLOCK]
                dst[g, kb, NSA_HEAD_DIM:, :] = ones
    gates = col(base + 8)
    gt_ref[...] = (1.0 / (1.0 + jnp.exp(-gates))).T


def _nsa_proj_weight(w_in):
    d = D_MODEL
    kvw = NSA_KV_WIDTH
    q, kc, vc, ks, vs, kw, vw, gates = jnp.split(
        w_in, [int(c) for c in np.cumsum([d] + [kvw] * 6)], axis=1)
    dh = NSA_HEAD_DIM
    dup = lambda t: jnp.concatenate([t[:, :dh], t[:, :dh], t[:, dh:], t[:, dh:]], axis=1)
    gates = jnp.pad(gates, ((0, 0), (0, LANES - NSA_GATE_WIDTH)))
    return jnp.concatenate([q, kc, vc, dup(ks), dup(kw), vs, vw, gates], axis=1).astype(BF16)


def _nsa_proj(x, w, tables):
    b, s, d = x.shape
    ncol = w.shape[1]
    nkb = s // ATT_KEY_BLOCK
    kb_tile = ROW_TILE // ATT_KEY_BLOCK
    dh = NSA_HEAD_DIM + SUM_ROWS
    tab_spec = pl.BlockSpec((ROW_TILE, LANES), lambda bi, i: (i, 0))
    chunk_spec = lambda n: pl.BlockSpec((None, n, ROW_TILE, LANES), lambda bi, i: (bi, 0, i, 0))
    row_spec = pl.BlockSpec((None, ROW_TILE, LANES), lambda bi, i: (bi, i, 0))
    vt_spec = pl.BlockSpec((None, NSA_KV_GROUPS, kb_tile, dh, ATT_KEY_BLOCK), lambda bi, i: (bi, 0, i, 0, 0))
    return pl.pallas_call(
        _nsa_proj_kernel,
        grid=(b, s // ROW_TILE),
        in_specs=[pl.BlockSpec((None, ROW_TILE, d), lambda bi, i: (bi, i, 0)),
                  _resident((d, ncol)), tab_spec, tab_spec, tab_spec],
        out_specs=[chunk_spec(N_CHUNKS), chunk_spec(N_CHUNKS), row_spec, row_spec,
                   chunk_spec(NSA_KV_GROUPS), chunk_spec(NSA_KV_GROUPS), vt_spec, vt_spec,
                   pl.BlockSpec((None, LANES, ROW_TILE), lambda bi, i: (bi, 0, i))],
        out_shape=[jax.ShapeDtypeStruct((b, N_CHUNKS, s, LANES), BF16),
                   jax.ShapeDtypeStruct((b, N_CHUNKS, s, LANES), BF16),
                   jax.ShapeDtypeStruct((b, s, LANES), F32),
                   jax.ShapeDtypeStruct((b, s, LANES), F32),
                   jax.ShapeDtypeStruct((b, NSA_KV_GROUPS, s, LANES), BF16),
                   jax.ShapeDtypeStruct((b, NSA_KV_GROUPS, s, LANES), BF16),
                   jax.ShapeDtypeStruct((b, NSA_KV_GROUPS, nkb, dh, ATT_KEY_BLOCK), BF16),
                   jax.ShapeDtypeStruct((b, NSA_KV_GROUPS, nkb, dh, ATT_KEY_BLOCK), BF16),
                   jax.ShapeDtypeStruct((b, LANES, s), F32)],
        compiler_params=_cparams(2),
        name="nsa_proj",
    )(x, w, *tables)


def _gelu_tanh(x):
    return 0.5 * x * (1.0 + jnp.tanh(np.sqrt(2.0 / np.pi).astype(np.float32) * (x + 0.044715 * (x * x * x))))


def _compress_kernel(kch_ref, vch_ref, pk_ref, pv_ref, w1k_ref, w1v_ref, b1k_ref, b1v_ref, w2k_ref, w2v_ref,
                     kcd_ref, vct_ref):
    n_rows = kch_ref.shape[0]

    def mlp(ch_ref, pos_ref, w1_ref, b1_ref, w2_ref):
        ch = ch_ref[...]
        lo = jnp.dot((ch + pos_ref[0:1, :]).astype(BF16), w1_ref[0], preferred_element_type=F32)
        hi = jnp.dot((ch + pos_ref[1:2, :]).astype(BF16), w1_ref[1], preferred_element_type=F32)
        hid = lo + pltpu.roll(hi, n_rows - 1, 0) + b1_ref[...]
        return jnp.dot(_gelu_tanh(hid).astype(BF16), w2_ref[...], preferred_element_type=F32)

    kcd = mlp(kch_ref, pk_ref, w1k_ref, b1k_ref, w2k_ref)
    for g in range(NSA_KV_GROUPS):
        kcd_ref[g] = kcd[:, g * LANES:(g + 1) * LANES].astype(BF16)
    vct = mlp(vch_ref, pv_ref, w1v_ref, b1v_ref, w2v_ref).T
    for g in range(NSA_KV_GROUPS):
        vct_ref[g] = vct[g * NSA_HEAD_DIM:(g + 1) * NSA_HEAD_DIM, :].astype(BF16)


def _compress_weights(pos, w1, b1, w2, dup_out):
    g_n, dh, hid = NSA_KV_GROUPS, NSA_HEAD_DIM, CMP_MLP_HIDDEN
    w1r = w1.reshape(2, CMP_STRIDE, dh, hid)
    eye = jnp.eye(g_n, dtype=w1.dtype)
    w1x = jnp.einsum('hldm,gk->hlgdkm', w1r, eye).reshape(2, CMP_STRIDE * g_n * dh, g_n * hid)
    posx = jnp.broadcast_to(pos.reshape(2, CMP_STRIDE, 1, dh), (2, CMP_STRIDE, g_n, dh)).reshape(2, -1)
    b1x = jnp.tile(b1, g_n)[None, :]
    reps = 2 if dup_out else 1
    w2x = jnp.einsum('md,gk->gmkd', w2, eye)
    w2x = jnp.broadcast_to(w2x[:, :, :, None, :], (g_n, hid, g_n, reps, dh)).reshape(g_n * hid, g_n * reps * dh)
    return posx, w1x.astype(BF16), b1x, w2x.astype(BF16)


def _compress(kc, vc, wk, wv):
    b, s, _ = kc.shape
    n_rows = s // CMP_STRIDE
    width = CMP_STRIDE * LANES
    kch = kc.reshape(b, n_rows, width)
    vch = vc.reshape(b, n_rows, width)
    pk, w1k, b1k, w2k = wk
    pv, w1v, b1v, w2v = wv
    ch_spec = pl.BlockSpec((None, n_rows, width), lambda bi: (bi, 0, 0))
    return pl.pallas_call(
        _compress_kernel,
        grid=(b,),
        in_specs=[ch_spec, ch_spec, _resident(pk.shape), _resident(pv.shape), _resident(w1k.shape),
                  _resident(w1v.shape), _resident(b1k.shape), _resident(b1v.shape), _resident(w2k.shape),
                  _resident(w2v.shape)],
        out_specs=[pl.BlockSpec((None, NSA_KV_GROUPS, n_rows, LANES), lambda bi: (bi, 0, 0, 0)),
                   pl.BlockSpec((None, NSA_KV_GROUPS, NSA_HEAD_DIM, n_rows), lambda bi: (bi, 0, 0, 0))],
        out_shape=[jax.ShapeDtypeStruct((b, NSA_KV_GROUPS, n_rows, LANES), BF16),
                   jax.ShapeDtypeStruct((b, NSA_KV_GROUPS, NSA_HEAD_DIM, n_rows), BF16)],
        compiler_params=_cparams(1),
        name="nsa_compress",
    )(kch, vch, pk, pv, w1k, w1v, b1k, b1v, w2k, w2v)


def _split_heads_rhs(q):
    lane = lax.broadcasted_iota(jnp.int32, q.shape, 1)
    zero = jnp.zeros_like(q)
    return jnp.concatenate([jnp.where(lane < LANES // 2, q, zero), jnp.where(lane >= LANES // 2, q, zero)], axis=0)


def _scores_t(k, rhs):
    return lax.dot_general(k, rhs, (((1,), (1,)), ((), ())), preferred_element_type=F32)


def _two(row):
    return jnp.concatenate([row, row], axis=1)


def _flash_update(s, vt_blk, m, acc_ref, idx):
    m_new = jnp.maximum(m, jnp.max(s, axis=0, keepdims=True))
    alpha = jnp.exp2(m - m_new)
    p = jnp.exp2(s - m_new).astype(BF16)
    acc_ref[idx] = alpha * acc_ref[idx] + jnp.dot(vt_blk, p, preferred_element_type=F32)
    return m_new


def _normalized(acc, dv):
    return acc[:dv] * (1.0 / acc[dv:dv + 1])


def _nsa_attn_kernel(qc_ref, qr_ref, kcd_ref, vct_ref, ks_ref, kw_ref, vst_ref, vwt_ref, gt_ref, ov_ref,
                     o_ref, ocmp_ref, acc_ref, selb_ref, sa_ref, sb_ref):
    g = pl.program_id(1)
    qi = pl.program_id(2)
    tq = ATT_Q_TILE
    t0 = qi * tq
    n_pairs = qc_ref.shape[0]
    n_cmp = kcd_ref.shape[0]
    seq = ks_ref.shape[0]
    n_sel = seq // SLC_BLOCK
    tok1 = t0 + lax.broadcasted_iota(jnp.int32, (1, tq), 1)
    tok2 = _two(tok1)

    cmp_end = lax.broadcasted_iota(jnp.int32, (n_cmp, 1), 0) * CMP_STRIDE + (CMP_BLOCK - 1)
    vis = cmp_end <= tok2
    any_vis = (tok2 >= CMP_BLOCK - 1).astype(F32)
    psum = jnp.zeros((n_cmp, tq), F32)
    scores = [_scores_t(kcd_ref[...], _split_heads_rhs(qc_ref[c])) for c in range(n_pairs)]
    for c in range(n_pairs):
        s = jnp.where(vis, scores[c], NEG)
        m = jnp.max(s, axis=0, keepdims=True)
        e = jnp.exp2(s - m)
        p = e * (any_vis / jnp.sum(e, axis=0, keepdims=True))
        psum += p[:, :tq] + p[:, tq:]
        ocmp_ref[c] = jnp.dot(vct_ref[...], p.astype(BF16), preferred_element_type=F32)

    imp = jnp.dot(ov_ref[...], psum, precision=lax.Precision.HIGHEST, preferred_element_type=F32)
    blk = lax.broadcasted_iota(jnp.int32, (n_sel, 1), 0)
    cur = tok1 // SLC_BLOCK
    forced = (blk == 0) | (blk == cur) | (blk == cur - 1)
    valid = blk <= cur
    imp = jnp.where(valid, imp + jnp.where(forced, FORCE_BONUS, 0.0), NEG)
    rows8 = 8
    grp = [imp[rows8 * j:rows8 * (j + 1), :] for j in range(n_sel // rows8)]
    sub = lax.broadcasted_iota(jnp.int32, (rows8, 1), 0)
    rank = [jnp.zeros((rows8, tq), jnp.int32) for _ in grp]
    for m in range(n_sel):
        row = imp[m:m + 1, :]
        for j in range(len(grp)):
            ge = lambda: (row >= grp[j]).astype(jnp.int32)
            gt = lambda: (row > grp[j]).astype(jnp.int32)
            if rows8 * j > m:
                beats = ge()
            elif rows8 * j + rows8 - 1 <= m:
                beats = gt()
            else:
                beats = jnp.where(sub + rows8 * j > m, ge(), gt())
            rank[j] = rank[j] + beats
    rank = jnp.concatenate(rank, axis=0)
    selb_ref[...] = jnp.where((rank < SLC_TOPK) & valid, 0.0, NEG)

    kb_sz = ATT_KEY_BLOCK
    last = (t0 + tq - 1) // kb_sz
    blocks_per_kb = kb_sz // SLC_BLOCK
    krow = lax.broadcasted_iota(jnp.int32, (kb_sz, 1), 0)
    rhs = [_split_heads_rhs(qr_ref[c]) for c in range(n_pairs)]
    acc_ref[...] = jnp.zeros_like(acc_ref)

    def sel_scores(kb, dst_ref):
        k_blk = ks_ref[pl.ds(pl.multiple_of(kb * kb_sz, kb_sz), kb_sz), :]
        for c in range(n_pairs):
            dst_ref[c] = _scores_t(k_blk, rhs[c])

    def sel_consume(kb, src_ref, st):
        bias = jnp.concatenate(
            [jnp.broadcast_to(_two(selb_ref[pl.ds(kb * blocks_per_kb + j, 1), :]), (SLC_BLOCK, 2 * tq))
             for j in range(blocks_per_kb)], axis=0)
        bias = jnp.where(kb * kb_sz + krow <= tok2, bias, NEG)
        return tuple(_flash_update(src_ref[c] + bias, vst_ref[kb], st[c], acc_ref, c) for c in range(n_pairs))

    def sel_two_blocks(i, st):
        kb = 2 * i
        sel_scores(kb + 1, sb_ref)
        st = sel_consume(kb, sa_ref, st)
        sel_scores(jnp.minimum(kb + 2, last), sa_ref)
        return sel_consume(kb + 1, sb_ref, st)

    init = tuple(jnp.full((1, 2 * tq), NEG, F32) for _ in range(n_pairs))
    sel_scores(0, sa_ref)
    st = lax.fori_loop(0, (last + 1) // 2, sel_two_blocks, init)
    st = lax.cond(last % 2 == 0, lambda st: sel_consume(last, sa_ref, st), lambda st: st, st)

    win_blocks = WINDOW // kb_sz + 1
    win_start = jnp.minimum(jnp.maximum(t0 - WINDOW, 0) // kb_sz, seq // kb_sz - win_blocks)
    w0 = pl.multiple_of(win_start * kb_sz, kb_sz)
    wrow = w0 + lax.broadcasted_iota(jnp.int32, (win_blocks * kb_sz, 1), 0)
    in_window = (wrow <= tok2) & (wrow > tok2 - WINDOW)
    k_win = kw_ref[pl.ds(w0, win_blocks * kb_sz), :]
    scores = [_scores_t(k_win, rhs[c]) for c in range(n_pairs)]
    for c in range(n_pairs):
        s = jnp.where(in_window, scores[c], NEG)
        m = jnp.max(s, axis=0, keepdims=True)
        p = jnp.exp2(s - m).astype(BF16)
        win = jnp.zeros((NSA_HEAD_DIM + SUM_ROWS, 2 * tq), F32)
        for j in range(win_blocks):
            win += jnp.dot(vwt_ref[win_start + j], p[j * kb_sz:(j + 1) * kb_sz, :], preferred_element_type=F32)
        o_win = _normalized(win, NSA_HEAD_DIM)
        o_sel = _normalized(acc_ref[c], NSA_HEAD_DIM)

        head = g * (NSA_HEADS // NSA_KV_GROUPS) + 2 * c
        gate = lambda j: jnp.concatenate([gt_ref[pl.ds(j * NSA_HEADS + head, 1), :],
                                          gt_ref[pl.ds(j * NSA_HEADS + head + 1, 1), :]], axis=1)
        o_t = gate(0) * ocmp_ref[c] + gate(1) * o_sel + gate(2) * o_win
        o_ref[c] = jnp.concatenate([o_t[:, :tq], o_t[:, tq:]], axis=0).T.astype(BF16)


def _cmp_to_slc_t(n_cmp_rows, n_sel):
    c0 = np.arange(n_cmp_rows)[None, :] * CMP_STRIDE
    s0 = np.arange(n_sel)[:, None] * SLC_BLOCK
    ov = np.minimum(c0 + CMP_BLOCK, s0 + SLC_BLOCK) - np.maximum(c0, s0)
    return (np.clip(ov, 0, None) / CMP_BLOCK).astype(np.float32)


def _nsa_attn(qc, qr, kcd, vct, ksd, kwd, vst, vwt, gt):
    b, n_chunks, s, _ = qc.shape
    groups = NSA_KV_GROUPS
    pairs = n_chunks // groups
    n_cmp = kcd.shape[2]
    nkb = s // ATT_KEY_BLOCK
    dh = NSA_HEAD_DIM
    tq = ATT_Q_TILE
    n_sel = s // SLC_BLOCK
    ov_t = jnp.asarray(_cmp_to_slc_t(n_cmp, n_sel))
    q_spec = pl.BlockSpec((None, pairs, tq, LANES), lambda bi, g, qi: (bi, g, qi, 0))
    per_group = lambda *shape: pl.BlockSpec((None, None) + shape, lambda bi, g, qi: (bi, g) + (0,) * len(shape))
    return pl.pallas_call(
        _nsa_attn_kernel,
        grid=(b, groups, s // tq),
        in_specs=[q_spec, q_spec, per_group(n_cmp, LANES), per_group(dh, n_cmp), per_group(s, LANES),
                  per_group(s, LANES), per_group(nkb, dh + SUM_ROWS, ATT_KEY_BLOCK),
                  per_group(nkb, dh + SUM_ROWS, ATT_KEY_BLOCK),
                  pl.BlockSpec((None, LANES, tq), lambda bi, g, qi: (bi, 0, qi)),
                  _resident(ov_t.shape)],
        out_specs=q_spec,
        out_shape=jax.ShapeDtypeStruct((b, n_chunks, s, LANES), BF16),
        scratch_shapes=[pltpu.VMEM((pairs, dh, 2 * tq), F32), pltpu.VMEM((pairs, dh + SUM_ROWS, 2 * tq), F32),
                        pltpu.VMEM((n_sel, tq), F32),
                        pltpu.VMEM((pairs, ATT_KEY_BLOCK, 2 * tq), F32), pltpu.VMEM((pairs, ATT_KEY_BLOCK, 2 * tq), F32)],
        compiler_params=_cparams(3),
        name="nsa_attn",
    )(qc, qr, kcd, vct, ksd, kwd, vst, vwt, gt, ov_t)


def _out_ln_kernel(o_ref, x_ref, w_ref, g_ref, b_ref, y_ref):
    o = jnp.concatenate([o_ref[j] for j in range(o_ref.shape[0])], axis=1)
    mix = jnp.dot(o, w_ref[...], preferred_element_type=F32)
    y_ref[...] = _layer_norm_rows(DEEPNORM_ALPHA * x_ref[...] + mix, g_ref[...], b_ref[...])


def _out_ln(o, x, w, g, b):
    bsz, n_chunks, s, _ = o.shape
    d = x.shape[-1]
    x_spec = pl.BlockSpec((None, ROW_TILE, d), lambda bi, i: (bi, i, 0))
    return pl.pallas_call(
        _out_ln_kernel,
        grid=(bsz, s // ROW_TILE),
        in_specs=[pl.BlockSpec((None, n_chunks, ROW_TILE, LANES), lambda bi, i: (bi, 0, i, 0)), x_spec,
                  _resident((d, d)), _resident((1, d)), _resident((1, d))],
        out_specs=x_spec,
        out_shape=jax.ShapeDtypeStruct(x.shape, F32),
        compiler_params=_cparams(2),
        name="mixer_out_ln",
    )(o, x, w, g, b)


def _pair_maps_columns(w):
    d = w.shape[0]
    return w.reshape(d, 2, DIFF_HEADS, DIFF_HEAD_DIM).transpose(0, 2, 1, 3).reshape(d, -1)


def _kv_proj_kernel(x_ref, w_ref, c_ref, sl_ref, sh_ref, k_ref, vt_ref):
    proj = jnp.dot(x_ref[...].astype(BF16), w_ref[...], preferred_element_type=F32)
    c, sl, sh = c_ref[...], sl_ref[...], sh_ref[...]
    n_kb = x_ref.shape[0] // ATT_KEY_BLOCK
    ones = jnp.ones((SUM_ROWS, ATT_KEY_BLOCK), BF16)
    for h in range(DIFF_HEADS):
        k_ref[h] = _rope_chunk(proj[:, h * LANES:(h + 1) * LANES], c, sl, sh).astype(BF16)
        vt = proj[:, D_MODEL + h * DIFF_V_DIM:D_MODEL + (h + 1) * DIFF_V_DIM].T.astype(BF16)
        for kb in range(n_kb):
            vt_ref[h, kb, :DIFF_V_DIM, :] = vt[:, kb * ATT_KEY_BLOCK:(kb + 1) * ATT_KEY_BLOCK]
            vt_ref[h, kb, DIFF_V_DIM:, :] = ones


def _kv_proj(x, w, tables):
    b, s, d = x.shape
    nkb = s // ATT_KEY_BLOCK
    kb_tile = ROW_TILE // ATT_KEY_BLOCK
    tab_spec = pl.BlockSpec((ROW_TILE, LANES), lambda bi, i: (i, 0))
    return pl.pallas_call(
        _kv_proj_kernel,
        grid=(b, s // ROW_TILE),
        in_specs=[pl.BlockSpec((None, ROW_TILE, d), lambda bi, i: (bi, i, 0)), _resident(w.shape),
                  tab_spec, tab_spec, tab_spec],
        out_specs=[pl.BlockSpec((None, DIFF_HEADS, ROW_TILE, LANES), lambda bi, i: (bi, 0, i, 0)),
                   pl.BlockSpec((None, DIFF_HEADS, kb_tile, DIFF_V_DIM + SUM_ROWS, ATT_KEY_BLOCK),
                                lambda bi, i: (bi, 0, i, 0, 0))],
        out_shape=[jax.ShapeDtypeStruct((b, DIFF_HEADS, s, LANES), BF16),
                   jax.ShapeDtypeStruct((b, DIFF_HEADS, nkb, DIFF_V_DIM + SUM_ROWS, ATT_KEY_BLOCK), BF16)],
        compiler_params=_cparams(2),
        name="diff_kv_proj",
    )(x, w, *tables)


def _dq_proj_kernel(x_ref, w_ref, c_ref, sl_ref, sh_ref, q_ref):
    proj = jnp.dot(x_ref[...].astype(BF16), w_ref[...], preferred_element_type=F32)
    c, sl, sh = c_ref[...], sl_ref[...], sh_ref[...]
    for h in range(DIFF_HEADS):
        q = proj[:, h * LANES:(h + 1) * LANES] * (DIFF_HEAD_DIM ** -0.5 * LOG2E)
        q_ref[h] = _rope_chunk(q, c, sl, sh).astype(BF16)


def _dq_proj(x, w, tables):
    b, s, d = x.shape
    tab_spec = pl.BlockSpec((ROW_TILE, LANES), lambda bi, i: (i, 0))
    return pl.pallas_call(
        _dq_proj_kernel,
        grid=(b, s // ROW_TILE),
        in_specs=[pl.BlockSpec((None, ROW_TILE, d), lambda bi, i: (bi, i, 0)), _resident(w.shape),
                  tab_spec, tab_spec, tab_spec],
        out_specs=pl.BlockSpec((None, DIFF_HEADS, ROW_TILE, LANES), lambda bi, i: (bi, 0, i, 0)),
        out_shape=jax.ShapeDtypeStruct((b, DIFF_HEADS, s, LANES), BF16),
        compiler_params=_cparams(2),
        name="diff_q_proj",
    )(x, w, *tables)


def _diff_attn_kernel(q_ref, k_ref, vt_ref, lam_ref, gain_ref, o_ref, acc_ref, sa_ref, sb_ref, *, lambda_init):
    qi = pl.program_id(2)
    n_heads = q_ref.shape[0]
    tq = ATT_Q_TILE
    t0 = qi * tq
    kb_sz = ATT_KEY_BLOCK
    tok2 = _two(t0 + lax.broadcasted_iota(jnp.int32, (1, tq), 1))
    krow = lax.broadcasted_iota(jnp.int32, (kb_sz, 1), 0)
    rhs = [_split_heads_rhs(q_ref[h]) for h in range(n_heads)]
    last = (t0 + tq - 1) // kb_sz
    acc_ref[...] = jnp.zeros_like(acc_ref)

    def scores_into(kb, dst_ref):
        k0 = pl.multiple_of(kb * kb_sz, kb_sz)
        for h in range(n_heads):
            dst_ref[h] = _scores_t(k_ref[h, pl.ds(k0, kb_sz), :], rhs[h])

    def consume(kb, src_ref, st, maybe_diagonal):
        if maybe_diagonal:
            visible = kb * kb_sz + krow <= tok2
        out = []
        for h in range(n_heads):
            s = src_ref[h]
            if maybe_diagonal:
                s = jnp.where(visible, s, NEG)
            out.append(_flash_update(s, vt_ref[h, kb], st[h], acc_ref, h))
        return tuple(out)

    def two_blocks(i, st):
        kb = 2 * i
        scores_into(kb + 1, sb_ref)
        st = consume(kb, sa_ref, st, False)
        scores_into(jnp.minimum(kb + 2, last), sa_ref)
        return consume(kb + 1, sb_ref, st, True)

    init = tuple(jnp.full((1, 2 * tq), NEG, F32) for _ in range(n_heads))
    scores_into(0, sa_ref)
    st = lax.fori_loop(0, (last + 1) // 2, two_blocks, init)
    st = lax.cond(last % 2 == 0, lambda st: consume(last, sa_ref, st, True), lambda st: st, st)

    lp = lam_ref[...]
    lam = (jnp.exp(jnp.sum(lp[0:1] * lp[1:2], axis=1, keepdims=True))
           - jnp.exp(jnp.sum(lp[2:3] * lp[3:4], axis=1, keepdims=True)) + lambda_init)
    for h in range(n_heads):
        a = _normalized(acc_ref[h], DIFF_V_DIM)
        o = a[:, :tq] - lam * a[:, tq:]
        ms = jnp.mean(o * o, axis=0, keepdims=True)
        o = o * lax.rsqrt(ms + NORM_EPS) * gain_ref[...] * (1.0 - lambda_init)
        o_ref[h] = o.T.astype(BF16)


def _diff_attn(q, k, vt, lam_params, gain, lambda_init):
    b, h, s, _ = q.shape
    nkb = s // ATT_KEY_BLOCK
    tq = ATT_Q_TILE
    hps = DIFF_HEADS_PER_STEP
    per_head = lambda *shape: pl.BlockSpec((None, hps) + shape, lambda bi, hi, qi: (bi, hi) + (0,) * len(shape))
    q_spec = pl.BlockSpec((None, hps, tq, LANES), lambda bi, hi, qi: (bi, hi, qi, 0))
    return pl.pallas_call(
        functools.partial(_diff_attn_kernel, lambda_init=lambda_init),
        grid=(b, h // hps, s // tq),
        in_specs=[q_spec, per_head(s, LANES), per_head(nkb, DIFF_V_DIM + SUM_ROWS, ATT_KEY_BLOCK),
                  _resident(lam_params.shape), _resident(gain.shape)],
        out_specs=q_spec,
        out_shape=jax.ShapeDtypeStruct((b, h, s, LANES), BF16),
        scratch_shapes=[pltpu.VMEM((hps, DIFF_V_DIM + SUM_ROWS, 2 * tq), F32),
                        pltpu.VMEM((hps, ATT_KEY_BLOCK, 2 * tq), F32), pltpu.VMEM((hps, ATT_KEY_BLOCK, 2 * tq), F32)],
        compiler_params=_cparams(3),
        name="diff_attn",
    )(q, k, vt, lam_params, gain)


def kernel(x, ffn1_w_gate, ffn1_w_up, ffn1_w_down, ffn2_w_gate, ffn2_w_up, ffn2_w_down, ln_g, ln_b, a_w_in,
           a_cmp_pos_k, a_cmp_w1_k, a_cmp_b1_k, a_cmp_w2_k, a_cmp_pos_v, a_cmp_w1_v, a_cmp_b1_v, a_cmp_w2_v,
           a_w_out, b_w_kv_shared, b_w_q, b_lambda_q1, b_lambda_k1, b_lambda_q2, b_lambda_k2, b_subln_g, b_w_out):
    bsz, seq, d = x.shape
    assert d == D_MODEL and seq % ROW_TILE == 0 and seq >= WINDOW + 2 * ATT_KEY_BLOCK
    tables = _rope_tables(seq)
    bf = lambda w: w.astype(BF16)
    row = lambda v: v[None, :]

    def ffn(xv, wg, wu, wd, g, b):
        return _ffn_ln(xv.reshape(bsz * seq, d), bf(wg), bf(wu), bf(wd), row(g), row(b)).reshape(bsz, seq, d)

    k12 = vt = None
    for layer in range(DEPTH):
        x = ffn(x, ffn1_w_gate[layer], ffn1_w_up[layer], ffn1_w_down[layer], ln_g[layer, 0], ln_b[layer, 0])
        if layer < N_A_LAYERS:
            qc, qr, kc, vc, ksd, kwd, vst, vwt, gt = _nsa_proj(x, _nsa_proj_weight(a_w_in[layer]), tables)
            wk = _compress_weights(a_cmp_pos_k[layer], a_cmp_w1_k[layer], a_cmp_b1_k[layer], a_cmp_w2_k[layer], True)
            wv = _compress_weights(a_cmp_pos_v[layer], a_cmp_w1_v[layer], a_cmp_b1_v[layer], a_cmp_w2_v[layer], False)
            kcd, vct = _compress(kc, vc, wk, wv)
            o = _nsa_attn(qc, qr, kcd, vct, ksd, kwd, vst, vwt, gt)
            w_out = a_w_out[layer]
        else:
            j = layer - N_A_LAYERS
            lambda_init = 0.8 - 0.6 * float(np.exp(-0.3 * layer))
            q = _dq_proj(x, bf(_pair_maps_columns(b_w_q[j])), tables)
            lam_params = jnp.pad(jnp.stack([b_lambda_q1[j], b_lambda_k1[j], b_lambda_q2[j], b_lambda_k2[j]]),
                                 ((0, 4), (0, LANES - DIFF_HEAD_DIM)))
            gain = jnp.broadcast_to(b_subln_g[j][:, None], (DIFF_V_DIM, ATT_Q_TILE))
            o = _diff_attn(q, k12, vt, lam_params, gain, lambda_init)
            w_out = b_w_out[j]
        x = _out_ln(o, x, bf(w_out), row(ln_g[layer, 1]), row(ln_b[layer, 1]))
        x = ffn(x, ffn2_w_gate[layer], ffn2_w_up[layer], ffn2_w_down[layer], ln_g[layer, 2], ln_b[layer, 2])
        if layer == N_A_LAYERS - 1:
            w_kv = jnp.concatenate([_pair_maps_columns(b_w_kv_shared[:, :D_MODEL]), b_w_kv_shared[:, D_MODEL:]], axis=1)
            k12, vt = _kv_proj(x, bf(w_kv), tables)
    return x
```

```python
import functools

import numpy as np
import jax
import jax.numpy as jnp
from jax import lax
from jax.experimental import pallas as pl
from jax.experimental.pallas import tpu as pltpu

F32 = jnp.float32
BF16 = jnp.bfloat16

D_MODEL = 1024
DEPTH = 4
N_A_LAYERS = DEPTH // 2
NSA_HEADS = 16
NSA_HEAD_DIM = D_MODEL // NSA_HEADS
NSA_KV_GROUPS = 2
NSA_KV_WIDTH = NSA_KV_GROUPS * NSA_HEAD_DIM
NSA_GATE_WIDTH = 3 * NSA_HEADS
CMP_BLOCK = 32
CMP_STRIDE = 16
CMP_MLP_HIDDEN = 128
SLC_BLOCK = 64
SLC_TOPK = 16
WINDOW = 512
FORCE_BONUS = 1000.0
DIFF_HEADS = 8
DIFF_HEAD_DIM = D_MODEL // (2 * DIFF_HEADS)
DIFF_V_DIM = 2 * DIFF_HEAD_DIM
ROPE_THETA = 500000.0
ROPE_FRACTION = 4
D_FF = 2816
DEEPNORM_ALPHA = (2 * DEPTH) ** 0.25
NORM_EPS = 1e-5
NEG = -1e30

LANES = 128
ROW_TILE = 512
FFN_COL_CHUNK = 512
ATT_Q_TILE = 128
ATT_KEY_BLOCK = 256
DIFF_HEADS_PER_STEP = 4
SUM_ROWS = 16
LOG2E = 1.4426950408889634
VMEM_LIMIT_BYTES = 56 * 1024 * 1024
N_CHUNKS = D_MODEL // LANES


def _cparams(n_axes):
    return pltpu.CompilerParams(dimension_semantics=("parallel",) * n_axes,
                                vmem_limit_bytes=VMEM_LIMIT_BYTES)


def _resident(shape):
    zeros = (0,) * len(shape)
    return pl.BlockSpec(shape, lambda *_: zeros, pipeline_mode=pl.Buffered(1))


def _layer_norm_rows(r, g, b):
    mu = jnp.mean(r, axis=-1, keepdims=True)
    xc = r - mu
    var = jnp.mean(xc * xc, axis=-1, keepdims=True)
    return xc * lax.rsqrt(var + NORM_EPS) * g + b


def _ffn_ln_kernel(x_ref, wg_ref, wu_ref, wd_ref, g_ref, b_ref, o_ref, h_ref):
    x = x_ref[...]
    xb = x.astype(BF16)
    d_ff = wg_ref.shape[1]
    for c0 in range(0, d_ff, FFN_COL_CHUNK):
        c1 = min(c0 + FFN_COL_CHUNK, d_ff)
        gate = jnp.dot(xb, wg_ref[:, c0:c1], preferred_element_type=F32)
        up = jnp.dot(xb, wu_ref[:, c0:c1], preferred_element_type=F32)
        silu = gate * (1.0 / (1.0 + jnp.exp(-gate)))
        h_ref[:, c0:c1] = (silu * up).astype(BF16)
    y = jnp.dot(h_ref[...], wd_ref[...], preferred_element_type=F32)
    o_ref[...] = _layer_norm_rows(DEEPNORM_ALPHA * x + 0.5 * y, g_ref[...], b_ref[...])


def _ffn_ln(x2d, wg, wu, wd, g, b):
    n, d = x2d.shape
    d_ff = wg.shape[1]
    return pl.pallas_call(
        _ffn_ln_kernel,
        grid=(n // ROW_TILE,),
        in_specs=[pl.BlockSpec((ROW_TILE, d), lambda i: (i, 0)),
                  _resident((d, d_ff)), _resident((d, d_ff)), _resident((d_ff, d)),
                  _resident((1, d)), _resident((1, d))],
        out_specs=pl.BlockSpec((ROW_TILE, d), lambda i: (i, 0)),
        out_shape=jax.ShapeDtypeStruct((n, d), F32),
        scratch_shapes=[pltpu.VMEM((ROW_TILE, d_ff), BF16)],
        compiler_params=_cparams(1),
        name="ffn_ln",
    )(x2d, wg, wu, wd, g, b)


def _rope_tables(seq):
    r = NSA_HEAD_DIM // ROPE_FRACTION
    half = r // 2
    inv = ROPE_THETA ** (-jnp.arange(half, dtype=F32) * 2.0 / r)
    ang = jnp.arange(seq).astype(F32)[:, None] * inv[None, :]
    cos, sin = jnp.cos(ang), jnp.sin(ang)
    pad = jnp.zeros((seq, NSA_HEAD_DIM - r), F32)
    zero = jnp.zeros_like(sin)
    c_head = jnp.concatenate([cos, cos, pad + 1.0], axis=1)
    s_lo_head = jnp.concatenate([-sin, zero, pad], axis=1)
    s_hi_head = jnp.concatenate([zero, sin, pad], axis=1)
    rep = lambda t: jnp.concatenate([t, t], axis=1)
    return rep(c_head), rep(s_lo_head), rep(s_hi_head)


def _rope_chunk(x, c, s_lo, s_hi):
    half = NSA_HEAD_DIM // ROPE_FRACTION // 2
    return x * c + pltpu.roll(x, LANES - half, 1) * s_lo + pltpu.roll(x, half, 1) * s_hi


def _nsa_proj_kernel(x_ref, w_ref, c_ref, sl_ref, sh_ref,
                     qc_ref, qr_ref, kc_ref, vc_ref, ks_ref, kw_ref, vst_ref, vwt_ref, gt_ref):
    proj = jnp.dot(x_ref[...].astype(BF16), w_ref[...], preferred_element_type=F32)
    c, sl, sh = c_ref[...], sl_ref[...], sh_ref[...]
    col = lambda j: proj[:, j * LANES:(j + 1) * LANES]
    for j in range(N_CHUNKS):
        q = col(j) * (NSA_HEAD_DIM ** -0.5 * LOG2E)
        qc_ref[j] = q.astype(BF16)
        qr_ref[j] = _rope_chunk(q, c, sl, sh).astype(BF16)
    base = N_CHUNKS
    kc_ref[...] = col(base)
    vc_ref[...] = col(base + 1)
    for g in range(NSA_KV_GROUPS):
        ks_ref[g] = _rope_chunk(col(base + 2 + g), c, sl, sh).astype(BF16)
        kw_ref[g] = _rope_chunk(col(base + 4 + g), c, sl, sh).astype(BF16)
    n_kb = x_ref.shape[0] // ATT_KEY_BLOCK
    ones = jnp.ones((SUM_ROWS, ATT_KEY_BLOCK), BF16)
    for src, dst in ((base + 6, vst_ref), (base + 7, vwt_ref)):
        vt = col(src).T.astype(BF16)
        for g in range(NSA_KV_GROUPS):
            for kb in range(n_kb):
                dst[g, kb, :NSA_HEAD_DIM, :] = vt[g * NSA_HEAD_DIM:(g + 1) * NSA_HEAD_DIM,
                                                  kb * ATT_KEY_BLOCK:(kb + 1) * ATT_KEY_BLOCK]
                dst[g, kb, NSA_HEAD_DIM:, :] = ones
    gates = col(base + 8)
    gt_ref[...] = (1.0 / (1.0 + jnp.exp(-gates))).T


def _nsa_proj_weight(w_in):
    d = D_MODEL
    kvw = NSA_KV_WIDTH
    q, kc, vc, ks, vs, kw, vw, gates = jnp.split(
        w_in, [int(c) for c in np.cumsum([d] + [kvw] * 6)], axis=1)
    dh = NSA_HEAD_DIM
    dup = lambda t: jnp.concatenate([t[:, :dh], t[:, :dh], t[:, dh:], t[:, dh:]], axis=1)
    gates = jnp.pad(gates, ((0, 0), (0, LANES - NSA_GATE_WIDTH)))
    return jnp.concatenate([q, kc, vc, dup(ks), dup(kw), vs, vw, gates], axis=1).astype(BF16)


def _nsa_proj(x, w, tables):
    b, s, d = x.shape
    ncol = w.shape[1]
    nkb = s // ATT_KEY_BLOCK
    kb_tile = ROW_TILE // ATT_KEY_BLOCK
    dh = NSA_HEAD_DIM + SUM_ROWS
    tab_spec = pl.BlockSpec((ROW_TILE, LANES), lambda bi, i: (i, 0))
    chunk_spec = lambda n: pl.BlockSpec((None, n, ROW_TILE, LANES), lambda bi, i: (bi, 0, i, 0))
    row_spec = pl.BlockSpec((None, ROW_TILE, LANES), lambda bi, i: (bi, i, 0))
    vt_spec = pl.BlockSpec((None, NSA_KV_GROUPS, kb_tile, dh, ATT_KEY_BLOCK), lambda bi, i: (bi, 0, i, 0, 0))
    return pl.pallas_call(
        _nsa_proj_kernel,
        grid=(b, s // ROW_TILE),
        in_specs=[pl.BlockSpec((None, ROW_TILE, d), lambda bi, i: (bi, i, 0)),
                  _resident((d, ncol)), tab_spec, tab_spec, tab_spec],
        out_specs=[chunk_spec(N_CHUNKS), chunk_spec(N_CHUNKS), row_spec, row_spec,
                   chunk_spec(NSA_KV_GROUPS), chunk_spec(NSA_KV_GROUPS), vt_spec, vt_spec,
                   pl.BlockSpec((None, LANES, ROW_TILE), lambda bi, i: (bi, 0, i))],
        out_shape=[jax.ShapeDtypeStruct((b, N_CHUNKS, s, LANES), BF16),
                   jax.ShapeDtypeStruct((b, N_CHUNKS, s, LANES), BF16),
                   jax.ShapeDtypeStruct((b, s, LANES), F32),
                   jax.ShapeDtypeStruct((b, s, LANES), F32),
                   jax.ShapeDtypeStruct((b, NSA_KV_GROUPS, s, LANES), BF16),
                   jax.ShapeDtypeStruct((b, NSA_KV_GROUPS, s, LANES), BF16),
                   jax.ShapeDtypeStruct((b, NSA_KV_GROUPS, nkb, dh, ATT_KEY_BLOCK), BF16),
                   jax.ShapeDtypeStruct((b, NSA_KV_GROUPS, nkb, dh, ATT_KEY_BLOCK), BF16),
                   jax.ShapeDtypeStruct((b, LANES, s), F32)],
        compiler_params=_cparams(2),
        name="nsa_proj",
    )(x, w, *tables)


def _gelu_tanh(x):
    return 0.5 * x * (1.0 + jnp.tanh(np.sqrt(2.0 / np.pi).astype(np.float32) * (x + 0.044715 * (x * x * x))))


def _compress_kernel(kch_ref, vch_ref, pk_ref, pv_ref, w1k_ref, w1v_ref, b1k_ref, b1v_ref, w2k_ref, w2v_ref,
                     kcd_ref, vct_ref):
    n_rows = kch_ref.shape[0]

    def mlp(ch_ref, pos_ref, w1_ref, b1_ref, w2_ref):
        ch = ch_ref[...]
        lo = jnp.dot((ch + pos_ref[0:1, :]).astype(BF16), w1_ref[0], preferred_element_type=F32)
        hi = jnp.dot((ch + pos_ref[1:2, :]).astype(BF16), w1_ref[1], preferred_element_type=F32)
        hid = lo + pltpu.roll(hi, n_rows - 1, 0) + b1_ref[...]
        return jnp.dot(_gelu_tanh(hid).astype(BF16), w2_ref[...], preferred_element_type=F32)

    kcd = mlp(kch_ref, pk_ref, w1k_ref, b1k_ref, w2k_ref)
    for g in range(NSA_KV_GROUPS):
        kcd_ref[g] = kcd[:, g * LANES:(g + 1) * LANES].astype(BF16)
    vct = mlp(vch_ref, pv_ref, w1v_ref, b1v_ref, w2v_ref).T
    for g in range(NSA_KV_GROUPS):
        vct_ref[g] = vct[g * NSA_HEAD_DIM:(g + 1) * NSA_HEAD_DIM, :].astype(BF16)


def _compress_weights(pos, w1, b1, w2, dup_out):
    g_n, dh, hid = NSA_KV_GROUPS, NSA_HEAD_DIM, CMP_MLP_HIDDEN
    w1r = w1.reshape(2, CMP_STRIDE, dh, hid)
    eye = jnp.eye(g_n, dtype=w1.dtype)
    w1x = jnp.einsum('hldm,gk->hlgdkm', w1r, eye).reshape(2, CMP_STRIDE * g_n * dh, g_n * hid)
    posx = jnp.broadcast_to(pos.reshape(2, CMP_STRIDE, 1, dh), (2, CMP_STRIDE, g_n, dh)).reshape(2, -1)
    b1x = jnp.tile(b1, g_n)[None, :]
    reps = 2 if dup_out else 1
    w2x = jnp.einsum('md,gk->gmkd', w2, eye)
    w2x = jnp.broadcast_to(w2x[:, :, :, None, :], (g_n, hid, g_n, reps, dh)).reshape(g_n * hid, g_n * reps * dh)
    return posx, w1x.astype(BF16), b1x, w2x.astype(BF16)


def _compress(kc, vc, wk, wv):
    b, s, _ = kc.shape
    n_rows = s // CMP_STRIDE
    width = CMP_STRIDE * LANES
    kch = kc.reshape(b, n_rows, width)
    vch = vc.reshape(b, n_rows, width)
    pk, w1k, b1k, w2k = wk
    pv, w1v, b1v, w2v = wv
    ch_spec = pl.BlockSpec((None, n_rows, width), lambda bi: (bi, 0, 0))
    return pl.pallas_call(
        _compress_kernel,
        grid=(b,),
        in_specs=[ch_spec, ch_spec, _resident(pk.shape), _resident(pv.shape), _resident(w1k.shape),
                  _resident(w1v.shape), _resident(b1k.shape), _resident(b1v.shape), _resident(w2k.shape),
                  _resident(w2v.shape)],
        out_specs=[pl.BlockSpec((None, NSA_KV_GROUPS, n_rows, LANES), lambda bi: (bi, 0, 0, 0)),
                   pl.BlockSpec((None, NSA_KV_GROUPS, NSA_HEAD_DIM, n_rows), lambda bi: (bi, 0, 0, 0))],
        out_shape=[jax.ShapeDtypeStruct((b, NSA_KV_GROUPS, n_rows, LANES), BF16),
                   jax.ShapeDtypeStruct((b, NSA_KV_GROUPS, NSA_HEAD_DIM, n_rows), BF16)],
        compiler_params=_cparams(1),
        name="nsa_compress",
    )(kch, vch, pk, pv, w1k, w1v, b1k, b1v, w2k, w2v)


def _split_heads_rhs(q):
    lane = lax.broadcasted_iota(jnp.int32, q.shape, 1)
    zero = jnp.zeros_like(q)
    return jnp.concatenate([jnp.where(lane < LANES // 2, q, zero), jnp.where(lane >= LANES // 2, q, zero)], axis=0)


def _scores_t(k, rhs):
    return lax.dot_general(k, rhs, (((1,), (1,)), ((), ())), preferred_element_type=F32)


def _two(row):
    return jnp.concatenate([row, row], axis=1)


def _flash_update(s, vt_blk, m, acc_ref, idx):
    m_new = jnp.maximum(m, jnp.max(s, axis=0, keepdims=True))
    alpha = jnp.exp2(m - m_new)
    p = jnp.exp2(s - m_new).astype(BF16)
    acc_ref[idx] = alpha * acc_ref[idx] + jnp.dot(vt_blk, p, preferred_element_type=F32)
    return m_new


def _flash_update_block_bias(src_ref, idx, bias_rows, vt_blk, m, acc_ref):
    rows = src_ref.shape[1] // len(bias_rows)
    piece = lambda j: src_ref[idx, j * rows:(j + 1) * rows, :]
    m_new = m
    for j, b in enumerate(bias_rows):
        m_new = jnp.maximum(m_new, jnp.max(piece(j), axis=0, keepdims=True) + b)
    alpha = jnp.exp2(m - m_new)
    p = jnp.concatenate([jnp.exp2(piece(j) + (b - m_new)).astype(BF16) for j, b in enumerate(bias_rows)], axis=0)
    acc_ref[idx] = alpha * acc_ref[idx] + jnp.dot(vt_blk, p, preferred_element_type=F32)
    return m_new


def _normalized(acc, dv):
    return acc[:dv] * (1.0 / acc[dv:dv + 1])


def _nsa_attn_kernel(qc_ref, qr_ref, kcd_ref, vct_ref, ks_ref, kw_ref, vst_ref, vwt_ref, gt_ref, ov_ref,
                     o_ref, og_ref, acc_ref, selb_ref, sa_ref, sb_ref):
    g = pl.program_id(1)
    qi = pl.program_id(2)
    tq = ATT_Q_TILE
    t0 = qi * tq
    n_pairs = qc_ref.shape[0]
    n_cmp = kcd_ref.shape[0]
    seq = ks_ref.shape[0]
    n_sel = seq // SLC_BLOCK
    tok1 = t0 + lax.broadcasted_iota(jnp.int32, (1, tq), 1)
    tok2 = _two(tok1)

    kb_sz = ATT_KEY_BLOCK
    head0 = g * (NSA_HEADS // NSA_KV_GROUPS)

    def gate(j, c):
        return jnp.concatenate([gt_ref[pl.ds(j * NSA_HEADS + head0 + 2 * c, 1), :],
                                gt_ref[pl.ds(j * NSA_HEADS + head0 + 2 * c + 1, 1), :]], axis=1)

    rhs = [_split_heads_rhs(qr_ref[c]) for c in range(n_pairs)]

    def sel_scores(kb, dst_ref):
        k_blk = ks_ref[pl.ds(pl.multiple_of(kb * kb_sz, kb_sz), kb_sz), :]
        for c in range(n_pairs):
            dst_ref[c] = _scores_t(k_blk, rhs[c])

    win_blocks = WINDOW // kb_sz + 1
    win_start = jnp.minimum(jnp.maximum(t0 - WINDOW, 0) // kb_sz, seq // kb_sz - win_blocks)
    w0 = pl.multiple_of(win_start * kb_sz, kb_sz)
    k_win = kw_ref[pl.ds(w0, win_blocks * kb_sz), :]

    cmp_scores = [_scores_t(kcd_ref[...], _split_heads_rhs(qc_ref[c])) for c in range(n_pairs)]
    win_scores = [_scores_t(k_win, rhs[c]) for c in range(n_pairs)]
    sel_scores(0, sa_ref)

    cmp_end = lax.broadcasted_iota(jnp.int32, (n_cmp, 1), 0) * CMP_STRIDE + (CMP_BLOCK - 1)
    vis = cmp_end <= tok2
    any_vis = (tok2 >= CMP_BLOCK - 1).astype(F32)
    psum = jnp.zeros((n_cmp, tq), F32)
    for c in range(n_pairs):
        s = jnp.where(vis, cmp_scores[c], NEG)
        m = jnp.max(s, axis=0, keepdims=True)
        e = jnp.exp2(s - m)
        p = e * (any_vis / jnp.sum(e, axis=0, keepdims=True))
        psum += p[:, :tq] + p[:, tq:]
        og_ref[c] = gate(0, c) * jnp.dot(vct_ref[...], p.astype(BF16), preferred_element_type=F32)

    imp = jnp.dot(ov_ref[...], psum, precision=lax.Precision.HIGHEST, preferred_element_type=F32)

    wrow = w0 + lax.broadcasted_iota(jnp.int32, (win_blocks * kb_sz, 1), 0)
    in_window = (wrow <= tok2) & (wrow > tok2 - WINDOW)
    for c in range(n_pairs):
        s = jnp.where(in_window, win_scores[c], NEG)
        m = jnp.max(s, axis=0, keepdims=True)
        p = jnp.exp2(s - m).astype(BF16)
        win = jnp.zeros((NSA_HEAD_DIM + SUM_ROWS, 2 * tq), F32)
        for j in range(win_blocks):
            win += jnp.dot(vwt_ref[win_start + j], p[j * kb_sz:(j + 1) * kb_sz, :], preferred_element_type=F32)
        og_ref[c] += gate(2, c) * _normalized(win, NSA_HEAD_DIM)

    blk = lax.broadcasted_iota(jnp.int32, (n_sel, 1), 0)
    cur = tok1 // SLC_BLOCK
    forced = (blk == 0) | (blk == cur) | (blk == cur - 1)
    valid = blk <= cur
    imp = jnp.where(valid, imp + jnp.where(forced, FORCE_BONUS, 0.0), NEG)
    rows8 = 8
    grp = [imp[rows8 * j:rows8 * (j + 1), :] for j in range(n_sel // rows8)]
    sub = lax.broadcasted_iota(jnp.int32, (rows8, 1), 0)
    rank = [jnp.zeros((rows8, tq), jnp.int32) for _ in grp]
    for m in range(n_sel):
        row = imp[m:m + 1, :]
        for j in range(len(grp)):
            ge = lambda: (row >= grp[j]).astype(jnp.int32)
            gt = lambda: (row > grp[j]).astype(jnp.int32)
            if rows8 * j > m:
                beats = ge()
            elif rows8 * j + rows8 - 1 <= m:
                beats = gt()
            else:
                beats = jnp.where(sub + rows8 * j > m, ge(), gt())
            rank[j] = rank[j] + beats
    rank = jnp.concatenate(rank, axis=0)
    selb_ref[...] = jnp.where((rank < SLC_TOPK) & valid, 0.0, NEG)

    last = (t0 + tq - 1) // kb_sz
    blocks_per_kb = kb_sz // SLC_BLOCK
    krow = lax.broadcasted_iota(jnp.int32, (kb_sz, 1), 0)
    acc_ref[...] = jnp.zeros_like(acc_ref)

    def sel_bias_rows(kb):
        return [_two(selb_ref[pl.ds(kb * blocks_per_kb + j, 1), :]) for j in range(blocks_per_kb)]

    def sel_below_diagonal(kb, src_ref, st):
        rows = sel_bias_rows(kb)
        return tuple(_flash_update_block_bias(src_ref, c, rows, vst_ref[kb], st[c], acc_ref)
                     for c in range(n_pairs))

    def sel_diagonal(kb, src_ref, st):
        bias = jnp.concatenate([jnp.broadcast_to(r, (SLC_BLOCK, 2 * tq)) for r in sel_bias_rows(kb)], axis=0)
        bias = jnp.where(kb * kb_sz + krow <= tok2, bias, NEG)
        return tuple(_flash_update(src_ref[c] + bias, vst_ref[kb], st[c], acc_ref, c) for c in range(n_pairs))

    def sel_two_blocks(i, st):
        kb = 2 * i
        sel_scores(kb + 1, sb_ref)
        st = sel_below_diagonal(kb, sa_ref, st)
        sel_scores(kb + 2, sa_ref)
        return sel_below_diagonal(kb + 1, sb_ref, st)

    def sel_tail_odd(st):
        sel_scores(last, sb_ref)
        st = sel_below_diagonal(last - 1, sa_ref, st)
        return sel_diagonal(last, sb_ref, st)

    init = tuple(jnp.full((1, 2 * tq), NEG, F32) for _ in range(n_pairs))
    st = lax.fori_loop(0, last // 2, sel_two_blocks, init)
    st = lax.cond(last % 2 == 0, lambda st: sel_diagonal(last, sa_ref, st), sel_tail_odd, st)

    for c in range(n_pairs):
        o_t = og_ref[c] + gate(1, c) * _normalized(acc_ref[c], NSA_HEAD_DIM)
        o_ref[c] = jnp.concatenate([o_t[:, :tq], o_t[:, tq:]], axis=0).T.astype(BF16)


def _cmp_to_slc_t(n_cmp_rows, n_sel):
    c0 = np.arange(n_cmp_rows)[None, :] * CMP_STRIDE
    s0 = np.arange(n_sel)[:, None] * SLC_BLOCK
    ov = np.minimum(c0 + CMP_BLOCK, s0 + SLC_BLOCK) - np.maximum(c0, s0)
    return (np.clip(ov, 0, None) / CMP_BLOCK).astype(np.float32)


def _nsa_attn(qc, qr, kcd, vct, ksd, kwd, vst, vwt, gt):
    b, n_chunks, s, _ = qc.shape
    groups = NSA_KV_GROUPS
    pairs = n_chunks // groups
    n_cmp = kcd.shape[2]
    nkb = s // ATT_KEY_BLOCK
    dh = NSA_HEAD_DIM
    tq = ATT_Q_TILE
    n_sel = s // SLC_BLOCK
    ov_t = jnp.asarray(_cmp_to_slc_t(n_cmp, n_sel))
    q_spec = pl.BlockSpec((None, pairs, tq, LANES), lambda bi, g, qi: (bi, g, qi, 0))
    per_group = lambda *shape: pl.BlockSpec((None, None) + shape, lambda bi, g, qi: (bi, g) + (0,) * len(shape))
    return pl.pallas_call(
        _nsa_attn_kernel,
        grid=(b, groups, s // tq),
        in_specs=[q_spec, q_spec, per_group(n_cmp, LANES), per_group(dh, n_cmp), per_group(s, LANES),
                  per_group(s, LANES), per_group(nkb, dh + SUM_ROWS, ATT_KEY_BLOCK),
                  per_group(nkb, dh + SUM_ROWS, ATT_KEY_BLOCK),
                  pl.BlockSpec((None, LANES, tq), lambda bi, g, qi: (bi, 0, qi)),
                  _resident(ov_t.shape)],
        out_specs=q_spec,
        out_shape=jax.ShapeDtypeStruct((b, n_chunks, s, LANES), BF16),
        scratch_shapes=[pltpu.VMEM((pairs, dh, 2 * tq), F32), pltpu.VMEM((pairs, dh + SUM_ROWS, 2 * tq), F32),
                        pltpu.VMEM((n_sel, tq), F32),
                        pltpu.VMEM((pairs, ATT_KEY_BLOCK, 2 * tq), F32), pltpu.VMEM((pairs, ATT_KEY_BLOCK, 2 * tq), F32)],
        compiler_params=_cparams(3),
        name="nsa_attn",
    )(qc, qr, kcd, vct, ksd, kwd, vst, vwt, gt, ov_t)


def _out_ln_kernel(o_ref, x_ref, w_ref, g_ref, b_ref, y_ref):
    o = jnp.concatenate([o_ref[j] for j in range(o_ref.shape[0])], axis=1)
    mix = jnp.dot(o, w_ref[...], preferred_element_type=F32)
    y_ref[...] = _layer_norm_rows(DEEPNORM_ALPHA * x_ref[...] + mix, g_ref[...], b_ref[...])


def _out_ln(o, x, w, g, b):
    bsz, n_chunks, s, _ = o.shape
    d = x.shape[-1]
    x_spec = pl.BlockSpec((None, ROW_TILE, d), lambda bi, i: (bi, i, 0))
    return pl.pallas_call(
        _out_ln_kernel,
        grid=(bsz, s // ROW_TILE),
        in_specs=[pl.BlockSpec((None, n_chunks, ROW_TILE, LANES), lambda bi, i: (bi, 0, i, 0)), x_spec,
                  _resident((d, d)), _resident((1, d)), _resident((1, d))],
        out_specs=x_spec,
        out_shape=jax.ShapeDtypeStruct(x.shape, F32),
        compiler_params=_cparams(2),
        name="mixer_out_ln",
    )(o, x, w, g, b)


def _pair_maps_columns(w):
    d = w.shape[0]
    return w.reshape(d, 2, DIFF_HEADS, DIFF_HEAD_DIM).transpose(0, 2, 1, 3).reshape(d, -1)


def _kv_proj_kernel(x_ref, w_ref, c_ref, sl_ref, sh_ref, k_ref, vt_ref):
    proj = jnp.dot(x_ref[...].astype(BF16), w_ref[...], preferred_element_type=F32)
    c, sl, sh = c_ref[...], sl_ref[...], sh_ref[...]
    n_kb = x_ref.shape[0] // ATT_KEY_BLOCK
    ones = jnp.ones((SUM_ROWS, ATT_KEY_BLOCK), BF16)
    for h in range(DIFF_HEADS):
        k_ref[h] = _rope_chunk(proj[:, h * LANES:(h + 1) * LANES], c, sl, sh).astype(BF16)
        vt = proj[:, D_MODEL + h * DIFF_V_DIM:D_MODEL + (h + 1) * DIFF_V_DIM].T.astype(BF16)
        for kb in range(n_kb):
            vt_ref[h, kb, :DIFF_V_DIM, :] = vt[:, kb * ATT_KEY_BLOCK:(kb + 1) * ATT_KEY_BLOCK]
            vt_ref[h, kb, DIFF_V_DIM:, :] = ones


def _kv_proj(x, w, tables):
    b, s, d = x.shape
    nkb = s // ATT_KEY_BLOCK
    kb_tile = ROW_TILE // ATT_KEY_BLOCK
    tab_spec = pl.BlockSpec((ROW_TILE, LANES), lambda bi, i: (i, 0))
    return pl.pallas_call(
        _kv_proj_kernel,
        grid=(b, s // ROW_TILE),
        in_specs=[pl.BlockSpec((None, ROW_TILE, d), lambda bi, i: (bi, i, 0)), _resident(w.shape),
                  tab_spec, tab_spec, tab_spec],
        out_specs=[pl.BlockSpec((None, DIFF_HEADS, ROW_TILE, LANES), lambda bi, i: (bi, 0, i, 0)),
                   pl.BlockSpec((None, DIFF_HEADS, kb_tile, DIFF_V_DIM + SUM_ROWS, ATT_KEY_BLOCK),
                                lambda bi, i: (bi, 0, i, 0, 0))],
        out_shape=[jax.ShapeDtypeStruct((b, DIFF_HEADS, s, LANES), BF16),
                   jax.ShapeDtypeStruct((b, DIFF_HEADS, nkb, DIFF_V_DIM + SUM_ROWS, ATT_KEY_BLOCK), BF16)],
        compiler_params=_cparams(2),
        name="diff_kv_proj",
    )(x, w, *tables)


def _dq_proj_kernel(x_ref, w_ref, c_ref, sl_ref, sh_ref, q_ref):
    proj = jnp.dot(x_ref[...].astype(BF16), w_ref[...], preferred_element_type=F32)
    c, sl, sh = c_ref[...], sl_ref[...], sh_ref[...]
    for h in range(DIFF_HEADS):
        q = proj[:, h * LANES:(h + 1) * LANES] * (DIFF_HEAD_DIM ** -0.5 * LOG2E)
        q_ref[h] = _rope_chunk(q, c, sl, sh).astype(BF16)


def _dq_proj(x, w, tables):
    b, s, d = x.shape
    tab_spec = pl.BlockSpec((ROW_TILE, LANES), lambda bi, i: (i, 0))
    return pl.pallas_call(
        _dq_proj_kernel,
        grid=(b, s // ROW_TILE),
        in_specs=[pl.BlockSpec((None, ROW_TILE, d), lambda bi, i: (bi, i, 0)), _resident(w.shape),
                  tab_spec, tab_spec, tab_spec],
        out_specs=pl.BlockSpec((None, DIFF_HEADS, ROW_TILE, LANES), lambda bi, i: (bi, 0, i, 0)),
        out_shape=jax.ShapeDtypeStruct((b, DIFF_HEADS, s, LANES), BF16),
        compiler_params=_cparams(2),
        name="diff_q_proj",
    )(x, w, *tables)


def _diff_attn_kernel(q_ref, k_ref, vt_ref, lam_ref, gain_ref, o_ref, acc_ref, sa_ref, sb_ref, *, lambda_init):
    qi = pl.program_id(2)
    n_heads = q_ref.shape[0]
    tq = ATT_Q_TILE
    t0 = qi * tq
    kb_sz = ATT_KEY_BLOCK
    tok2 = _two(t0 + lax.broadcasted_iota(jnp.int32, (1, tq), 1))
    krow = lax.broadcasted_iota(jnp.int32, (kb_sz, 1), 0)
    rhs = [_split_heads_rhs(q_ref[h]) for h in range(n_heads)]
    last = (t0 + tq - 1) // kb_sz
    acc_ref[...] = jnp.zeros_like(acc_ref)

    def scores_into(kb, dst_ref):
        k0 = pl.multiple_of(kb * kb_sz, kb_sz)
        for h in range(n_heads):
            dst_ref[h] = _scores_t(k_ref[h, pl.ds(k0, kb_sz), :], rhs[h])

    def consume(kb, src_ref, st, maybe_diagonal):
        if maybe_diagonal:
            visible = kb * kb_sz + krow <= tok2
        out = []
        for h in range(n_heads):
            s = src_ref[h]
            if maybe_diagonal:
                s = jnp.where(visible, s, NEG)
            out.append(_flash_update(s, vt_ref[h, kb], st[h], acc_ref, h))
        return tuple(out)

    def two_blocks(i, st):
        kb = 2 * i
        scores_into(kb + 1, sb_ref)
        st = consume(kb, sa_ref, st, False)
        scores_into(kb + 2, sa_ref)
        return consume(kb + 1, sb_ref, st, False)

    def tail_odd(st):
        scores_into(last, sb_ref)
        st = consume(last - 1, sa_ref, st, False)
        return consume(last, sb_ref, st, True)

    init = tuple(jnp.full((1, 2 * tq), NEG, F32) for _ in range(n_heads))
    scores_into(0, sa_ref)
    st = lax.fori_loop(0, last // 2, two_blocks, init)
    st = lax.cond(last % 2 == 0, lambda st: consume(last, sa_ref, st, True), tail_odd, st)

    lp = lam_ref[...]
    lam = (jnp.exp(jnp.sum(lp[0:1] * lp[1:2], axis=1, keepdims=True))
           - jnp.exp(jnp.sum(lp[2:3] * lp[3:4], axis=1, keepdims=True)) + lambda_init)
    for h in range(n_heads):
        a = _normalized(acc_ref[h], DIFF_V_DIM)
        o = a[:, :tq] - lam * a[:, tq:]
        ms = jnp.mean(o * o, axis=0, keepdims=True)
        o = o * lax.rsqrt(ms + NORM_EPS) * gain_ref[...] * (1.0 - lambda_init)
        o_ref[h] = o.T.astype(BF16)


def _diff_attn(q, k, vt, lam_params, gain, lambda_init):
    b, h, s, _ = q.shape
    nkb = s // ATT_KEY_BLOCK
    tq = ATT_Q_TILE
    hps = DIFF_HEADS_PER_STEP
    per_head = lambda *shape: pl.BlockSpec((None, hps) + shape, lambda bi, hi, qi: (bi, hi) + (0,) * len(shape))
    q_spec = pl.BlockSpec((None, hps, tq, LANES), lambda bi, hi, qi: (bi, hi, qi, 0))
    return pl.pallas_call(
        functools.partial(_diff_attn_kernel, lambda_init=lambda_init),
        grid=(b, h // hps, s // tq),
        in_specs=[q_spec, per_head(s, LANES), per_head(nkb, DIFF_V_DIM + SUM_ROWS, ATT_KEY_BLOCK),
                  _resident(lam_params.shape), _resident(gain.shape)],
        out_specs=q_spec,
        out_shape=jax.ShapeDtypeStruct((b, h, s, LANES), BF16),
        scratch_shapes=[pltpu.VMEM((hps, DIFF_V_DIM + SUM_ROWS, 2 * tq), F32),
                        pltpu.VMEM((hps, ATT_KEY_BLOCK, 2 * tq), F32), pltpu.VMEM((hps, ATT_KEY_BLOCK, 2 * tq), F32)],
        compiler_params=_cparams(3),
        name="diff_attn",
    )(q, k, vt, lam_params, gain)


def kernel(x, ffn1_w_gate, ffn1_w_up, ffn1_w_down, ffn2_w_gate, ffn2_w_up, ffn2_w_down, ln_g, ln_b, a_w_in,
           a_cmp_pos_k, a_cmp_w1_k, a_cmp_b1_k, a_cmp_w2_k, a_cmp_pos_v, a_cmp_w1_v, a_cmp_b1_v, a_cmp_w2_v,
           a_w_out, b_w_kv_shared, b_w_q, b_lambda_q1, b_lambda_k1, b_lambda_q2, b_lambda_k2, b_subln_g, b_w_out):
    bsz, seq, d = x.shape
    assert d == D_MODEL and seq % ROW_TILE == 0 and seq >= WINDOW + 2 * ATT_KEY_BLOCK
    tables = _rope_tables(seq)
    bf = lambda w: w.astype(BF16)
    row = lambda v: v[None, :]

    def ffn(xv, wg, wu, wd, g, b):
        return _ffn_ln(xv.reshape(bsz * seq, d), bf(wg), bf(wu), bf(wd), row(g), row(b)).reshape(bsz, seq, d)

    k12 = vt = None
    for layer in range(DEPTH):
        x = ffn(x, ffn1_w_gate[layer], ffn1_w_up[layer], ffn1_w_down[layer], ln_g[layer, 0], ln_b[layer, 0])
        if layer < N_A_LAYERS:
            qc, qr, kc, vc, ksd, kwd, vst, vwt, gt = _nsa_proj(x, _nsa_proj_weight(a_w_in[layer]), tables)
            wk = _compress_weights(a_cmp_pos_k[layer], a_cmp_w1_k[layer], a_cmp_b1_k[layer], a_cmp_w2_k[layer], True)
            wv = _compress_weights(a_cmp_pos_v[layer], a_cmp_w1_v[layer], a_cmp_b1_v[layer], a_cmp_w2_v[layer], False)
            kcd, vct = _compress(kc, vc, wk, wv)
            o = _nsa_attn(qc, qr, kcd, vct, ksd, kwd, vst, vwt, gt)
            w_out = a_w_out[layer]
        else:
            j = layer - N_A_LAYERS
            lambda_init = 0.8 - 0.6 * float(np.exp(-0.3 * layer))
            q = _dq_proj(x, bf(_pair_maps_columns(b_w_q[j])), tables)
            lam_params = jnp.pad(jnp.stack([b_lambda_q1[j], b_lambda_k1[j], b_lambda_q2[j], b_lambda_k2[j]]),
                                 ((0, 4), (0, LANES - DIFF_HEAD_DIM)))
            gain = jnp.broadcast_to(b_subln_g[j][:, None], (DIFF_V_DIM, ATT_Q_TILE))
            o = _diff_attn(q, k12, vt, lam_params, gain, lambda_init)
            w_out = b_w_out[j]
        x = _out_ln(o, x, bf(w_out), row(ln_g[layer, 1]), row(ln_b[layer, 1]))
        x = ffn(x, ffn2_w_gate[layer], ffn2_w_up[layer], ffn2_w_down[layer], ln_g[layer, 2], ln_b[layer, 2])
        if layer == N_A_LAYERS - 1:
            w_kv = jnp.concatenate([_pair_maps_columns(b_w_kv_shared[:, :D_MODEL]), b_w_kv_shared[:, D_MODEL:]], axis=1)
            k12, vt = _kv_proj(x, bf(w_kv), tables)
    return x
```

```python
import functools

import numpy as np
import jax
import jax.numpy as jnp
from jax import lax
from jax.experimental import pallas as pl
from jax.experimental.pallas import tpu as pltpu

F32 = jnp.float32
BF16 = jnp.bfloat16

D_MODEL = 1024
DEPTH = 4
N_A_LAYERS = DEPTH // 2
NSA_HEADS = 16
NSA_HEAD_DIM = D_MODEL // NSA_HEADS
NSA_KV_GROUPS = 2
NSA_KV_WIDTH = NSA_KV_GROUPS * NSA_HEAD_DIM
NSA_GATE_WIDTH = 3 * NSA_HEADS
CMP_BLOCK = 32
CMP_STRIDE = 16
CMP_MLP_HIDDEN = 128
SLC_BLOCK = 64
SLC_TOPK = 16
WINDOW = 512
FORCE_BONUS = 1000.0
DIFF_HEADS = 8
DIFF_HEAD_DIM = D_MODEL // (2 * DIFF_HEADS)
DIFF_V_DIM = 2 * DIFF_HEAD_DIM
ROPE_THETA = 500000.0
ROPE_FRACTION = 4
D_FF = 2816
DEEPNORM_ALPHA = (2 * DEPTH) ** 0.25
NORM_EPS = 1e-5
NEG = -1e30

LANES = 128
ROW_TILE = 512
FFN_COL_CHUNK = 512
ATT_Q_TILE = 128
ATT_KEY_BLOCK = 256
DIFF_HEADS_PER_STEP = 8
SUM_ROWS = 16
LOG2E = 1.4426950408889634
VMEM_LIMIT_BYTES = 56 * 1024 * 1024
N_CHUNKS = D_MODEL // LANES


def _cparams(n_axes):
    return pltpu.CompilerParams(dimension_semantics=("parallel",) * n_axes,
                                vmem_limit_bytes=VMEM_LIMIT_BYTES)


def _resident(shape):
    zeros = (0,) * len(shape)
    return pl.BlockSpec(shape, lambda *_: zeros, pipeline_mode=pl.Buffered(1))


def _layer_norm_rows(r, g, b):
    mu = jnp.mean(r, axis=-1, keepdims=True)
    xc = r - mu
    var = jnp.mean(xc * xc, axis=-1, keepdims=True)
    return xc * lax.rsqrt(var + NORM_EPS) * g + b


def _ffn_ln_kernel(x_ref, wg_ref, wu_ref, wd_ref, g_ref, b_ref, o_ref, h_ref):
    x = x_ref[...]
    xb = x.astype(BF16)
    d_ff = wg_ref.shape[1]
    for c0 in range(0, d_ff, FFN_COL_CHUNK):
        c1 = min(c0 + FFN_COL_CHUNK, d_ff)
        gate = jnp.dot(xb, wg_ref[:, c0:c1], preferred_element_type=F32)
        up = jnp.dot(xb, wu_ref[:, c0:c1], preferred_element_type=F32)
        silu = gate * (1.0 / (1.0 + jnp.exp(-gate)))
        h_ref[:, c0:c1] = (silu * up).astype(BF16)
    y = jnp.dot(h_ref[...], wd_ref[...], preferred_element_type=F32)
    o_ref[...] = _layer_norm_rows(DEEPNORM_ALPHA * x + 0.5 * y, g_ref[...], b_ref[...])


def _ffn_ln(x2d, wg, wu, wd, g, b):
    n, d = x2d.shape
    d_ff = wg.shape[1]
    return pl.pallas_call(
        _ffn_ln_kernel,
        grid=(n // ROW_TILE,),
        in_specs=[pl.BlockSpec((ROW_TILE, d), lambda i: (i, 0)),
                  _resident((d, d_ff)), _resident((d, d_ff)), _resident((d_ff, d)),
                  _resident((1, d)), _resident((1, d))],
        out_specs=pl.BlockSpec((ROW_TILE, d), lambda i: (i, 0)),
        out_shape=jax.ShapeDtypeStruct((n, d), F32),
        scratch_shapes=[pltpu.VMEM((ROW_TILE, d_ff), BF16)],
        compiler_params=_cparams(1),
        name="ffn_ln",
    )(x2d, wg, wu, wd, g, b)


def _rope_tables(seq):
    r = NSA_HEAD_DIM // ROPE_FRACTION
    half = r // 2
    inv = ROPE_THETA ** (-jnp.arange(half, dtype=F32) * 2.0 / r)
    ang = jnp.arange(seq).astype(F32)[:, None] * inv[None, :]
    cos, sin = jnp.cos(ang), jnp.sin(ang)
    pad = jnp.zeros((seq, NSA_HEAD_DIM - r), F32)
    zero = jnp.zeros_like(sin)
    c_head = jnp.concatenate([cos, cos, pad + 1.0], axis=1)
    s_lo_head = jnp.concatenate([-sin, zero, pad], axis=1)
    s_hi_head = jnp.concatenate([zero, sin, pad], axis=1)
    rep = lambda t: jnp.concatenate([t, t], axis=1)
    return rep(c_head), rep(s_lo_head), rep(s_hi_head)


def _rope_chunk(x, c, s_lo, s_hi):
    half = NSA_HEAD_DIM // ROPE_FRACTION // 2
    return x * c + pltpu.roll(x, LANES - half, 1) * s_lo + pltpu.roll(x, half, 1) * s_hi


def _nsa_proj_kernel(x_ref, w_ref, c_ref, sl_ref, sh_ref,
                     qc_ref, qr_ref, kc_ref, vc_ref, ks_ref, kw_ref, vst_ref, vwt_ref, gt_ref):
    proj = jnp.dot(x_ref[...].astype(BF16), w_ref[...], preferred_element_type=F32)
    c, sl, sh = c_ref[...], sl_ref[...], sh_ref[...]
    col = lambda j: proj[:, j * LANES:(j + 1) * LANES]
    for j in range(N_CHUNKS):
        q = col(j) * (NSA_HEAD_DIM ** -0.5 * LOG2E)
        qc_ref[j] = q.astype(BF16)
        qr_ref[j] = _rope_chunk(q, c, sl, sh).astype(BF16)
    base = N_CHUNKS
    kc_ref[...] = col(base)
    vc_ref[...] = col(base + 1)
    for g in range(NSA_KV_GROUPS):
        ks_ref[g] = _rope_chunk(col(base + 2 + g), c, sl, sh).astype(BF16)
        kw_ref[g] = _rope_chunk(col(base + 4 + g), c, sl, sh).astype(BF16)
    n_kb = x_ref.shape[0] // ATT_KEY_BLOCK
    ones = jnp.ones((SUM_ROWS, ATT_KEY_BLOCK), BF16)
    for src, dst in ((base + 6, vst_ref), (base + 7, vwt_ref)):
        vt = col(src).T.astype(BF16)
        for g in range(NSA_KV_GROUPS):
            for kb in range(n_kb):
                dst[g, kb, :NSA_HEAD_DIM, :] = vt[g * NSA_HEAD_DIM:(g + 1) * NSA_HEAD_DIM,
                                                  kb * ATT_KEY_BLOCK:(kb + 1) * ATT_KEY_BLOCK]
                dst[g, kb, NSA_HEAD_DIM:, :] = ones
    gates = col(base + 8)
    gt_ref[...] = (1.0 / (1.0 + jnp.exp(-gates))).T


def _nsa_proj_weight(w_in):
    d = D_MODEL
    kvw = NSA_KV_WIDTH
    q, kc, vc, ks, vs, kw, vw, gates = jnp.split(
        w_in, [int(c) for c in np.cumsum([d] + [kvw] * 6)], axis=1)
    dh = NSA_HEAD_DIM
    dup = lambda t: jnp.concatenate([t[:, :dh], t[:, :dh], t[:, dh:], t[:, dh:]], axis=1)
    gates = jnp.pad(gates, ((0, 0), (0, LANES - NSA_GATE_WIDTH)))
    return jnp.concatenate([q, kc, vc, dup(ks), dup(kw), vs, vw, gates], axis=1).astype(BF16)


def _nsa_proj(x, w, tables):
    b, s, d = x.shape
    ncol = w.shape[1]
    nkb = s // ATT_KEY_BLOCK
    kb_tile = ROW_TILE // ATT_KEY_BLOCK
    dh = NSA_HEAD_DIM + SUM_ROWS
    tab_spec = pl.BlockSpec((ROW_TILE, LANES), lambda bi, i: (i, 0))
    chunk_spec = lambda n: pl.BlockSpec((None, n, ROW_TILE, LANES), lambda bi, i: (bi, 0, i, 0))
    row_spec = pl.BlockSpec((None, ROW_TILE, LANES), lambda bi, i: (bi, i, 0))
    vt_spec = pl.BlockSpec((None, NSA_KV_GROUPS, kb_tile, dh, ATT_KEY_BLOCK), lambda bi, i: (bi, 0, i, 0, 0))
    return pl.pallas_call(
        _nsa_proj_kernel,
        grid=(b, s // ROW_TILE),
        in_specs=[pl.BlockSpec((None, ROW_TILE, d), lambda bi, i: (bi, i, 0)),
                  _resident((d, ncol)), tab_spec, tab_spec, tab_spec],
        out_specs=[chunk_spec(N_CHUNKS), chunk_spec(N_CHUNKS), row_spec, row_spec,
                   chunk_spec(NSA_KV_GROUPS), chunk_spec(NSA_KV_GROUPS), vt_spec, vt_spec,
                   pl.BlockSpec((None, LANES, ROW_TILE), lambda bi, i: (bi, 0, i))],
        out_shape=[jax.ShapeDtypeStruct((b, N_CHUNKS, s, LANES), BF16),
                   jax.ShapeDtypeStruct((b, N_CHUNKS, s, LANES), BF16),
                   jax.ShapeDtypeStruct((b, s, LANES), F32),
                   jax.ShapeDtypeStruct((b, s, LANES), F32),
                   jax.ShapeDtypeStruct((b, NSA_KV_GROUPS, s, LANES), BF16),
                   jax.ShapeDtypeStruct((b, NSA_KV_GROUPS, s, LANES), BF16),
                   jax.ShapeDtypeStruct((b, NSA_KV_GROUPS, nkb, dh, ATT_KEY_BLOCK), BF16),
                   jax.ShapeDtypeStruct((b, NSA_KV_GROUPS, nkb, dh, ATT_KEY_BLOCK), BF16),
                   jax.ShapeDtypeStruct((b, LANES, s), F32)],
        compiler_params=_cparams(2),
        name="nsa_proj",
    )(x, w, *tables)


def _gelu_tanh(x):
    return 0.5 * x * (1.0 + jnp.tanh(np.sqrt(2.0 / np.pi).astype(np.float32) * (x + 0.044715 * (x * x * x))))


def _compress_kernel(kch_ref, vch_ref, pk_ref, pv_ref, w1k_ref, w1v_ref, b1k_ref, b1v_ref, w2k_ref, w2v_ref,
                     kcd_ref, vct_ref):
    n_rows = kch_ref.shape[0]

    def mlp(ch_ref, pos_ref, w1_ref, b1_ref, w2_ref):
        ch = ch_ref[...]
        lo = jnp.dot((ch + pos_ref[0:1, :]).astype(BF16), w1_ref[0], preferred_element_type=F32)
        hi = jnp.dot((ch + pos_ref[1:2, :]).astype(BF16), w1_ref[1], preferred_element_type=F32)
        hid = lo + pltpu.roll(hi, n_rows - 1, 0) + b1_ref[...]
        return jnp.dot(_gelu_tanh(hid).astype(BF16), w2_ref[...], preferred_element_type=F32)

    kcd = mlp(kch_ref, pk_ref, w1k_ref, b1k_ref, w2k_ref)
    for g in range(NSA_KV_GROUPS):
        kcd_ref[g] = kcd[:, g * LANES:(g + 1) * LANES].astype(BF16)
    vct = mlp(vch_ref, pv_ref, w1v_ref, b1v_ref, w2v_ref).T
    for g in range(NSA_KV_GROUPS):
        vct_ref[g] = vct[g * NSA_HEAD_DIM:(g + 1) * NSA_HEAD_DIM, :].astype(BF16)


def _compress_weights(pos, w1, b1, w2, dup_out):
    g_n, dh, hid = NSA_KV_GROUPS, NSA_HEAD_DIM, CMP_MLP_HIDDEN
    w1r = w1.reshape(2, CMP_STRIDE, dh, hid)
    eye = jnp.eye(g_n, dtype=w1.dtype)
    w1x = jnp.einsum('hldm,gk->hlgdkm', w1r, eye).reshape(2, CMP_STRIDE * g_n * dh, g_n * hid)
    posx = jnp.broadcast_to(pos.reshape(2, CMP_STRIDE, 1, dh), (2, CMP_STRIDE, g_n, dh)).reshape(2, -1)
    b1x = jnp.tile(b1, g_n)[None, :]
    reps = 2 if dup_out else 1
    w2x = jnp.einsum('md,gk->gmkd', w2, eye)
    w2x = jnp.broadcast_to(w2x[:, :, :, None, :], (g_n, hid, g_n, reps, dh)).reshape(g_n * hid, g_n * reps * dh)
    return posx, w1x.astype(BF16), b1x, w2x.astype(BF16)


def _compress(kc, vc, wk, wv):
    b, s, _ = kc.shape
    n_rows = s // CMP_STRIDE
    width = CMP_STRIDE * LANES
    kch = kc.reshape(b, n_rows, width)
    vch = vc.reshape(b, n_rows, width)
    pk, w1k, b1k, w2k = wk
    pv, w1v, b1v, w2v = wv
    ch_spec = pl.BlockSpec((None, n_rows, width), lambda bi: (bi, 0, 0))
    return pl.pallas_call(
        _compress_kernel,
        grid=(b,),
        in_specs=[ch_spec, ch_spec, _resident(pk.shape), _resident(pv.shape), _resident(w1k.shape),
                  _resident(w1v.shape), _resident(b1k.shape), _resident(b1v.shape), _resident(w2k.shape),
                  _resident(w2v.shape)],
        out_specs=[pl.BlockSpec((None, NSA_KV_GROUPS, n_rows, LANES), lambda bi: (bi, 0, 0, 0)),
                   pl.BlockSpec((None, NSA_KV_GROUPS, NSA_HEAD_DIM, n_rows), lambda bi: (bi, 0, 0, 0))],
        out_shape=[jax.ShapeDtypeStruct((b, NSA_KV_GROUPS, n_rows, LANES), BF16),
                   jax.ShapeDtypeStruct((b, NSA_KV_GROUPS, NSA_HEAD_DIM, n_rows), BF16)],
        compiler_params=_cparams(1),
        name="nsa_compress",
    )(kch, vch, pk, pv, w1k, w1v, b1k, b1v, w2k, w2v)


def _split_heads_rhs(q):
    lane = lax.broadcasted_iota(jnp.int32, q.shape, 1)
    zero = jnp.zeros_like(q)
    return jnp.concatenate([jnp.where(lane < LANES // 2, q, zero), jnp.where(lane >= LANES // 2, q, zero)], axis=0)


def _scores_t(k, rhs):
    return lax.dot_general(k, rhs, (((1,), (1,)), ((), ())), preferred_element_type=F32)


def _two(row):
    return jnp.concatenate([row, row], axis=1)


def _flash_update(s, vt_blk, m, acc_ref, idx):
    m_new = jnp.maximum(m, jnp.max(s, axis=0, keepdims=True))
    alpha = jnp.exp2(m - m_new)
    p = jnp.exp2(s - m_new).astype(BF16)
    acc_ref[idx] = alpha * acc_ref[idx] + jnp.dot(vt_blk, p, preferred_element_type=F32)
    return m_new


def _flash_update_block_bias(src_ref, idx, bias_rows, vt_blk, m, acc_ref):
    rows = src_ref.shape[1] // len(bias_rows)
    piece = lambda j: src_ref[idx, j * rows:(j + 1) * rows, :]
    m_new = m
    for j, b in enumerate(bias_rows):
        m_new = jnp.maximum(m_new, jnp.max(piece(j), axis=0, keepdims=True) + b)
    alpha = jnp.exp2(m - m_new)
    p = jnp.concatenate([jnp.exp2(piece(j) + (b - m_new)).astype(BF16) for j, b in enumerate(bias_rows)], axis=0)
    acc_ref[idx] = alpha * acc_ref[idx] + jnp.dot(vt_blk, p, preferred_element_type=F32)
    return m_new


def _normalized(acc, dv):
    return acc[:dv] * (1.0 / acc[dv:dv + 1])


def _nsa_attn_kernel(qc_ref, qr_ref, kcd_ref, vct_ref, ks_ref, kw_ref, vst_ref, vwt_ref, gt_ref, ov_ref,
                     o_ref, og_ref, acc_ref, selb_ref, sa_ref, sb_ref):
    qi = pl.program_id(1)
    tq = ATT_Q_TILE
    t0 = qi * tq
    n_pairs = qc_ref.shape[0]
    n_groups = kcd_ref.shape[0]
    group_of = lambda c: c // (n_pairs // n_groups)
    n_cmp = kcd_ref.shape[1]
    seq = ks_ref.shape[1]
    n_sel = seq // SLC_BLOCK
    tok1 = t0 + lax.broadcasted_iota(jnp.int32, (1, tq), 1)
    tok2 = _two(tok1)

    kb_sz = ATT_KEY_BLOCK

    def gate(j, c):
        return jnp.concatenate([gt_ref[j * NSA_HEADS + 2 * c:j * NSA_HEADS + 2 * c + 1, :],
                                gt_ref[j * NSA_HEADS + 2 * c + 1:j * NSA_HEADS + 2 * c + 2, :]], axis=1)

    rhs = [_split_heads_rhs(qr_ref[c]) for c in range(n_pairs)]

    def sel_scores(kb, dst_ref):
        k0 = pl.multiple_of(kb * kb_sz, kb_sz)
        for c in range(n_pairs):
            dst_ref[c] = _scores_t(ks_ref[group_of(c), pl.ds(k0, kb_sz), :], rhs[c])

    win_blocks = WINDOW // kb_sz + 1
    win_start = jnp.minimum(jnp.maximum(t0 - WINDOW, 0) // kb_sz, seq // kb_sz - win_blocks)
    w0 = pl.multiple_of(win_start * kb_sz, kb_sz)

    cmp_scores = [_scores_t(kcd_ref[group_of(c)], _split_heads_rhs(qc_ref[c])) for c in range(n_pairs)]
    win_scores = [_scores_t(kw_ref[group_of(c), pl.ds(w0, win_blocks * kb_sz), :], rhs[c]) for c in range(n_pairs)]
    sel_scores(0, sa_ref)

    cmp_end = lax.broadcasted_iota(jnp.int32, (n_cmp, 1), 0) * CMP_STRIDE + (CMP_BLOCK - 1)
    vis = cmp_end <= tok2
    any_vis = (tok2 >= CMP_BLOCK - 1).astype(F32)
    psum = [jnp.zeros((n_cmp, tq), F32) for _ in range(n_groups)]
    for c in range(n_pairs):
        s = jnp.where(vis, cmp_scores[c], NEG)
        m = jnp.max(s, axis=0, keepdims=True)
        e = jnp.exp2(s - m)
        p = e * (any_vis / jnp.sum(e, axis=0, keepdims=True))
        psum[group_of(c)] += p[:, :tq] + p[:, tq:]
        og_ref[c] = gate(0, c) * jnp.dot(vct_ref[group_of(c)], p.astype(BF16), preferred_element_type=F32)

    imp_raw = [jnp.dot(ov_ref[...], ps, precision=lax.Precision.HIGHEST, preferred_element_type=F32)
               for ps in psum]

    wrow = w0 + lax.broadcasted_iota(jnp.int32, (win_blocks * kb_sz, 1), 0)
    in_window = (wrow <= tok2) & (wrow > tok2 - WINDOW)
    for c in range(n_pairs):
        s = jnp.where(in_window, win_scores[c], NEG)
        m = jnp.max(s, axis=0, keepdims=True)
        p = jnp.exp2(s - m).astype(BF16)
        win = jnp.zeros((NSA_HEAD_DIM + SUM_ROWS, 2 * tq), F32)
        for j in range(win_blocks):
            win += jnp.dot(vwt_ref[group_of(c), win_start + j], p[j * kb_sz:(j + 1) * kb_sz, :],
                           preferred_element_type=F32)
        og_ref[c] += gate(2, c) * _normalized(win, NSA_HEAD_DIM)

    blk = lax.broadcasted_iota(jnp.int32, (n_sel, 1), 0)
    cur = tok1 // SLC_BLOCK
    forced = (blk == 0) | (blk == cur) | (blk == cur - 1)
    valid = blk <= cur
    rows8 = 8
    sub = lax.broadcasted_iota(jnp.int32, (rows8, 1), 0)
    for gi in range(n_groups):
        imp = jnp.where(valid, imp_raw[gi] + jnp.where(forced, FORCE_BONUS, 0.0), NEG)
        grp = [imp[rows8 * j:rows8 * (j + 1), :] for j in range(n_sel // rows8)]
        rank = [jnp.zeros((rows8, tq), jnp.int32) for _ in grp]
        for m in range(n_sel):
            row = imp[m:m + 1, :]
            for j in range(len(grp)):
                ge = lambda: (row >= grp[j]).astype(jnp.int32)
                gt = lambda: (row > grp[j]).astype(jnp.int32)
                if rows8 * j > m:
                    beats = ge()
                elif rows8 * j + rows8 - 1 <= m:
                    beats = gt()
                else:
                    beats = jnp.where(sub + rows8 * j > m, ge(), gt())
                rank[j] = rank[j] + beats
        rank = jnp.concatenate(rank, axis=0)
        selb_ref[gi] = jnp.where((rank < SLC_TOPK) & valid, 0.0, NEG)

    last = (t0 + tq - 1) // kb_sz
    blocks_per_kb = kb_sz // SLC_BLOCK
    krow = lax.broadcasted_iota(jnp.int32, (kb_sz, 1), 0)
    acc_ref[...] = jnp.zeros_like(acc_ref)

    def sel_bias_rows(kb):
        return [[_two(selb_ref[gi, pl.ds(kb * blocks_per_kb + j, 1), :]) for j in range(blocks_per_kb)]
                for gi in range(n_groups)]

    def sel_below_diagonal(kb, src_ref, st):
        rows = sel_bias_rows(kb)
        return tuple(_flash_update_block_bias(src_ref, c, rows[group_of(c)], vst_ref[group_of(c), kb], st[c], acc_ref)
                     for c in range(n_pairs))

    def sel_diagonal(kb, src_ref, st):
        causal = kb * kb_sz + krow <= tok2
        bias = [jnp.where(causal, jnp.concatenate([jnp.broadcast_to(r, (SLC_BLOCK, 2 * tq)) for r in rows], axis=0),
                          NEG) for rows in sel_bias_rows(kb)]
        return tuple(_flash_update(src_ref[c] + bias[group_of(c)], vst_ref[group_of(c), kb], st[c], acc_ref, c)
                     for c in range(n_pairs))

    def sel_two_blocks(i, st):
        kb = 2 * i
        sel_scores(kb + 1, sb_ref)
        st = sel_below_diagonal(kb, sa_ref, st)
        sel_scores(kb + 2, sa_ref)
        return sel_below_diagonal(kb + 1, sb_ref, st)

    def sel_tail_odd(st):
        sel_scores(last, sb_ref)
        st = sel_below_diagonal(last - 1, sa_ref, st)
        return sel_diagonal(last, sb_ref, st)

    init = tuple(jnp.full((1, 2 * tq), NEG, F32) for _ in range(n_pairs))
    st = lax.fori_loop(0, last // 2, sel_two_blocks, init)
    st = lax.cond(last % 2 == 0, lambda st: sel_diagonal(last, sa_ref, st), sel_tail_odd, st)

    for c in range(n_pairs):
        o_t = og_ref[c] + gate(1, c) * _normalized(acc_ref[c], NSA_HEAD_DIM)
        o_ref[c] = jnp.concatenate([o_t[:, :tq], o_t[:, tq:]], axis=0).T.astype(BF16)


def _cmp_to_slc_t(n_cmp_rows, n_sel):
    c0 = np.arange(n_cmp_rows)[None, :] * CMP_STRIDE
    s0 = np.arange(n_sel)[:, None] * SLC_BLOCK
    ov = np.minimum(c0 + CMP_BLOCK, s0 + SLC_BLOCK) - np.maximum(c0, s0)
    return (np.clip(ov, 0, None) / CMP_BLOCK).astype(np.float32)


def _nsa_attn(qc, qr, kcd, vct, ksd, kwd, vst, vwt, gt):
    b, pairs, s, _ = qc.shape
    groups = NSA_KV_GROUPS
    n_cmp = kcd.shape[2]
    nkb = s // ATT_KEY_BLOCK
    dh = NSA_HEAD_DIM
    tq = ATT_Q_TILE
    n_sel = s // SLC_BLOCK
    ov_t = jnp.asarray(_cmp_to_slc_t(n_cmp, n_sel))
    q_spec = pl.BlockSpec((None, pairs, tq, LANES), lambda bi, qi: (bi, 0, qi, 0))
    per_batch = lambda *shape: pl.BlockSpec((None,) + shape, lambda bi, qi: (bi,) + (0,) * len(shape))
    return pl.pallas_call(
        _nsa_attn_kernel,
        grid=(b, s // tq),
        in_specs=[q_spec, q_spec, per_batch(groups, n_cmp, LANES), per_batch(groups, dh, n_cmp),
                  per_batch(groups, s, LANES), per_batch(groups, s, LANES),
                  per_batch(groups, nkb, dh + SUM_ROWS, ATT_KEY_BLOCK),
                  per_batch(groups, nkb, dh + SUM_ROWS, ATT_KEY_BLOCK),
                  pl.BlockSpec((None, LANES, tq), lambda bi, qi: (bi, 0, qi)),
                  _resident(ov_t.shape)],
        out_specs=q_spec,
        out_shape=jax.ShapeDtypeStruct((b, pairs, s, LANES), BF16),
        scratch_shapes=[pltpu.VMEM((pairs, dh, 2 * tq), F32), pltpu.VMEM((pairs, dh + SUM_ROWS, 2 * tq), F32),
                        pltpu.VMEM((groups, n_sel, tq), F32),
                        pltpu.VMEM((pairs, ATT_KEY_BLOCK, 2 * tq), F32), pltpu.VMEM((pairs, ATT_KEY_BLOCK, 2 * tq), F32)],
        compiler_params=_cparams(2),
        name="nsa_attn",
    )(qc, qr, kcd, vct, ksd, kwd, vst, vwt, gt, ov_t)


def _out_ln_kernel(o_ref, x_ref, w_ref, g_ref, b_ref, y_ref):
    o = jnp.concatenate([o_ref[j] for j in range(o_ref.shape[0])], axis=1)
    mix = jnp.dot(o, w_ref[...], preferred_element_type=F32)
    y_ref[...] = _layer_norm_rows(DEEPNORM_ALPHA * x_ref[...] + mix, g_ref[...], b_ref[...])


def _out_ln(o, x, w, g, b):
    bsz, n_chunks, s, _ = o.shape
    d = x.shape[-1]
    x_spec = pl.BlockSpec((None, ROW_TILE, d), lambda bi, i: (bi, i, 0))
    return pl.pallas_call(
        _out_ln_kernel,
        grid=(bsz, s // ROW_TILE),
        in_specs=[pl.BlockSpec((None, n_chunks, ROW_TILE, LANES), lambda bi, i: (bi, 0, i, 0)), x_spec,
                  _resident((d, d)), _resident((1, d)), _resident((1, d))],
        out_specs=x_spec,
        out_shape=jax.ShapeDtypeStruct(x.shape, F32),
        compiler_params=_cparams(2),
        name="mixer_out_ln",
    )(o, x, w, g, b)


def _pair_maps_columns(w):
    d = w.shape[0]
    return w.reshape(d, 2, DIFF_HEADS, DIFF_HEAD_DIM).transpose(0, 2, 1, 3).reshape(d, -1)


def _kv_proj_kernel(x_ref, w_ref, c_ref, sl_ref, sh_ref, k_ref, vt_ref):
    proj = jnp.dot(x_ref[...].astype(BF16), w_ref[...], preferred_element_type=F32)
    c, sl, sh = c_ref[...], sl_ref[...], sh_ref[...]
    n_kb = x_ref.shape[0] // ATT_KEY_BLOCK
    ones = jnp.ones((SUM_ROWS, ATT_KEY_BLOCK), BF16)
    for h in range(DIFF_HEADS):
        k_ref[h] = _rope_chunk(proj[:, h * LANES:(h + 1) * LANES], c, sl, sh).astype(BF16)
        vt = proj[:, D_MODEL + h * DIFF_V_DIM:D_MODEL + (h + 1) * DIFF_V_DIM].T.astype(BF16)
        for kb in range(n_kb):
            vt_ref[h, kb, :DIFF_V_DIM, :] = vt[:, kb * ATT_KEY_BLOCK:(kb + 1) * ATT_KEY_BLOCK]
            vt_ref[h, kb, DIFF_V_DIM:, :] = ones


def _kv_proj(x, w, tables):
    b, s, d = x.shape
    nkb = s // ATT_KEY_BLOCK
    kb_tile = ROW_TILE // ATT_KEY_BLOCK
    tab_spec = pl.BlockSpec((ROW_TILE, LANES), lambda bi, i: (i, 0))
    return pl.pallas_call(
        _kv_proj_kernel,
        grid=(b, s // ROW_TILE),
        in_specs=[pl.BlockSpec((None, ROW_TILE, d), lambda bi, i: (bi, i, 0)), _resident(w.shape),
                  tab_spec, tab_spec, tab_spec],
        out_specs=[pl.BlockSpec((None, DIFF_HEADS, ROW_TILE, LANES), lambda bi, i: (bi, 0, i, 0)),
                   pl.BlockSpec((None, DIFF_HEADS, kb_tile, DIFF_V_DIM + SUM_ROWS, ATT_KEY_BLOCK),
                                lambda bi, i: (bi, 0, i, 0, 0))],
        out_shape=[jax.ShapeDtypeStruct((b, DIFF_HEADS, s, LANES), BF16),
                   jax.ShapeDtypeStruct((b, DIFF_HEADS, nkb, DIFF_V_DIM + SUM_ROWS, ATT_KEY_BLOCK), BF16)],
        compiler_params=_cparams(2),
        name="diff_kv_proj",
    )(x, w, *tables)


def _dq_proj_kernel(x_ref, w_ref, c_ref, sl_ref, sh_ref, q_ref):
    proj = jnp.dot(x_ref[...].astype(BF16), w_ref[...], preferred_element_type=F32)
    c, sl, sh = c_ref[...], sl_ref[...], sh_ref[...]
    for h in range(DIFF_HEADS):
        q = proj[:, h * LANES:(h + 1) * LANES] * (DIFF_HEAD_DIM ** -0.5 * LOG2E)
        q_ref[h] = _rope_chunk(q, c, sl, sh).astype(BF16)


def _dq_proj(x, w, tables):
    b, s, d = x.shape
    tab_spec = pl.BlockSpec((ROW_TILE, LANES), lambda bi, i: (i, 0))
    return pl.pallas_call(
        _dq_proj_kernel,
        grid=(b, s // ROW_TILE),
        in_specs=[pl.BlockSpec((None, ROW_TILE, d), lambda bi, i: (bi, i, 0)), _resident(w.shape),
                  tab_spec, tab_spec, tab_spec],
        out_specs=pl.BlockSpec((None, DIFF_HEADS, ROW_TILE, LANES), lambda bi, i: (bi, 0, i, 0)),
        out_shape=jax.ShapeDtypeStruct((b, DIFF_HEADS, s, LANES), BF16),
        compiler_params=_cparams(2),
        name="diff_q_proj",
    )(x, w, *tables)


def _diff_attn_kernel(q_ref, k_ref, vt_ref, lam_ref, gain_ref, o_ref, acc_ref, sa_ref, sb_ref, *, lambda_init):
    qi = pl.program_id(2)
    n_heads = q_ref.shape[0]
    tq = ATT_Q_TILE
    t0 = qi * tq
    kb_sz = ATT_KEY_BLOCK
    tok2 = _two(t0 + lax.broadcasted_iota(jnp.int32, (1, tq), 1))
    krow = lax.broadcasted_iota(jnp.int32, (kb_sz, 1), 0)
    rhs = [_split_heads_rhs(q_ref[h]) for h in range(n_heads)]
    last = (t0 + tq - 1) // kb_sz
    acc_ref[...] = jnp.zeros_like(acc_ref)

    def scores_into(kb, dst_ref):
        k0 = pl.multiple_of(kb * kb_sz, kb_sz)
        for h in range(n_heads):
            dst_ref[h] = _scores_t(k_ref[h, pl.ds(k0, kb_sz), :], rhs[h])

    def consume(kb, src_ref, st, maybe_diagonal):
        if maybe_diagonal:
            visible = kb * kb_sz + krow <= tok2
        out = []
        for h in range(n_heads):
            s = src_ref[h]
            if maybe_diagonal:
                s = jnp.where(visible, s, NEG)
            out.append(_flash_update(s, vt_ref[h, kb], st[h], acc_ref, h))
        return tuple(out)

    def two_blocks(i, st):
        kb = 2 * i
        scores_into(kb + 1, sb_ref)
        st = consume(kb, sa_ref, st, False)
        scores_into(kb + 2, sa_ref)
        return consume(kb + 1, sb_ref, st, False)

    def tail_odd(st):
        scores_into(last, sb_ref)
        st = consume(last - 1, sa_ref, st, False)
        return consume(last, sb_ref, st, True)

    init = tuple(jnp.full((1, 2 * tq), NEG, F32) for _ in range(n_heads))
    scores_into(0, sa_ref)
    st = lax.fori_loop(0, last // 2, two_blocks, init)
    st = lax.cond(last % 2 == 0, lambda st: consume(last, sa_ref, st, True), tail_odd, st)

    lp = lam_ref[...]
    lam = (jnp.exp(jnp.sum(lp[0:1] * lp[1:2], axis=1, keepdims=True))
           - jnp.exp(jnp.sum(lp[2:3] * lp[3:4], axis=1, keepdims=True)) + lambda_init)
    for h in range(n_heads):
        a = _normalized(acc_ref[h], DIFF_V_DIM)
        o = a[:, :tq] - lam * a[:, tq:]
        ms = jnp.mean(o * o, axis=0, keepdims=True)
        o = o * lax.rsqrt(ms + NORM_EPS) * gain_ref[...] * (1.0 - lambda_init)
        o_ref[h] = o.T.astype(BF16)


def _diff_attn(q, k, vt, lam_params, gain, lambda_init):
    b, h, s, _ = q.shape
    nkb = s // ATT_KEY_BLOCK
    tq = ATT_Q_TILE
    hps = DIFF_HEADS_PER_STEP
    per_head = lambda *shape: pl.BlockSpec((None, hps) + shape, lambda bi, hi, qi: (bi, hi) + (0,) * len(shape))
    q_spec = pl.BlockSpec((None, hps, tq, LANES), lambda bi, hi, qi: (bi, hi, qi, 0))
    return pl.pallas_call(
        functools.partial(_diff_attn_kernel, lambda_init=lambda_init),
        grid=(b, h // hps, s // tq),
        in_specs=[q_spec, per_head(s, LANES), per_head(nkb, DIFF_V_DIM + SUM_ROWS, ATT_KEY_BLOCK),
                  _resident(lam_params.shape), _resident(gain.shape)],
        out_specs=q_spec,
        out_shape=jax.ShapeDtypeStruct((b, h, s, LANES), BF16),
        scratch_shapes=[pltpu.VMEM((hps, DIFF_V_DIM + SUM_ROWS, 2 * tq), F32),
                        pltpu.VMEM((hps, ATT_KEY_BLOCK, 2 * tq), F32), pltpu.VMEM((hps, ATT_KEY_BLOCK, 2 * tq), F32)],
        compiler_params=_cparams(3),
        name="diff_attn",
    )(q, k, vt, lam_params, gain)


def kernel(x, ffn1_w_gate, ffn1_w_up, ffn1_w_down, ffn2_w_gate, ffn2_w_up, ffn2_w_down, ln_g, ln_b, a_w_in,
           a_cmp_pos_k, a_cmp_w1_k, a_cmp_b1_k, a_cmp_w2_k, a_cmp_pos_v, a_cmp_w1_v, a_cmp_b1_v, a_cmp_w2_v,
           a_w_out, b_w_kv_shared, b_w_q, b_lambda_q1, b_lambda_k1, b_lambda_q2, b_lambda_k2, b_subln_g, b_w_out):
    bsz, seq, d = x.shape
    assert d == D_MODEL and seq % ROW_TILE == 0 and seq >= WINDOW + 2 * ATT_KEY_BLOCK
    tables = _rope_tables(seq)
    bf = lambda w: w.astype(BF16)
    row = lambda v: v[None, :]

    def ffn(xv, wg, wu, wd, g, b):
        return _ffn_ln(xv.reshape(bsz * seq, d), bf(wg), bf(wu), bf(wd), row(g), row(b)).reshape(bsz, seq, d)

    k12 = vt = None
    for layer in range(DEPTH):
        x = ffn(x, ffn1_w_gate[layer], ffn1_w_up[layer], ffn1_w_down[layer], ln_g[layer, 0], ln_b[layer, 0])
        if layer < N_A_LAYERS:
            qc, qr, kc, vc, ksd, kwd, vst, vwt, gt = _nsa_proj(x, _nsa_proj_weight(a_w_in[layer]), tables)
            wk = _compress_weights(a_cmp_pos_k[layer], a_cmp_w1_k[layer], a_cmp_b1_k[layer], a_cmp_w2_k[layer], True)
            wv = _compress_weights(a_cmp_pos_v[layer], a_cmp_w1_v[layer], a_cmp_b1_v[layer], a_cmp_w2_v[layer], False)
            kcd, vct = _compress(kc, vc, wk, wv)
            o = _nsa_attn(qc, qr, kcd, vct, ksd, kwd, vst, vwt, gt)
            w_out = a_w_out[layer]
        else:
            j = layer - N_A_LAYERS
            lambda_init = 0.8 - 0.6 * float(np.exp(-0.3 * layer))
            q = _dq_proj(x, bf(_pair_maps_columns(b_w_q[j])), tables)
            lam_params = jnp.pad(jnp.stack([b_lambda_q1[j], b_lambda_k1[j], b_lambda_q2[j], b_lambda_k2[j]]),
                                 ((0, 4), (0, LANES - DIFF_HEAD_DIM)))
            gain = jnp.broadcast_to(b_subln_g[j][:, None], (DIFF_V_DIM, ATT_Q_TILE))
            o = _diff_attn(q, k12, vt, lam_params, gain, lambda_init)
            w_out = b_w_out[j]
        x = _out_ln(o, x, bf(w_out), row(ln_g[layer, 1]), row(ln_b[layer, 1]))
        x = ffn(x, ffn2_w_gate[layer], ffn2_w_up[layer], ffn2_w_down[layer], ln_g[layer, 2], ln_b[layer, 2])
        if layer == N_A_LAYERS - 1:
            w_kv = jnp.concatenate([_pair_maps_columns(b_w_kv_shared[:, :D_MODEL]), b_w_kv_shared[:, D_MODEL:]], axis=1)
            k12, vt = _kv_proj(x, bf(w_kv), tables)
    return x
```

```python
import functools

import numpy as np
import jax
import jax.numpy as jnp
from jax import lax
from jax.experimental import pallas as pl
from jax.experimental.pallas import tpu as pltpu

F32 = jnp.float32
BF16 = jnp.bfloat16

D_MODEL = 1024
DEPTH = 4
N_A_LAYERS = DEPTH // 2
NSA_HEADS = 16
NSA_HEAD_DIM = D_MODEL // NSA_HEADS
NSA_KV_GROUPS = 2
NSA_KV_WIDTH = NSA_KV_GROUPS * NSA_HEAD_DIM
NSA_GATE_WIDTH = 3 * NSA_HEADS
CMP_BLOCK = 32
CMP_STRIDE = 16
CMP_MLP_HIDDEN = 128
SLC_BLOCK = 64
SLC_TOPK = 16
WINDOW = 512
FORCE_BONUS = 1000.0
DIFF_HEADS = 8
DIFF_HEAD_DIM = D_MODEL // (2 * DIFF_HEADS)
DIFF_V_DIM = 2 * DIFF_HEAD_DIM
ROPE_THETA = 500000.0
ROPE_FRACTION = 4
D_FF = 2816
DEEPNORM_ALPHA = (2 * DEPTH) ** 0.25
NORM_EPS = 1e-5
NEG = -1e30

LANES = 128
ROW_TILE = 512
FFN_COL_CHUNK = 512
ATT_Q_TILE = 128
ATT_KEY_BLOCK = 256
DIFF_HEADS_PER_STEP = 8
SUM_ROWS = 16
LOG2E = 1.4426950408889634
VMEM_LIMIT_BYTES = 56 * 1024 * 1024
N_CHUNKS = D_MODEL // LANES


def _cparams(n_axes):
    return pltpu.CompilerParams(dimension_semantics=("parallel",) * n_axes,
                                vmem_limit_bytes=VMEM_LIMIT_BYTES)


def _resident(shape):
    zeros = (0,) * len(shape)
    return pl.BlockSpec(shape, lambda *_: zeros, pipeline_mode=pl.Buffered(1))


def _layer_norm_rows(r, g, b):
    mu = jnp.mean(r, axis=-1, keepdims=True)
    xc = r - mu
    var = jnp.mean(xc * xc, axis=-1, keepdims=True)
    return xc * lax.rsqrt(var + NORM_EPS) * g + b


def _ffn_ln_rows(x, wg_ref, wu_ref, wd_ref, g_ref, b_ref, h_ref):
    xb = x.astype(BF16)
    d_ff = wg_ref.shape[1]
    for c0 in range(0, d_ff, FFN_COL_CHUNK):
        c1 = min(c0 + FFN_COL_CHUNK, d_ff)
        gate = jnp.dot(xb, wg_ref[:, c0:c1], preferred_element_type=F32)
        up = jnp.dot(xb, wu_ref[:, c0:c1], preferred_element_type=F32)
        silu = gate * (1.0 / (1.0 + jnp.exp(-gate)))
        h_ref[:, c0:c1] = (silu * up).astype(BF16)
    y = jnp.dot(h_ref[...], wd_ref[...], preferred_element_type=F32)
    return _layer_norm_rows(DEEPNORM_ALPHA * x + 0.5 * y, g_ref[...], b_ref[...])


def _rope_tables(seq):
    r = NSA_HEAD_DIM // ROPE_FRACTION
    half = r // 2
    inv = ROPE_THETA ** (-jnp.arange(half, dtype=F32) * 2.0 / r)
    ang = jnp.arange(seq).astype(F32)[:, None] * inv[None, :]
    cos, sin = jnp.cos(ang), jnp.sin(ang)
    pad = jnp.zeros((seq, NSA_HEAD_DIM - r), F32)
    zero = jnp.zeros_like(sin)
    c_head = jnp.concatenate([cos, cos, pad + 1.0], axis=1)
    s_lo_head = jnp.concatenate([-sin, zero, pad], axis=1)
    s_hi_head = jnp.concatenate([zero, sin, pad], axis=1)
    rep = lambda t: jnp.concatenate([t, t], axis=1)
    return rep(c_head), rep(s_lo_head), rep(s_hi_head)


def _rope_chunk(x, c, s_lo, s_hi):
    half = NSA_HEAD_DIM // ROPE_FRACTION // 2
    return x * c + pltpu.roll(x, LANES - half, 1) * s_lo + pltpu.roll(x, half, 1) * s_hi


def _nsa_proj_kernel(x_ref, wg_ref, wu_ref, wd_ref, g_ref, b_ref, w_ref, c_ref, sl_ref, sh_ref,
                     xo_ref, qc_ref, qr_ref, kc_ref, vc_ref, ks_ref, kw_ref, vst_ref, vwt_ref, gt_ref, h_ref):
    x = _ffn_ln_rows(x_ref[...], wg_ref, wu_ref, wd_ref, g_ref, b_ref, h_ref)
    xo_ref[...] = x
    proj = jnp.dot(x.astype(BF16), w_ref[...], preferred_element_type=F32)
    c, sl, sh = c_ref[...], sl_ref[...], sh_ref[...]
    col = lambda j: proj[:, j * LANES:(j + 1) * LANES]
    for j in range(N_CHUNKS):
        q = col(j) * (NSA_HEAD_DIM ** -0.5 * LOG2E)
        qc_ref[j] = q.astype(BF16)
        qr_ref[j] = _rope_chunk(q, c, sl, sh).astype(BF16)
    base = N_CHUNKS
    kc_ref[...] = col(base)
    vc_ref[...] = col(base + 1)
    for g in range(NSA_KV_GROUPS):
        ks_ref[g] = _rope_chunk(col(base + 2 + g), c, sl, sh).astype(BF16)
        kw_ref[g] = _rope_chunk(col(base + 4 + g), c, sl, sh).astype(BF16)
    n_kb = x_ref.shape[0] // ATT_KEY_BLOCK
    ones = jnp.ones((SUM_ROWS, ATT_KEY_BLOCK), BF16)
    for src, dst in ((base + 6, vst_ref), (base + 7, vwt_ref)):
        vt = col(src).T.astype(BF16)
        for g in range(NSA_KV_GROUPS):
            for kb in range(n_kb):
                dst[g, kb, :NSA_HEAD_DIM, :] = vt[g * NSA_HEAD_DIM:(g + 1) * NSA_HEAD_DIM,
                                                  kb * ATT_KEY_BLOCK:(kb + 1) * ATT_KEY_BLOCK]
                dst[g, kb, NSA_HEAD_DIM:, :] = ones
    gates = col(base + 8)
    gt_ref[...] = (1.0 / (1.0 + jnp.exp(-gates))).T


def _nsa_proj_weight(w_in):
    d = D_MODEL
    kvw = NSA_KV_WIDTH
    q, kc, vc, ks, vs, kw, vw, gates = jnp.split(
        w_in, [int(c) for c in np.cumsum([d] + [kvw] * 6)], axis=1)
    dh = NSA_HEAD_DIM
    dup = lambda t: jnp.concatenate([t[:, :dh], t[:, :dh], t[:, dh:], t[:, dh:]], axis=1)
    gates = jnp.pad(gates, ((0, 0), (0, LANES - NSA_GATE_WIDTH)))
    return jnp.concatenate([q, kc, vc, dup(ks), dup(kw), vs, vw, gates], axis=1).astype(BF16)


def _nsa_proj(x, ffn, w, tables):
    b, s, d = x.shape
    d_ff = ffn[0].shape[1]
    ncol = w.shape[1]
    nkb = s // ATT_KEY_BLOCK
    kb_tile = ROW_TILE // ATT_KEY_BLOCK
    dh = NSA_HEAD_DIM + SUM_ROWS
    tab_spec = pl.BlockSpec((ROW_TILE, LANES), lambda bi, i: (i, 0))
    chunk_spec = lambda n: pl.BlockSpec((None, n, ROW_TILE, LANES), lambda bi, i: (bi, 0, i, 0))
    row_spec = pl.BlockSpec((None, ROW_TILE, LANES), lambda bi, i: (bi, i, 0))
    vt_spec = pl.BlockSpec((None, NSA_KV_GROUPS, kb_tile, dh, ATT_KEY_BLOCK), lambda bi, i: (bi, 0, i, 0, 0))
    x_spec = pl.BlockSpec((None, ROW_TILE, d), lambda bi, i: (bi, i, 0))
    return pl.pallas_call(
        _nsa_proj_kernel,
        grid=(b, s // ROW_TILE),
        in_specs=[x_spec, _resident((d, d_ff)), _resident((d, d_ff)), _resident((d_ff, d)),
                  _resident((1, d)), _resident((1, d)),
                  _resident((d, ncol)), tab_spec, tab_spec, tab_spec],
        out_specs=[x_spec, chunk_spec(N_CHUNKS), chunk_spec(N_CHUNKS), row_spec, row_spec,
                   chunk_spec(NSA_KV_GROUPS), chunk_spec(NSA_KV_GROUPS), vt_spec, vt_spec,
                   pl.BlockSpec((None, LANES, ROW_TILE), lambda bi, i: (bi, 0, i))],
        out_shape=[jax.ShapeDtypeStruct((b, s, d), F32),
                   jax.ShapeDtypeStruct((b, N_CHUNKS, s, LANES), BF16),
                   jax.ShapeDtypeStruct((b, N_CHUNKS, s, LANES), BF16),
                   jax.ShapeDtypeStruct((b, s, LANES), F32),
                   jax.ShapeDtypeStruct((b, s, LANES), F32),
                   jax.ShapeDtypeStruct((b, NSA_KV_GROUPS, s, LANES), BF16),
                   jax.ShapeDtypeStruct((b, NSA_KV_GROUPS, s, LANES), BF16),
                   jax.ShapeDtypeStruct((b, NSA_KV_GROUPS, nkb, dh, ATT_KEY_BLOCK), BF16),
                   jax.ShapeDtypeStruct((b, NSA_KV_GROUPS, nkb, dh, ATT_KEY_BLOCK), BF16),
                   jax.ShapeDtypeStruct((b, LANES, s), F32)],
        scratch_shapes=[pltpu.VMEM((ROW_TILE, d_ff), BF16)],
        compiler_params=_cparams(2),
        name="ffn_nsa_proj",
    )(x, *ffn, w, *tables)


def _gelu_tanh(x):
    return 0.5 * x * (1.0 + jnp.tanh(np.sqrt(2.0 / np.pi).astype(np.float32) * (x + 0.044715 * (x * x * x))))


def _compress_kernel(kch_ref, vch_ref, pk_ref, pv_ref, w1k_ref, w1v_ref, b1k_ref, b1v_ref, w2k_ref, w2v_ref,
                     kcd_ref, vct_ref):
    n_rows = kch_ref.shape[0]

    def mlp(ch_ref, pos_ref, w1_ref, b1_ref, w2_ref):
        ch = ch_ref[...]
        lo = jnp.dot((ch + pos_ref[0:1, :]).astype(BF16), w1_ref[0], preferred_element_type=F32)
        hi = jnp.dot((ch + pos_ref[1:2, :]).astype(BF16), w1_ref[1], preferred_element_type=F32)
        hid = lo + pltpu.roll(hi, n_rows - 1, 0) + b1_ref[...]
        return jnp.dot(_gelu_tanh(hid).astype(BF16), w2_ref[...], preferred_element_type=F32)

    kcd = mlp(kch_ref, pk_ref, w1k_ref, b1k_ref, w2k_ref)
    for g in range(NSA_KV_GROUPS):
        kcd_ref[g] = kcd[:, g * LANES:(g + 1) * LANES].astype(BF16)
    vct = mlp(vch_ref, pv_ref, w1v_ref, b1v_ref, w2v_ref).T
    for g in range(NSA_KV_GROUPS):
        vct_ref[g] = vct[g * NSA_HEAD_DIM:(g + 1) * NSA_HEAD_DIM, :].astype(BF16)


def _compress_weights(pos, w1, b1, w2, dup_out):
    g_n, dh, hid = NSA_KV_GROUPS, NSA_HEAD_DIM, CMP_MLP_HIDDEN
    w1r = w1.reshape(2, CMP_STRIDE, dh, hid)
    eye = jnp.eye(g_n, dtype=w1.dtype)
    w1x = jnp.einsum('hldm,gk->hlgdkm', w1r, eye).reshape(2, CMP_STRIDE * g_n * dh, g_n * hid)
    posx = jnp.broadcast_to(pos.reshape(2, CMP_STRIDE, 1, dh), (2, CMP_STRIDE, g_n, dh)).reshape(2, -1)
    b1x = jnp.tile(b1, g_n)[None, :]
    reps = 2 if dup_out else 1
    w2x = jnp.einsum('md,gk->gmkd', w2, eye)
    w2x = jnp.broadcast_to(w2x[:, :, :, None, :], (g_n, hid, g_n, reps, dh)).reshape(g_n * hid, g_n * reps * dh)
    return posx, w1x.astype(BF16), b1x, w2x.astype(BF16)


def _compress(kc, vc, wk, wv):
    b, s, _ = kc.shape
    n_rows = s // CMP_STRIDE
    width = CMP_STRIDE * LANES
    kch = kc.reshape(b, n_rows, width)
    vch = vc.reshape(b, n_rows, width)
    pk, w1k, b1k, w2k = wk
    pv, w1v, b1v, w2v = wv
    ch_spec = pl.BlockSpec((None, n_rows, width), lambda bi: (bi, 0, 0))
    return pl.pallas_call(
        _compress_kernel,
        grid=(b,),
        in_specs=[ch_spec, ch_spec, _resident(pk.shape), _resident(pv.shape), _resident(w1k.shape),
                  _resident(w1v.shape), _resident(b1k.shape), _resident(b1v.shape), _resident(w2k.shape),
                  _resident(w2v.shape)],
        out_specs=[pl.BlockSpec((None, NSA_KV_GROUPS, n_rows, LANES), lambda bi: (bi, 0, 0, 0)),
                   pl.BlockSpec((None, NSA_KV_GROUPS, NSA_HEAD_DIM, n_rows), lambda bi: (bi, 0, 0, 0))],
        out_shape=[jax.ShapeDtypeStruct((b, NSA_KV_GROUPS, n_rows, LANES), BF16),
                   jax.ShapeDtypeStruct((b, NSA_KV_GROUPS, NSA_HEAD_DIM, n_rows), BF16)],
        compiler_params=_cparams(1),
        name="nsa_compress",
    )(kch, vch, pk, pv, w1k, w1v, b1k, b1v, w2k, w2v)


def _split_heads_rhs(q):
    lane = lax.broadcasted_iota(jnp.int32, q.shape, 1)
    zero = jnp.zeros_like(q)
    return jnp.concatenate([jnp.where(lane < LANES // 2, q, zero), jnp.where(lane >= LANES // 2, q, zero)], axis=0)


def _scores_t(k, rhs):
    return lax.dot_general(k, rhs, (((1,), (1,)), ((), ())), preferred_element_type=F32)


def _two(row):
    return jnp.concatenate([row, row], axis=1)


def _flash_update(s, vt_blk, m, acc_ref, idx):
    m_new = jnp.maximum(m, jnp.max(s, axis=0, keepdims=True))
    alpha = jnp.exp2(m - m_new)
    p = jnp.exp2(s - m_new).astype(BF16)
    acc_ref[idx] = alpha * acc_ref[idx] + jnp.dot(vt_blk, p, preferred_element_type=F32)
    return m_new


def _flash_update_block_bias(src_ref, idx, bias_rows, vt_blk, m, acc_ref):
    rows = src_ref.shape[1] // len(bias_rows)
    piece = lambda j: src_ref[idx, j * rows:(j + 1) * rows, :]
    m_new = m
    for j, b in enumerate(bias_rows):
        m_new = jnp.maximum(m_new, jnp.max(piece(j), axis=0, keepdims=True) + b)
    alpha = jnp.exp2(m - m_new)
    p = jnp.concatenate([jnp.exp2(piece(j) + (b - m_new)).astype(BF16) for j, b in enumerate(bias_rows)], axis=0)
    acc_ref[idx] = alpha * acc_ref[idx] + jnp.dot(vt_blk, p, preferred_element_type=F32)
    return m_new


def _normalized(acc, dv):
    return acc[:dv] * (1.0 / acc[dv:dv + 1])


def _nsa_attn_kernel(qc_ref, qr_ref, kcd_ref, vct_ref, ks_ref, kw_ref, vst_ref, vwt_ref, gt_ref, ov_ref,
                     o_ref, og_ref, acc_ref, selb_ref, sa_ref, sb_ref):
    qi = pl.program_id(1)
    tq = ATT_Q_TILE
    t0 = qi * tq
    n_pairs = qc_ref.shape[0]
    n_groups = kcd_ref.shape[0]
    group_of = lambda c: c // (n_pairs // n_groups)
    n_cmp = kcd_ref.shape[1]
    seq = ks_ref.shape[1]
    n_sel = seq // SLC_BLOCK
    tok1 = t0 + lax.broadcasted_iota(jnp.int32, (1, tq), 1)
    tok2 = _two(tok1)

    kb_sz = ATT_KEY_BLOCK

    def gate(j, c):
        return jnp.concatenate([gt_ref[j * NSA_HEADS + 2 * c:j * NSA_HEADS + 2 * c + 1, :],
                                gt_ref[j * NSA_HEADS + 2 * c + 1:j * NSA_HEADS + 2 * c + 2, :]], axis=1)

    rhs = [_split_heads_rhs(qr_ref[c]) for c in range(n_pairs)]

    def sel_scores(kb, dst_ref):
        k0 = pl.multiple_of(kb * kb_sz, kb_sz)
        for c in range(n_pairs):
            dst_ref[c] = _scores_t(ks_ref[group_of(c), pl.ds(k0, kb_sz), :], rhs[c])

    win_blocks = WINDOW // kb_sz + 1
    win_start = jnp.minimum(jnp.maximum(t0 - WINDOW, 0) // kb_sz, seq // kb_sz - win_blocks)
    w0 = pl.multiple_of(win_start * kb_sz, kb_sz)

    cmp_scores = [_scores_t(kcd_ref[group_of(c)], _split_heads_rhs(qc_ref[c])) for c in range(n_pairs)]
    win_scores = [_scores_t(kw_ref[group_of(c), pl.ds(w0, win_blocks * kb_sz), :], rhs[c]) for c in range(n_pairs)]

    cmp_end = lax.broadcasted_iota(jnp.int32, (n_cmp, 1), 0) * CMP_STRIDE + (CMP_BLOCK - 1)
    vis = cmp_end <= tok2
    any_vis = (tok2 >= CMP_BLOCK - 1).astype(F32)
    psum = [jnp.zeros((n_cmp, tq), F32) for _ in range(n_groups)]
    for c in range(n_pairs):
        s = jnp.where(vis, cmp_scores[c], NEG)
        m = jnp.max(s, axis=0, keepdims=True)
        e = jnp.exp2(s - m)
        p = e * (any_vis / jnp.sum(e, axis=0, keepdims=True))
        psum[group_of(c)] += p[:, :tq] + p[:, tq:]
        og_ref[c] = gate(0, c) * jnp.dot(vct_ref[group_of(c)], p.astype(BF16), preferred_element_type=F32)

    imp_raw = [jnp.dot(ov_ref[...], ps, precision=lax.Precision.HIGHEST, preferred_element_type=F32)
               for ps in psum]
    sel_scores(0, sa_ref)

    wrow = w0 + lax.broadcasted_iota(jnp.int32, (win_blocks * kb_sz, 1), 0)
    in_window = (wrow <= tok2) & (wrow > tok2 - WINDOW)
    for c in range(n_pairs):
        s = jnp.where(in_window, win_scores[c], NEG)
        m = jnp.max(s, axis=0, keepdims=True)
        p = jnp.exp2(s - m).astype(BF16)
        win = jnp.zeros((NSA_HEAD_DIM + SUM_ROWS, 2 * tq), F32)
        for j in range(win_blocks):
            win += jnp.dot(vwt_ref[group_of(c), win_start + j], p[j * kb_sz:(j + 1) * kb_sz, :],
                           preferred_element_type=F32)
        og_ref[c] += gate(2, c) * _normalized(win, NSA_HEAD_DIM)

    blk = lax.broadcasted_iota(jnp.int32, (n_sel, 1), 0)
    cur = tok1 // SLC_BLOCK
    forced = (blk == 0) | (blk == cur) | (blk == cur - 1)
    valid = blk <= cur
    rows8 = 8
    sub = lax.broadcasted_iota(jnp.int32, (rows8, 1), 0)
    for gi in range(n_groups):
        imp = jnp.where(valid, imp_raw[gi] + jnp.where(forced, FORCE_BONUS, 0.0), NEG)
        grp = [imp[rows8 * j:rows8 * (j + 1), :] for j in range(n_sel // rows8)]
        rank = [jnp.zeros((rows8, tq), jnp.int32) for _ in grp]
        for m in range(n_sel):
            row = imp[m:m + 1, :]
            for j in range(len(grp)):
                ge = lambda: (row >= grp[j]).astype(jnp.int32)
                gt = lambda: (row > grp[j]).astype(jnp.int32)
                if rows8 * j > m:
                    beats = ge()
                elif rows8 * j + rows8 - 1 <= m:
                    beats = gt()
                else:
                    beats = jnp.where(sub + rows8 * j > m, ge(), gt())
                rank[j] = rank[j] + beats
        rank = jnp.concatenate(rank, axis=0)
        selb_ref[gi] = jnp.where((rank < SLC_TOPK) & valid, 0.0, NEG)

    last = (t0 + tq - 1) // kb_sz
    blocks_per_kb = kb_sz // SLC_BLOCK
    krow = lax.broadcasted_iota(jnp.int32, (kb_sz, 1), 0)
    acc_ref[...] = jnp.zeros_like(acc_ref)

    def sel_bias_rows(kb):
        return [[_two(selb_ref[gi, pl.ds(kb * blocks_per_kb + j, 1), :]) for j in range(blocks_per_kb)]
                for gi in range(n_groups)]

    def sel_below_diagonal(kb, src_ref, st):
        rows = sel_bias_rows(kb)
        return tuple(_flash_update_block_bias(src_ref, c, rows[group_of(c)], vst_ref[group_of(c), kb], st[c], acc_ref)
                     for c in range(n_pairs))

    def sel_diagonal(kb, src_ref, st):
        causal = kb * kb_sz + krow <= tok2
        bias = [jnp.where(causal, jnp.concatenate([jnp.broadcast_to(r, (SLC_BLOCK, 2 * tq)) for r in rows], axis=0),
                          NEG) for rows in sel_bias_rows(kb)]
        return tuple(_flash_update(src_ref[c] + bias[group_of(c)], vst_ref[group_of(c), kb], st[c], acc_ref, c)
                     for c in range(n_pairs))

    def sel_two_blocks(i, st):
        kb = 2 * i
        sel_scores(kb + 1, sb_ref)
        st = sel_below_diagonal(kb, sa_ref, st)
        sel_scores(kb + 2, sa_ref)
        return sel_below_diagonal(kb + 1, sb_ref, st)

    def sel_tail_odd(st):
        sel_scores(last, sb_ref)
        st = sel_below_diagonal(last - 1, sa_ref, st)
        return sel_diagonal(last, sb_ref, st)

    init = tuple(jnp.full((1, 2 * tq), NEG, F32) for _ in range(n_pairs))
    st = lax.fori_loop(0, last // 2, sel_two_blocks, init)
    st = lax.cond(last % 2 == 0, lambda st: sel_diagonal(last, sa_ref, st), sel_tail_odd, st)

    for c in range(n_pairs):
        o_t = og_ref[c] + gate(1, c) * _normalized(acc_ref[c], NSA_HEAD_DIM)
        o_ref[c] = jnp.concatenate([o_t[:, :tq], o_t[:, tq:]], axis=0).T.astype(BF16)


def _cmp_to_slc_t(n_cmp_rows, n_sel):
    c0 = np.arange(n_cmp_rows)[None, :] * CMP_STRIDE
    s0 = np.arange(n_sel)[:, None] * SLC_BLOCK
    ov = np.minimum(c0 + CMP_BLOCK, s0 + SLC_BLOCK) - np.maximum(c0, s0)
    return (np.clip(ov, 0, None) / CMP_BLOCK).astype(np.float32)


def _nsa_attn(qc, qr, kcd, vct, ksd, kwd, vst, vwt, gt):
    b, pairs, s, _ = qc.shape
    groups = NSA_KV_GROUPS
    n_cmp = kcd.shape[2]
    nkb = s // ATT_KEY_BLOCK
    dh = NSA_HEAD_DIM
    tq = ATT_Q_TILE
    n_sel = s // SLC_BLOCK
    ov_t = jnp.asarray(_cmp_to_slc_t(n_cmp, n_sel))
    q_spec = pl.BlockSpec((None, pairs, tq, LANES), lambda bi, qi: (bi, 0, qi, 0))
    per_batch = lambda *shape: pl.BlockSpec((None,) + shape, lambda bi, qi: (bi,) + (0,) * len(shape))
    return pl.pallas_call(
        _nsa_attn_kernel,
        grid=(b, s // tq),
        in_specs=[q_spec, q_spec, per_batch(groups, n_cmp, LANES), per_batch(groups, dh, n_cmp),
                  per_batch(groups, s, LANES), per_batch(groups, s, LANES),
                  per_batch(groups, nkb, dh + SUM_ROWS, ATT_KEY_BLOCK),
                  per_batch(groups, nkb, dh + SUM_ROWS, ATT_KEY_BLOCK),
                  pl.BlockSpec((None, LANES, tq), lambda bi, qi: (bi, 0, qi)),
                  _resident(ov_t.shape)],
        out_specs=q_spec,
        out_shape=jax.ShapeDtypeStruct((b, pairs, s, LANES), BF16),
        scratch_shapes=[pltpu.VMEM((pairs, dh, 2 * tq), F32), pltpu.VMEM((pairs, dh + SUM_ROWS, 2 * tq), F32),
                        pltpu.VMEM((groups, n_sel, tq), F32),
                        pltpu.VMEM((pairs, ATT_KEY_BLOCK, 2 * tq), F32), pltpu.VMEM((pairs, ATT_KEY_BLOCK, 2 * tq), F32)],
        compiler_params=_cparams(2),
        name="nsa_attn",
    )(qc, qr, kcd, vct, ksd, kwd, vst, vwt, gt, ov_t)


def _out_ffn_kernel(o_ref, x_ref, wo_ref, g1_ref, b1_ref, wg_ref, wu_ref, wd_ref, g2_ref, b2_ref, y_ref, h_ref):
    o = jnp.concatenate([o_ref[j] for j in range(o_ref.shape[0])], axis=1)
    mix = jnp.dot(o, wo_ref[...], preferred_element_type=F32)
    x1 = _layer_norm_rows(DEEPNORM_ALPHA * x_ref[...] + mix, g1_ref[...], b1_ref[...])
    y_ref[...] = _ffn_ln_rows(x1, wg_ref, wu_ref, wd_ref, g2_ref, b2_ref, h_ref)


def _out_ffn(o, x, wo, g1, b1, wg, wu, wd, g2, b2):
    bsz, n_chunks, s, _ = o.shape
    d = x.shape[-1]
    d_ff = wg.shape[1]
    x_spec = pl.BlockSpec((None, ROW_TILE, d), lambda bi, i: (bi, i, 0))
    return pl.pallas_call(
        _out_ffn_kernel,
        grid=(bsz, s // ROW_TILE),
        in_specs=[pl.BlockSpec((None, n_chunks, ROW_TILE, LANES), lambda bi, i: (bi, 0, i, 0)), x_spec,
                  _resident((d, d)), _resident((1, d)), _resident((1, d)),
                  _resident((d, d_ff)), _resident((d, d_ff)), _resident((d_ff, d)),
                  _resident((1, d)), _resident((1, d))],
        out_specs=x_spec,
        out_shape=jax.ShapeDtypeStruct(x.shape, F32),
        scratch_shapes=[pltpu.VMEM((ROW_TILE, d_ff), BF16)],
        compiler_params=_cparams(2),
        name="mixer_out_ffn",
    )(o, x, wo, g1, b1, wg, wu, wd, g2, b2)


def _pair_maps_columns(w):
    d = w.shape[0]
    return w.reshape(d, 2, DIFF_HEADS, DIFF_HEAD_DIM).transpose(0, 2, 1, 3).reshape(d, -1)


def _kv_proj_kernel(x_ref, w_ref, c_ref, sl_ref, sh_ref, k_ref, vt_ref):
    proj = jnp.dot(x_ref[...].astype(BF16), w_ref[...], preferred_element_type=F32)
    c, sl, sh = c_ref[...], sl_ref[...], sh_ref[...]
    n_kb = x_ref.shape[0] // ATT_KEY_BLOCK
    ones = jnp.ones((SUM_ROWS, ATT_KEY_BLOCK), BF16)
    for h in range(DIFF_HEADS):
        k_ref[h] = _rope_chunk(proj[:, h * LANES:(h + 1) * LANES], c, sl, sh).astype(BF16)
        vt = proj[:, D_MODEL + h * DIFF_V_DIM:D_MODEL + (h + 1) * DIFF_V_DIM].T.astype(BF16)
        for kb in range(n_kb):
            vt_ref[h, kb, :DIFF_V_DIM, :] = vt[:, kb * ATT_KEY_BLOCK:(kb + 1) * ATT_KEY_BLOCK]
            vt_ref[h, kb, DIFF_V_DIM:, :] = ones


def _kv_proj(x, w, tables):
    b, s, d = x.shape
    nkb = s // ATT_KEY_BLOCK
    kb_tile = ROW_TILE // ATT_KEY_BLOCK
    tab_spec = pl.BlockSpec((ROW_TILE, LANES), lambda bi, i: (i, 0))
    return pl.pallas_call(
        _kv_proj_kernel,
        grid=(b, s // ROW_TILE),
        in_specs=[pl.BlockSpec((None, ROW_TILE, d), lambda bi, i: (bi, i, 0)), _resident(w.shape),
                  tab_spec, tab_spec, tab_spec],
        out_specs=[pl.BlockSpec((None, DIFF_HEADS, ROW_TILE, LANES), lambda bi, i: (bi, 0, i, 0)),
                   pl.BlockSpec((None, DIFF_HEADS, kb_tile, DIFF_V_DIM + SUM_ROWS, ATT_KEY_BLOCK),
                                lambda bi, i: (bi, 0, i, 0, 0))],
        out_shape=[jax.ShapeDtypeStruct((b, DIFF_HEADS, s, LANES), BF16),
                   jax.ShapeDtypeStruct((b, DIFF_HEADS, nkb, DIFF_V_DIM + SUM_ROWS, ATT_KEY_BLOCK), BF16)],
        compiler_params=_cparams(2),
        name="diff_kv_proj",
    )(x, w, *tables)


def _dq_proj_kernel(x_ref, wg_ref, wu_ref, wd_ref, g_ref, b_ref, w_ref, c_ref, sl_ref, sh_ref,
                    xo_ref, q_ref, h_ref):
    x = _ffn_ln_rows(x_ref[...], wg_ref, wu_ref, wd_ref, g_ref, b_ref, h_ref)
    xo_ref[...] = x
    proj = jnp.dot(x.astype(BF16), w_ref[...], preferred_element_type=F32)
    c, sl, sh = c_ref[...], sl_ref[...], sh_ref[...]
    for h in range(DIFF_HEADS):
        q = proj[:, h * LANES:(h + 1) * LANES] * (DIFF_HEAD_DIM ** -0.5 * LOG2E)
        q_ref[h] = _rope_chunk(q, c, sl, sh).astype(BF16)


def _dq_proj(x, ffn, w, tables):
    b, s, d = x.shape
    d_ff = ffn[0].shape[1]
    tab_spec = pl.BlockSpec((ROW_TILE, LANES), lambda bi, i: (i, 0))
    x_spec = pl.BlockSpec((None, ROW_TILE, d), lambda bi, i: (bi, i, 0))
    return pl.pallas_call(
        _dq_proj_kernel,
        grid=(b, s // ROW_TILE),
        in_specs=[x_spec, _resident((d, d_ff)), _resident((d, d_ff)), _resident((d_ff, d)),
                  _resident((1, d)), _resident((1, d)), _resident(w.shape), tab_spec, tab_spec, tab_spec],
        out_specs=[x_spec, pl.BlockSpec((None, DIFF_HEADS, ROW_TILE, LANES), lambda bi, i: (bi, 0, i, 0))],
        out_shape=[jax.ShapeDtypeStruct((b, s, d), F32), jax.ShapeDtypeStruct((b, DIFF_HEADS, s, LANES), BF16)],
        scratch_shapes=[pltpu.VMEM((ROW_TILE, d_ff), BF16)],
        compiler_params=_cparams(2),
        name="ffn_diff_q_proj",
    )(x, *ffn, w, *tables)


def _diff_attn_kernel(q_ref, k_ref, vt_ref, lam_ref, gain_ref, o_ref, acc_ref, sa_ref, sb_ref, *, lambda_init):
    qi = pl.program_id(2)
    n_heads = q_ref.shape[0]
    tq = ATT_Q_TILE
    t0 = qi * tq
    kb_sz = ATT_KEY_BLOCK
    tok2 = _two(t0 + lax.broadcasted_iota(jnp.int32, (1, tq), 1))
    krow = lax.broadcasted_iota(jnp.int32, (kb_sz, 1), 0)
    rhs = [_split_heads_rhs(q_ref[h]) for h in range(n_heads)]
    last = (t0 + tq - 1) // kb_sz
    acc_ref[...] = jnp.zeros_like(acc_ref)

    def scores_into(kb, dst_ref):
        k0 = pl.multiple_of(kb * kb_sz, kb_sz)
        for h in range(n_heads):
            dst_ref[h] = _scores_t(k_ref[h, pl.ds(k0, kb_sz), :], rhs[h])

    def consume(kb, src_ref, st, maybe_diagonal):
        if maybe_diagonal:
            visible = kb * kb_sz + krow <= tok2
        out = []
        for h in range(n_heads):
            s = src_ref[h]
            if maybe_diagonal:
                s = jnp.where(visible, s, NEG)
            out.append(_flash_update(s, vt_ref[h, kb], st[h], acc_ref, h))
        return tuple(out)

    def two_blocks(i, st):
        kb = 2 * i
        scores_into(kb + 1, sb_ref)
        st = consume(kb, sa_ref, st, False)
        scores_into(kb + 2, sa_ref)
        return consume(kb + 1, sb_ref, st, False)

    def tail_odd(st):
        scores_into(last, sb_ref)
        st = consume(last - 1, sa_ref, st, False)
        return consume(last, sb_ref, st, True)

    init = tuple(jnp.full((1, 2 * tq), NEG, F32) for _ in range(n_heads))
    scores_into(0, sa_ref)
    st = lax.fori_loop(0, last // 2, two_blocks, init)
    st = lax.cond(last % 2 == 0, lambda st: consume(last, sa_ref, st, True), tail_odd, st)

    lp = lam_ref[...]
    lam = (jnp.exp(jnp.sum(lp[0:1] * lp[1:2], axis=1, keepdims=True))
           - jnp.exp(jnp.sum(lp[2:3] * lp[3:4], axis=1, keepdims=True)) + lambda_init)
    for h in range(n_heads):
        a = _normalized(acc_ref[h], DIFF_V_DIM)
        o = a[:, :tq] - lam * a[:, tq:]
        ms = jnp.mean(o * o, axis=0, keepdims=True)
        o = o * lax.rsqrt(ms + NORM_EPS) * gain_ref[...] * (1.0 - lambda_init)
        o_ref[h] = o.T.astype(BF16)


def _diff_attn(q, k, vt, lam_params, gain, lambda_init):
    b, h, s, _ = q.shape
    nkb = s // ATT_KEY_BLOCK
    tq = ATT_Q_TILE
    hps = DIFF_HEADS_PER_STEP
    per_head = lambda *shape: pl.BlockSpec((None, hps) + shape, lambda bi, hi, qi: (bi, hi) + (0,) * len(shape))
    q_spec = pl.BlockSpec((None, hps, tq, LANES), lambda bi, hi, qi: (bi, hi, qi, 0))
    return pl.pallas_call(
        functools.partial(_diff_attn_kernel, lambda_init=lambda_init),
        grid=(b, h // hps, s // tq),
        in_specs=[q_spec, per_head(s, LANES), per_head(nkb, DIFF_V_DIM + SUM_ROWS, ATT_KEY_BLOCK),
                  _resident(lam_params.shape), _resident(gain.shape)],
        out_specs=q_spec,
        out_shape=jax.ShapeDtypeStruct((b, h, s, LANES), BF16),
        scratch_shapes=[pltpu.VMEM((hps, DIFF_V_DIM + SUM_ROWS, 2 * tq), F32),
                        pltpu.VMEM((hps, ATT_KEY_BLOCK, 2 * tq), F32), pltpu.VMEM((hps, ATT_KEY_BLOCK, 2 * tq), F32)],
        compiler_params=_cparams(3),
        name="diff_attn",
    )(q, k, vt, lam_params, gain)


def kernel(x, ffn1_w_gate, ffn1_w_up, ffn1_w_down, ffn2_w_gate, ffn2_w_up, ffn2_w_down, ln_g, ln_b, a_w_in,
           a_cmp_pos_k, a_cmp_w1_k, a_cmp_b1_k, a_cmp_w2_k, a_cmp_pos_v, a_cmp_w1_v, a_cmp_b1_v, a_cmp_w2_v,
           a_w_out, b_w_kv_shared, b_w_q, b_lambda_q1, b_lambda_k1, b_lambda_q2, b_lambda_k2, b_subln_g, b_w_out):
    bsz, seq, d = x.shape
    assert d == D_MODEL and seq % ROW_TILE == 0 and seq >= WINDOW + 2 * ATT_KEY_BLOCK
    tables = _rope_tables(seq)
    bf = lambda w: w.astype(BF16)
    row = lambda v: v[None, :]

    k12 = vt = None
    for layer in range(DEPTH):
        ffn1 = (bf(ffn1_w_gate[layer]), bf(ffn1_w_up[layer]), bf(ffn1_w_down[layer]),
                row(ln_g[layer, 0]), row(ln_b[layer, 0]))
        if layer < N_A_LAYERS:
            x, qc, qr, kc, vc, ksd, kwd, vst, vwt, gt = _nsa_proj(x, ffn1, _nsa_proj_weight(a_w_in[layer]), tables)
            wk = _compress_weights(a_cmp_pos_k[layer], a_cmp_w1_k[layer], a_cmp_b1_k[layer], a_cmp_w2_k[layer], True)
            wv = _compress_weights(a_cmp_pos_v[layer], a_cmp_w1_v[layer], a_cmp_b1_v[layer], a_cmp_w2_v[layer], False)
            kcd, vct = _compress(kc, vc, wk, wv)
            o = _nsa_attn(qc, qr, kcd, vct, ksd, kwd, vst, vwt, gt)
            w_out = a_w_out[layer]
        else:
            j = layer - N_A_LAYERS
            lambda_init = 0.8 - 0.6 * float(np.exp(-0.3 * layer))
            x, q = _dq_proj(x, ffn1, bf(_pair_maps_columns(b_w_q[j])), tables)
            lam_params = jnp.pad(jnp.stack([b_lambda_q1[j], b_lambda_k1[j], b_lambda_q2[j], b_lambda_k2[j]]),
                                 ((0, 4), (0, LANES - DIFF_HEAD_DIM)))
            gain = jnp.broadcast_to(b_subln_g[j][:, None], (DIFF_V_DIM, ATT_Q_TILE))
            o = _diff_attn(q, k12, vt, lam_params, gain, lambda_init)
            w_out = b_w_out[j]
        x = _out_ffn(o, x, bf(w_out), row(ln_g[layer, 1]), row(ln_b[layer, 1]),
                     bf(ffn2_w_gate[layer]), bf(ffn2_w_up[layer]), bf(ffn2_w_down[layer]),
                     row(ln_g[layer, 2]), row(ln_b[layer, 2]))
        if layer == N_A_LAYERS - 1:
            w_kv = jnp.concatenate([_pair_maps_columns(b_w_kv_shared[:, :D_MODEL]), b_w_kv_shared[:, D_MODEL:]], axis=1)
            k12, vt = _kv_proj(x, bf(w_kv), tables)
    return x
```

```python
import functools

import numpy as np
import jax
import jax.numpy as jnp
from jax import lax
from jax.experimental import pallas as pl
from jax.experimental.pallas import tpu as pltpu

F32 = jnp.float32
BF16 = jnp.bfloat16

D_MODEL = 1024
DEPTH = 4
N_A_LAYERS = DEPTH // 2
NSA_HEADS = 16
NSA_HEAD_DIM = D_MODEL // NSA_HEADS
NSA_KV_GROUPS = 2
NSA_KV_WIDTH = NSA_KV_GROUPS * NSA_HEAD_DIM
NSA_GATE_WIDTH = 3 * NSA_HEADS
CMP_BLOCK = 32
CMP_STRIDE = 16
CMP_MLP_HIDDEN = 128
SLC_BLOCK = 64
SLC_TOPK = 16
WINDOW = 512
FORCE_BONUS = 1000.0
DIFF_HEADS = 8
DIFF_HEAD_DIM = D_MODEL // (2 * DIFF_HEADS)
DIFF_V_DIM = 2 * DIFF_HEAD_DIM
ROPE_THETA = 500000.0
ROPE_FRACTION = 4
D_FF = 2816
DEEPNORM_ALPHA = (2 * DEPTH) ** 0.25
NORM_EPS = 1e-5
NEG = -1e30

LANES = 128
ROW_TILE = 512
FFN_COL_CHUNK = 512
ATT_Q_TILE = 128
DIFF_Q_TILE = 256
ATT_KEY_BLOCK = 256
DIFF_HEADS_PER_STEP = 8
SUM_ROWS = 16
LOG2E = 1.4426950408889634
VMEM_LIMIT_BYTES = 56 * 1024 * 1024
N_CHUNKS = D_MODEL // LANES


def _cparams(n_axes):
    return pltpu.CompilerParams(dimension_semantics=("parallel",) * n_axes,
                                vmem_limit_bytes=VMEM_LIMIT_BYTES)


def _resident(shape):
    zeros = (0,) * len(shape)
    return pl.BlockSpec(shape, lambda *_: zeros, pipeline_mode=pl.Buffered(1))


def _layer_norm_rows(r, g, b):
    mu = jnp.mean(r, axis=-1, keepdims=True)
    xc = r - mu
    var = jnp.mean(xc * xc, axis=-1, keepdims=True)
    return xc * lax.rsqrt(var + NORM_EPS) * g + b


def _ffn_ln_rows(x, wg_ref, wu_ref, wd_ref, g_ref, b_ref, h_ref):
    xb = x.astype(BF16)
    d_ff = wg_ref.shape[1]
    for c0 in range(0, d_ff, FFN_COL_CHUNK):
        c1 = min(c0 + FFN_COL_CHUNK, d_ff)
        gate = jnp.dot(xb, wg_ref[:, c0:c1], preferred_element_type=F32)
        up = jnp.dot(xb, wu_ref[:, c0:c1], preferred_element_type=F32)
        silu = gate * (1.0 / (1.0 + jnp.exp(-gate)))
        h_ref[:, c0:c1] = (silu * up).astype(BF16)
    y = jnp.dot(h_ref[...], wd_ref[...], preferred_element_type=F32)
    return _layer_norm_rows(DEEPNORM_ALPHA * x + 0.5 * y, g_ref[...], b_ref[...])


def _rope_tables(seq):
    r = NSA_HEAD_DIM // ROPE_FRACTION
    half = r // 2
    inv = ROPE_THETA ** (-jnp.arange(half, dtype=F32) * 2.0 / r)
    ang = jnp.arange(seq).astype(F32)[:, None] * inv[None, :]
    cos, sin = jnp.cos(ang), jnp.sin(ang)
    pad = jnp.zeros((seq, NSA_HEAD_DIM - r), F32)
    zero = jnp.zeros_like(sin)
    c_head = jnp.concatenate([cos, cos, pad + 1.0], axis=1)
    s_lo_head = jnp.concatenate([-sin, zero, pad], axis=1)
    s_hi_head = jnp.concatenate([zero, sin, pad], axis=1)
    rep = lambda t: jnp.concatenate([t, t], axis=1)
    return rep(c_head), rep(s_lo_head), rep(s_hi_head)


def _rope_chunk(x, c, s_lo, s_hi):
    half = NSA_HEAD_DIM // ROPE_FRACTION // 2
    return x * c + pltpu.roll(x, LANES - half, 1) * s_lo + pltpu.roll(x, half, 1) * s_hi


def _nsa_proj_kernel(x_ref, wg_ref, wu_ref, wd_ref, g_ref, b_ref, w_ref, c_ref, sl_ref, sh_ref,
                     xo_ref, qc_ref, qr_ref, kc_ref, vc_ref, ks_ref, kw_ref, vst_ref, vwt_ref, gt_ref, h_ref):
    x = _ffn_ln_rows(x_ref[...], wg_ref, wu_ref, wd_ref, g_ref, b_ref, h_ref)
    xo_ref[...] = x
    proj = jnp.dot(x.astype(BF16), w_ref[...], preferred_element_type=F32)
    c, sl, sh = c_ref[...], sl_ref[...], sh_ref[...]
    col = lambda j: proj[:, j * LANES:(j + 1) * LANES]
    for j in range(N_CHUNKS):
        q = col(j) * (NSA_HEAD_DIM ** -0.5 * LOG2E)
        qc_ref[j] = q.astype(BF16)
        qr_ref[j] = _rope_chunk(q, c, sl, sh).astype(BF16)
    base = N_CHUNKS
    kc_ref[...] = col(base)
    vc_ref[...] = col(base + 1)
    for g in range(NSA_KV_GROUPS):
        ks_ref[g] = _rope_chunk(col(base + 2 + g), c, sl, sh).astype(BF16)
        kw_ref[g] = _rope_chunk(col(base + 4 + g), c, sl, sh).astype(BF16)
    n_kb = x_ref.shape[0] // ATT_KEY_BLOCK
    ones = jnp.ones((SUM_ROWS, ATT_KEY_BLOCK), BF16)
    for src, dst in ((base + 6, vst_ref), (base + 7, vwt_ref)):
        vt = col(src).T.astype(BF16)
        for g in range(NSA_KV_GROUPS):
            for kb in range(n_kb):
                dst[g, kb, :NSA_HEAD_DIM, :] = vt[g * NSA_HEAD_DIM:(g + 1) * NSA_HEAD_DIM,
                                                  kb * ATT_KEY_BLOCK:(kb + 1) * ATT_KEY_BLOCK]
                dst[g, kb, NSA_HEAD_DIM:, :] = ones
    gates = col(base + 8)
    gt_ref[...] = (1.0 / (1.0 + jnp.exp(-gates))).T


def _nsa_proj_weight(w_in):
    d = D_MODEL
    kvw = NSA_KV_WIDTH
    q, kc, vc, ks, vs, kw, vw, gates = jnp.split(
        w_in, [int(c) for c in np.cumsum([d] + [kvw] * 6)], axis=1)
    dh = NSA_HEAD_DIM
    dup = lambda t: jnp.concatenate([t[:, :dh], t[:, :dh], t[:, dh:], t[:, dh:]], axis=1)
    gates = jnp.pad(gates, ((0, 0), (0, LANES - NSA_GATE_WIDTH)))
    return jnp.concatenate([q, kc, vc, dup(ks), dup(kw), vs, vw, gates], axis=1).astype(BF16)


def _nsa_proj(x, ffn, w, tables):
    b, s, d = x.shape
    d_ff = ffn[0].shape[1]
    ncol = w.shape[1]
    nkb = s // ATT_KEY_BLOCK
    kb_tile = ROW_TILE // ATT_KEY_BLOCK
    dh = NSA_HEAD_DIM + SUM_ROWS
    tab_spec = pl.BlockSpec((ROW_TILE, LANES), lambda bi, i: (i, 0))
    chunk_spec = lambda n: pl.BlockSpec((None, n, ROW_TILE, LANES), lambda bi, i: (bi, 0, i, 0))
    row_spec = pl.BlockSpec((None, ROW_TILE, LANES), lambda bi, i: (bi, i, 0))
    vt_spec = pl.BlockSpec((None, NSA_KV_GROUPS, kb_tile, dh, ATT_KEY_BLOCK), lambda bi, i: (bi, 0, i, 0, 0))
    x_spec = pl.BlockSpec((None, ROW_TILE, d), lambda bi, i: (bi, i, 0))
    return pl.pallas_call(
        _nsa_proj_kernel,
        grid=(b, s // ROW_TILE),
        in_specs=[x_spec, _resident((d, d_ff)), _resident((d, d_ff)), _resident((d_ff, d)),
                  _resident((1, d)), _resident((1, d)),
                  _resident((d, ncol)), tab_spec, tab_spec, tab_spec],
        out_specs=[x_spec, chunk_spec(N_CHUNKS), chunk_spec(N_CHUNKS), row_spec, row_spec,
                   chunk_spec(NSA_KV_GROUPS), chunk_spec(NSA_KV_GROUPS), vt_spec, vt_spec,
                   pl.BlockSpec((None, LANES, ROW_TILE), lambda bi, i: (bi, 0, i))],
        out_shape=[jax.ShapeDtypeStruct((b, s, d), F32),
                   jax.ShapeDtypeStruct((b, N_CHUNKS, s, LANES), BF16),
                   jax.ShapeDtypeStruct((b, N_CHUNKS, s, LANES), BF16),
                   jax.ShapeDtypeStruct((b, s, LANES), F32),
                   jax.ShapeDtypeStruct((b, s, LANES), F32),
                   jax.ShapeDtypeStruct((b, NSA_KV_GROUPS, s, LANES), BF16),
                   jax.ShapeDtypeStruct((b, NSA_KV_GROUPS, s, LANES), BF16),
                   jax.ShapeDtypeStruct((b, NSA_KV_GROUPS, nkb, dh, ATT_KEY_BLOCK), BF16),
                   jax.ShapeDtypeStruct((b, NSA_KV_GROUPS, nkb, dh, ATT_KEY_BLOCK), BF16),
                   jax.ShapeDtypeStruct((b, LANES, s), F32)],
        scratch_shapes=[pltpu.VMEM((ROW_TILE, d_ff), BF16)],
        compiler_params=_cparams(2),
        name="ffn_nsa_proj",
    )(x, *ffn, w, *tables)


def _gelu_tanh(x):
    return 0.5 * x * (1.0 + jnp.tanh(np.sqrt(2.0 / np.pi).astype(np.float32) * (x + 0.044715 * (x * x * x))))


def _compress_kernel(kch_ref, vch_ref, pk_ref, pv_ref, w1k_ref, w1v_ref, b1k_ref, b1v_ref, w2k_ref, w2v_ref,
                     kcd_ref, vct_ref):
    n_rows = kch_ref.shape[0]

    def mlp(ch_ref, pos_ref, w1_ref, b1_ref, w2_ref):
        ch = ch_ref[...]
        lo = jnp.dot((ch + pos_ref[0:1, :]).astype(BF16), w1_ref[0], preferred_element_type=F32)
        hi = jnp.dot((ch + pos_ref[1:2, :]).astype(BF16), w1_ref[1], preferred_element_type=F32)
        hid = lo + pltpu.roll(hi, n_rows - 1, 0) + b1_ref[...]
        return jnp.dot(_gelu_tanh(hid).astype(BF16), w2_ref[...], preferred_element_type=F32)

    kcd = mlp(kch_ref, pk_ref, w1k_ref, b1k_ref, w2k_ref)
    for g in range(NSA_KV_GROUPS):
        kcd_ref[g] = kcd[:, g * LANES:(g + 1) * LANES].astype(BF16)
    vct = mlp(vch_ref, pv_ref, w1v_ref, b1v_ref, w2v_ref).T
    for g in range(NSA_KV_GROUPS):
        vct_ref[g] = vct[g * NSA_HEAD_DIM:(g + 1) * NSA_HEAD_DIM, :].astype(BF16)


def _compress_weights(pos, w1, b1, w2, dup_out):
    g_n, dh, hid = NSA_KV_GROUPS, NSA_HEAD_DIM, CMP_MLP_HIDDEN
    w1r = w1.reshape(2, CMP_STRIDE, dh, hid)
    eye = jnp.eye(g_n, dtype=w1.dtype)
    w1x = jnp.einsum('hldm,gk->hlgdkm', w1r, eye).reshape(2, CMP_STRIDE * g_n * dh, g_n * hid)
    posx = jnp.broadcast_to(pos.reshape(2, CMP_STRIDE, 1, dh), (2, CMP_STRIDE, g_n, dh)).reshape(2, -1)
    b1x = jnp.tile(b1, g_n)[None, :]
    reps = 2 if dup_out else 1
    w2x = jnp.einsum('md,gk->gmkd', w2, eye)
    w2x = jnp.broadcast_to(w2x[:, :, :, None, :], (g_n, hid, g_n, reps, dh)).reshape(g_n * hid, g_n * reps * dh)
    return posx, w1x.astype(BF16), b1x, w2x.astype(BF16)


def _compress(kc, vc, wk, wv):
    b, s, _ = kc.shape
    n_rows = s // CMP_STRIDE
    width = CMP_STRIDE * LANES
    kch = kc.reshape(b, n_rows, width)
    vch = vc.reshape(b, n_rows, width)
    pk, w1k, b1k, w2k = wk
    pv, w1v, b1v, w2v = wv
    ch_spec = pl.BlockSpec((None, n_rows, width), lambda bi: (bi, 0, 0))
    return pl.pallas_call(
        _compress_kernel,
        grid=(b,),
        in_specs=[ch_spec, ch_spec, _resident(pk.shape), _resident(pv.shape), _resident(w1k.shape),
                  _resident(w1v.shape), _resident(b1k.shape), _resident(b1v.shape), _resident(w2k.shape),
                  _resident(w2v.shape)],
        out_specs=[pl.BlockSpec((None, NSA_KV_GROUPS, n_rows, LANES), lambda bi: (bi, 0, 0, 0)),
                   pl.BlockSpec((None, NSA_KV_GROUPS, NSA_HEAD_DIM, n_rows), lambda bi: (bi, 0, 0, 0))],
        out_shape=[jax.ShapeDtypeStruct((b, NSA_KV_GROUPS, n_rows, LANES), BF16),
                   jax.ShapeDtypeStruct((b, NSA_KV_GROUPS, NSA_HEAD_DIM, n_rows), BF16)],
        compiler_params=_cparams(1),
        name="nsa_compress",
    )(kch, vch, pk, pv, w1k, w1v, b1k, b1v, w2k, w2v)


def _split_heads_rhs(q):
    lane = lax.broadcasted_iota(jnp.int32, q.shape, 1)
    zero = jnp.zeros_like(q)
    return jnp.concatenate([jnp.where(lane < LANES // 2, q, zero), jnp.where(lane >= LANES // 2, q, zero)], axis=0)


def _scores_t(k, rhs):
    return lax.dot_general(k, rhs, (((1,), (1,)), ((), ())), preferred_element_type=F32)


def _two(row):
    return jnp.concatenate([row, row], axis=1)


def _flash_update(s, vt_blk, m, acc_ref, idx):
    m_new = jnp.maximum(m, jnp.max(s, axis=0, keepdims=True))
    alpha = jnp.exp2(m - m_new)
    p = jnp.exp2(s - m_new).astype(BF16)
    acc_ref[idx] = alpha * acc_ref[idx] + jnp.dot(vt_blk, p, preferred_element_type=F32)
    return m_new


def _flash_update_block_bias(src_ref, idx, bias_rows, vt_blk, m, acc_ref):
    rows = src_ref.shape[1] // len(bias_rows)
    piece = lambda j: src_ref[idx, j * rows:(j + 1) * rows, :]
    m_new = m
    for j, b in enumerate(bias_rows):
        m_new = jnp.maximum(m_new, jnp.max(piece(j), axis=0, keepdims=True) + b)
    alpha = jnp.exp2(m - m_new)
    p = jnp.concatenate([jnp.exp2(piece(j) + (b - m_new)).astype(BF16) for j, b in enumerate(bias_rows)], axis=0)
    acc_ref[idx] = alpha * acc_ref[idx] + jnp.dot(vt_blk, p, preferred_element_type=F32)
    return m_new


def _normalized(acc, dv):
    return acc[:dv] * (1.0 / acc[dv:dv + 1])


def _nsa_attn_kernel(qc_ref, qr_ref, kcd_ref, vct_ref, ks_ref, kw_ref, vst_ref, vwt_ref, gt_ref, ov_ref,
                     o_ref, og_ref, acc_ref, selb_ref, sa_ref, sb_ref):
    qi = pl.program_id(1)
    tq = ATT_Q_TILE
    t0 = qi * tq
    n_pairs = qc_ref.shape[0]
    n_groups = kcd_ref.shape[0]
    group_of = lambda c: c // (n_pairs // n_groups)
    n_cmp = kcd_ref.shape[1]
    seq = ks_ref.shape[1]
    n_sel = seq // SLC_BLOCK
    tok1 = t0 + lax.broadcasted_iota(jnp.int32, (1, tq), 1)
    tok2 = _two(tok1)

    kb_sz = ATT_KEY_BLOCK

    def gate(j, c):
        return jnp.concatenate([gt_ref[j * NSA_HEADS + 2 * c:j * NSA_HEADS + 2 * c + 1, :],
                                gt_ref[j * NSA_HEADS + 2 * c + 1:j * NSA_HEADS + 2 * c + 2, :]], axis=1)

    rhs = [_split_heads_rhs(qr_ref[c]) for c in range(n_pairs)]

    def sel_scores(kb, dst_ref):
        k0 = pl.multiple_of(kb * kb_sz, kb_sz)
        for c in range(n_pairs):
            dst_ref[c] = _scores_t(ks_ref[group_of(c), pl.ds(k0, kb_sz), :], rhs[c])

    win_blocks = WINDOW // kb_sz + 1
    win_start = jnp.minimum(jnp.maximum(t0 - WINDOW, 0) // kb_sz, seq // kb_sz - win_blocks)
    w0 = pl.multiple_of(win_start * kb_sz, kb_sz)

    cmp_scores = [_scores_t(kcd_ref[group_of(c)], _split_heads_rhs(qc_ref[c])) for c in range(n_pairs)]
    win_scores = [_scores_t(kw_ref[group_of(c), pl.ds(w0, win_blocks * kb_sz), :], rhs[c]) for c in range(n_pairs)]

    cmp_end = lax.broadcasted_iota(jnp.int32, (n_cmp, 1), 0) * CMP_STRIDE + (CMP_BLOCK - 1)
    vis = cmp_end <= tok2
    any_vis = (tok2 >= CMP_BLOCK - 1).astype(F32)
    psum = [jnp.zeros((n_cmp, tq), F32) for _ in range(n_groups)]
    for c in range(n_pairs):
        s = jnp.where(vis, cmp_scores[c], NEG)
        m = jnp.max(s, axis=0, keepdims=True)
        e = jnp.exp2(s - m)
        p = e * (any_vis / jnp.sum(e, axis=0, keepdims=True))
        psum[group_of(c)] += p[:, :tq] + p[:, tq:]
        og_ref[c] = gate(0, c) * jnp.dot(vct_ref[group_of(c)], p.astype(BF16), preferred_element_type=F32)

    imp_raw = [jnp.dot(ov_ref[...], ps, precision=lax.Precision.HIGHEST, preferred_element_type=F32)
               for ps in psum]
    sel_scores(0, sa_ref)

    wrow = w0 + lax.broadcasted_iota(jnp.int32, (win_blocks * kb_sz, 1), 0)
    in_window = (wrow <= tok2) & (wrow > tok2 - WINDOW)
    for c in range(n_pairs):
        s = jnp.where(in_window, win_scores[c], NEG)
        m = jnp.max(s, axis=0, keepdims=True)
        p = jnp.exp2(s - m).astype(BF16)
        win = jnp.zeros((NSA_HEAD_DIM + SUM_ROWS, 2 * tq), F32)
        for j in range(win_blocks):
            win += jnp.dot(vwt_ref[group_of(c), win_start + j], p[j * kb_sz:(j + 1) * kb_sz, :],
                           preferred_element_type=F32)
        og_ref[c] += gate(2, c) * _normalized(win, NSA_HEAD_DIM)

    blk = lax.broadcasted_iota(jnp.int32, (n_sel, 1), 0)
    cur = tok1 // SLC_BLOCK
    forced = (blk == 0) | (blk == cur) | (blk == cur - 1)
    valid = blk <= cur
    rows8 = 8
    sub = lax.broadcasted_iota(jnp.int32, (rows8, 1), 0)
    for gi in range(n_groups):
        imp = jnp.where(valid, imp_raw[gi] + jnp.where(forced, FORCE_BONUS, 0.0), NEG)
        grp = [imp[rows8 * j:rows8 * (j + 1), :] for j in range(n_sel // rows8)]
        rank = [jnp.zeros((rows8, tq), jnp.int32) for _ in grp]
        for m in range(n_sel):
            row = imp[m:m + 1, :]
            for j in range(len(grp)):
                ge = lambda: (row >= grp[j]).astype(jnp.int32)
                gt = lambda: (row > grp[j]).astype(jnp.int32)
                if rows8 * j > m:
                    beats = ge()
                elif rows8 * j + rows8 - 1 <= m:
                    beats = gt()
                else:
                    beats = jnp.where(sub + rows8 * j > m, ge(), gt())
                rank[j] = rank[j] + beats
        rank = jnp.concatenate(rank, axis=0)
        selb_ref[gi] = jnp.where((rank < SLC_TOPK) & valid, 0.0, NEG)

    last = (t0 + tq - 1) // kb_sz
    blocks_per_kb = kb_sz // SLC_BLOCK
    krow = lax.broadcasted_iota(jnp.int32, (kb_sz, 1), 0)
    acc_ref[...] = jnp.zeros_like(acc_ref)

    def sel_bias_rows(kb):
        return [[_two(selb_ref[gi, pl.ds(kb * blocks_per_kb + j, 1), :]) for j in range(blocks_per_kb)]
                for gi in range(n_groups)]

    def sel_below_diagonal(kb, src_ref, st):
        rows = sel_bias_rows(kb)
        return tuple(_flash_update_block_bias(src_ref, c, rows[group_of(c)], vst_ref[group_of(c), kb], st[c], acc_ref)
                     for c in range(n_pairs))

    def sel_diagonal(kb, src_ref, st):
        causal = kb * kb_sz + krow <= tok2
        bias = [jnp.where(causal, jnp.concatenate([jnp.broadcast_to(r, (SLC_BLOCK, 2 * tq)) for r in rows], axis=0),
                          NEG) for rows in sel_bias_rows(kb)]
        return tuple(_flash_update(src_ref[c] + bias[group_of(c)], vst_ref[group_of(c), kb], st[c], acc_ref, c)
                     for c in range(n_pairs))

    def sel_two_blocks(i, st):
        kb = 2 * i
        sel_scores(kb + 1, sb_ref)
        st = sel_below_diagonal(kb, sa_ref, st)
        sel_scores(kb + 2, sa_ref)
        return sel_below_diagonal(kb + 1, sb_ref, st)

    def sel_tail_odd(st):
        sel_scores(last, sb_ref)
        st = sel_below_diagonal(last - 1, sa_ref, st)
        return sel_diagonal(last, sb_ref, st)

    init = tuple(jnp.full((1, 2 * tq), NEG, F32) for _ in range(n_pairs))
    st = lax.fori_loop(0, last // 2, sel_two_blocks, init)
    st = lax.cond(last % 2 == 0, lambda st: sel_diagonal(last, sa_ref, st), sel_tail_odd, st)

    for c in range(n_pairs):
        o_t = og_ref[c] + gate(1, c) * _normalized(acc_ref[c], NSA_HEAD_DIM)
        o_ref[c] = jnp.concatenate([o_t[:, :tq], o_t[:, tq:]], axis=0).T.astype(BF16)


def _cmp_to_slc_t(n_cmp_rows, n_sel):
    c0 = np.arange(n_cmp_rows)[None, :] * CMP_STRIDE
    s0 = np.arange(n_sel)[:, None] * SLC_BLOCK
    ov = np.minimum(c0 + CMP_BLOCK, s0 + SLC_BLOCK) - np.maximum(c0, s0)
    return (np.clip(ov, 0, None) / CMP_BLOCK).astype(np.float32)


def _nsa_attn(qc, qr, kcd, vct, ksd, kwd, vst, vwt, gt):
    b, pairs, s, _ = qc.shape
    groups = NSA_KV_GROUPS
    n_cmp = kcd.shape[2]
    nkb = s // ATT_KEY_BLOCK
    dh = NSA_HEAD_DIM
    tq = ATT_Q_TILE
    n_sel = s // SLC_BLOCK
    ov_t = jnp.asarray(_cmp_to_slc_t(n_cmp, n_sel))
    q_spec = pl.BlockSpec((None, pairs, tq, LANES), lambda bi, qi: (bi, 0, qi, 0))
    per_batch = lambda *shape: pl.BlockSpec((None,) + shape, lambda bi, qi: (bi,) + (0,) * len(shape))
    return pl.pallas_call(
        _nsa_attn_kernel,
        grid=(b, s // tq),
        in_specs=[q_spec, q_spec, per_batch(groups, n_cmp, LANES), per_batch(groups, dh, n_cmp),
                  per_batch(groups, s, LANES), per_batch(groups, s, LANES),
                  per_batch(groups, nkb, dh + SUM_ROWS, ATT_KEY_BLOCK),
                  per_batch(groups, nkb, dh + SUM_ROWS, ATT_KEY_BLOCK),
                  pl.BlockSpec((None, LANES, tq), lambda bi, qi: (bi, 0, qi)),
                  _resident(ov_t.shape)],
        out_specs=q_spec,
        out_shape=jax.ShapeDtypeStruct((b, pairs, s, LANES), BF16),
        scratch_shapes=[pltpu.VMEM((pairs, dh, 2 * tq), F32), pltpu.VMEM((pairs, dh + SUM_ROWS, 2 * tq), F32),
                        pltpu.VMEM((groups, n_sel, tq), F32),
                        pltpu.VMEM((pairs, ATT_KEY_BLOCK, 2 * tq), F32), pltpu.VMEM((pairs, ATT_KEY_BLOCK, 2 * tq), F32)],
        compiler_params=_cparams(2),
        name="nsa_attn",
    )(qc, qr, kcd, vct, ksd, kwd, vst, vwt, gt, ov_t)


def _out_ffn_kernel(o_ref, x_ref, wo_ref, g1_ref, b1_ref, wg_ref, wu_ref, wd_ref, g2_ref, b2_ref, y_ref, h_ref):
    o = jnp.concatenate([o_ref[j] for j in range(o_ref.shape[0])], axis=1)
    mix = jnp.dot(o, wo_ref[...], preferred_element_type=F32)
    x1 = _layer_norm_rows(DEEPNORM_ALPHA * x_ref[...] + mix, g1_ref[...], b1_ref[...])
    y_ref[...] = _ffn_ln_rows(x1, wg_ref, wu_ref, wd_ref, g2_ref, b2_ref, h_ref)


def _out_ffn(o, x, wo, g1, b1, wg, wu, wd, g2, b2):
    bsz, n_chunks, s, _ = o.shape
    d = x.shape[-1]
    d_ff = wg.shape[1]
    x_spec = pl.BlockSpec((None, ROW_TILE, d), lambda bi, i: (bi, i, 0))
    return pl.pallas_call(
        _out_ffn_kernel,
        grid=(bsz, s // ROW_TILE),
        in_specs=[pl.BlockSpec((None, n_chunks, ROW_TILE, LANES), lambda bi, i: (bi, 0, i, 0)), x_spec,
                  _resident((d, d)), _resident((1, d)), _resident((1, d)),
                  _resident((d, d_ff)), _resident((d, d_ff)), _resident((d_ff, d)),
                  _resident((1, d)), _resident((1, d))],
        out_specs=x_spec,
        out_shape=jax.ShapeDtypeStruct(x.shape, F32),
        scratch_shapes=[pltpu.VMEM((ROW_TILE, d_ff), BF16)],
        compiler_params=_cparams(2),
        name="mixer_out_ffn",
    )(o, x, wo, g1, b1, wg, wu, wd, g2, b2)


def _pair_maps_columns(w):
    d = w.shape[0]
    return w.reshape(d, 2, DIFF_HEADS, DIFF_HEAD_DIM).transpose(0, 2, 1, 3).reshape(d, -1)


def _kv_proj_kernel(x_ref, w_ref, c_ref, sl_ref, sh_ref, k_ref, vt_ref):
    proj = jnp.dot(x_ref[...].astype(BF16), w_ref[...], preferred_element_type=F32)
    c, sl, sh = c_ref[...], sl_ref[...], sh_ref[...]
    n_kb = x_ref.shape[0] // ATT_KEY_BLOCK
    ones = jnp.ones((SUM_ROWS, ATT_KEY_BLOCK), BF16)
    for h in range(DIFF_HEADS):
        k_ref[h] = _rope_chunk(proj[:, h * LANES:(h + 1) * LANES], c, sl, sh).astype(BF16)
        vt = proj[:, D_MODEL + h * DIFF_V_DIM:D_MODEL + (h + 1) * DIFF_V_DIM].T.astype(BF16)
        for kb in range(n_kb):
            vt_ref[h, kb, :DIFF_V_DIM, :] = vt[:, kb * ATT_KEY_BLOCK:(kb + 1) * ATT_KEY_BLOCK]
            vt_ref[h, kb, DIFF_V_DIM:, :] = ones


def _kv_proj(x, w, tables):
    b, s, d = x.shape
    nkb = s // ATT_KEY_BLOCK
    kb_tile = ROW_TILE // ATT_KEY_BLOCK
    tab_spec = pl.BlockSpec((ROW_TILE, LANES), lambda bi, i: (i, 0))
    return pl.pallas_call(
        _kv_proj_kernel,
        grid=(b, s // ROW_TILE),
        in_specs=[pl.BlockSpec((None, ROW_TILE, d), lambda bi, i: (bi, i, 0)), _resident(w.shape),
                  tab_spec, tab_spec, tab_spec],
        out_specs=[pl.BlockSpec((None, DIFF_HEADS, ROW_TILE, LANES), lambda bi, i: (bi, 0, i, 0)),
                   pl.BlockSpec((None, DIFF_HEADS, kb_tile, DIFF_V_DIM + SUM_ROWS, ATT_KEY_BLOCK),
                                lambda bi, i: (bi, 0, i, 0, 0))],
        out_shape=[jax.ShapeDtypeStruct((b, DIFF_HEADS, s, LANES), BF16),
                   jax.ShapeDtypeStruct((b, DIFF_HEADS, nkb, DIFF_V_DIM + SUM_ROWS, ATT_KEY_BLOCK), BF16)],
        compiler_params=_cparams(2),
        name="diff_kv_proj",
    )(x, w, *tables)


def _dq_proj_kernel(x_ref, wg_ref, wu_ref, wd_ref, g_ref, b_ref, w_ref, c_ref, sl_ref, sh_ref,
                    xo_ref, q_ref, h_ref):
    x = _ffn_ln_rows(x_ref[...], wg_ref, wu_ref, wd_ref, g_ref, b_ref, h_ref)
    xo_ref[...] = x
    proj = jnp.dot(x.astype(BF16), w_ref[...], preferred_element_type=F32)
    c, sl, sh = c_ref[...], sl_ref[...], sh_ref[...]
    for h in range(DIFF_HEADS):
        q = proj[:, h * LANES:(h + 1) * LANES] * (DIFF_HEAD_DIM ** -0.5 * LOG2E)
        q_ref[h] = _rope_chunk(q, c, sl, sh).astype(BF16)


def _dq_proj(x, ffn, w, tables):
    b, s, d = x.shape
    d_ff = ffn[0].shape[1]
    tab_spec = pl.BlockSpec((ROW_TILE, LANES), lambda bi, i: (i, 0))
    x_spec = pl.BlockSpec((None, ROW_TILE, d), lambda bi, i: (bi, i, 0))
    return pl.pallas_call(
        _dq_proj_kernel,
        grid=(b, s // ROW_TILE),
        in_specs=[x_spec, _resident((d, d_ff)), _resident((d, d_ff)), _resident((d_ff, d)),
                  _resident((1, d)), _resident((1, d)), _resident(w.shape), tab_spec, tab_spec, tab_spec],
        out_specs=[x_spec, pl.BlockSpec((None, DIFF_HEADS, ROW_TILE, LANES), lambda bi, i: (bi, 0, i, 0))],
        out_shape=[jax.ShapeDtypeStruct((b, s, d), F32), jax.ShapeDtypeStruct((b, DIFF_HEADS, s, LANES), BF16)],
        scratch_shapes=[pltpu.VMEM((ROW_TILE, d_ff), BF16)],
        compiler_params=_cparams(2),
        name="ffn_diff_q_proj",
    )(x, *ffn, w, *tables)


def _diff_attn_kernel(q_ref, k_ref, vt_ref, lam_ref, gain_ref, o_ref, acc_ref, sa_ref, sb_ref, *, lambda_init):
    qi = pl.program_id(2)
    n_heads = q_ref.shape[0]
    tq = DIFF_Q_TILE
    t0 = qi * tq
    kb_sz = ATT_KEY_BLOCK
    tok2 = _two(t0 + lax.broadcasted_iota(jnp.int32, (1, tq), 1))
    krow = lax.broadcasted_iota(jnp.int32, (kb_sz, 1), 0)
    rhs = [_split_heads_rhs(q_ref[h]) for h in range(n_heads)]
    last = (t0 + tq - 1) // kb_sz
    acc_ref[...] = jnp.zeros_like(acc_ref)

    def scores_into(kb, dst_ref):
        k0 = pl.multiple_of(kb * kb_sz, kb_sz)
        for h in range(n_heads):
            dst_ref[h] = _scores_t(k_ref[h, pl.ds(k0, kb_sz), :], rhs[h])

    def consume(kb, src_ref, st, maybe_diagonal):
        if maybe_diagonal:
            visible = kb * kb_sz + krow <= tok2
        out = []
        for h in range(n_heads):
            s = src_ref[h]
            if maybe_diagonal:
                s = jnp.where(visible, s, NEG)
            out.append(_flash_update(s, vt_ref[h, kb], st[h], acc_ref, h))
        return tuple(out)

    def two_blocks(i, st):
        kb = 2 * i
        scores_into(kb + 1, sb_ref)
        st = consume(kb, sa_ref, st, False)
        scores_into(kb + 2, sa_ref)
        return consume(kb + 1, sb_ref, st, False)

    def tail_odd(st):
        scores_into(last, sb_ref)
        st = consume(last - 1, sa_ref, st, False)
        return consume(last, sb_ref, st, True)

    init = tuple(jnp.full((1, 2 * tq), NEG, F32) for _ in range(n_heads))
    scores_into(0, sa_ref)
    st = lax.fori_loop(0, last // 2, two_blocks, init)
    st = lax.cond(last % 2 == 0, lambda st: consume(last, sa_ref, st, True), tail_odd, st)

    lp = lam_ref[...]
    lam = (jnp.exp(jnp.sum(lp[0:1] * lp[1:2], axis=1, keepdims=True))
           - jnp.exp(jnp.sum(lp[2:3] * lp[3:4], axis=1, keepdims=True)) + lambda_init)
    for h in range(n_heads):
        a = _normalized(acc_ref[h], DIFF_V_DIM)
        o = a[:, :tq] - lam * a[:, tq:]
        ms = jnp.mean(o * o, axis=0, keepdims=True)
        o = o * lax.rsqrt(ms + NORM_EPS) * gain_ref[...] * (1.0 - lambda_init)
        o_ref[h] = o.T.astype(BF16)


def _diff_attn(q, k, vt, lam_params, gain, lambda_init):
    b, h, s, _ = q.shape
    nkb = s // ATT_KEY_BLOCK
    tq = DIFF_Q_TILE
    hps = DIFF_HEADS_PER_STEP
    per_head = lambda *shape: pl.BlockSpec((None, hps) + shape, lambda bi, hi, qi: (bi, hi) + (0,) * len(shape))
    q_spec = pl.BlockSpec((None, hps, tq, LANES), lambda bi, hi, qi: (bi, hi, qi, 0))
    return pl.pallas_call(
        functools.partial(_diff_attn_kernel, lambda_init=lambda_init),
        grid=(b, h // hps, s // tq),
        in_specs=[q_spec, per_head(s, LANES), per_head(nkb, DIFF_V_DIM + SUM_ROWS, ATT_KEY_BLOCK),
                  _resident(lam_params.shape), _resident(gain.shape)],
        out_specs=q_spec,
        out_shape=jax.ShapeDtypeStruct((b, h, s, LANES), BF16),
        scratch_shapes=[pltpu.VMEM((hps, DIFF_V_DIM + SUM_ROWS, 2 * tq), F32),
                        pltpu.VMEM((hps, ATT_KEY_BLOCK, 2 * tq), F32), pltpu.VMEM((hps, ATT_KEY_BLOCK, 2 * tq), F32)],
        compiler_params=_cparams(3),
        name="diff_attn",
    )(q, k, vt, lam_params, gain)


def kernel(x, ffn1_w_gate, ffn1_w_up, ffn1_w_down, ffn2_w_gate, ffn2_w_up, ffn2_w_down, ln_g, ln_b, a_w_in,
           a_cmp_pos_k, a_cmp_w1_k, a_cmp_b1_k, a_cmp_w2_k, a_cmp_pos_v, a_cmp_w1_v, a_cmp_b1_v, a_cmp_w2_v,
           a_w_out, b_w_kv_shared, b_w_q, b_lambda_q1, b_lambda_k1, b_lambda_q2, b_lambda_k2, b_subln_g, b_w_out):
    bsz, seq, d = x.shape
    assert d == D_MODEL and seq % ROW_TILE == 0 and seq >= WINDOW + 2 * ATT_KEY_BLOCK
    tables = _rope_tables(seq)
    bf = lambda w: w.astype(BF16)
    row = lambda v: v[None, :]

    k12 = vt = None
    for layer in range(DEPTH):
        ffn1 = (bf(ffn1_w_gate[layer]), bf(ffn1_w_up[layer]), bf(ffn1_w_down[layer]),
                row(ln_g[layer, 0]), row(ln_b[layer, 0]))
        if layer < N_A_LAYERS:
            x, qc, qr, kc, vc, ksd, kwd, vst, vwt, gt = _nsa_proj(x, ffn1, _nsa_proj_weight(a_w_in[layer]), tables)
            wk = _compress_weights(a_cmp_pos_k[layer], a_cmp_w1_k[layer], a_cmp_b1_k[layer], a_cmp_w2_k[layer], True)
            wv = _compress_weights(a_cmp_pos_v[layer], a_cmp_w1_v[layer], a_cmp_b1_v[layer], a_cmp_w2_v[layer], False)
            kcd, vct = _compress(kc, vc, wk, wv)
            o = _nsa_attn(qc, qr, kcd, vct, ksd, kwd, vst, vwt, gt)
            w_out = a_w_out[layer]
        else:
            j = layer - N_A_LAYERS
            lambda_init = 0.8 - 0.6 * float(np.exp(-0.3 * layer))
            x, q = _dq_proj(x, ffn1, bf(_pair_maps_columns(b_w_q[j])), tables)
            lam_params = jnp.pad(jnp.stack([b_lambda_q1[j], b_lambda_k1[j], b_lambda_q2[j], b_lambda_k2[j]]),
                                 ((0, 4), (0, LANES - DIFF_HEAD_DIM)))
            gain = jnp.broadcast_to(b_subln_g[j][:, None], (DIFF_V_DIM, DIFF_Q_TILE))
            o = _diff_attn(q, k12, vt, lam_params, gain, lambda_init)
            w_out = b_w_out[j]
        x = _out_ffn(o, x, bf(w_out), row(ln_g[layer, 1]), row(ln_b[layer, 1]),
                     bf(ffn2_w_gate[layer]), bf(ffn2_w_up[layer]), bf(ffn2_w_down[layer]),
                     row(ln_g[layer, 2]), row(ln_b[layer, 2]))
        if layer == N_A_LAYERS - 1:
            w_kv = jnp.concatenate([_pair_maps_columns(b_w_kv_shared[:, :D_MODEL]), b_w_kv_shared[:, D_MODEL:]], axis=1)
            k12, vt = _kv_proj(x, bf(w_kv), tables)
    return x
```

```python
import functools

import numpy as np
import jax
import jax.numpy as jnp
from jax import lax
from jax.experimental import pallas as pl
from jax.experimental.pallas import tpu as pltpu

F32 = jnp.float32
BF16 = jnp.bfloat16

D_MODEL = 1024
DEPTH = 4
N_A_LAYERS = DEPTH // 2
NSA_HEADS = 16
NSA_HEAD_DIM = D_MODEL // NSA_HEADS
NSA_KV_GROUPS = 2
NSA_KV_WIDTH = NSA_KV_GROUPS * NSA_HEAD_DIM
NSA_GATE_WIDTH = 3 * NSA_HEADS
CMP_BLOCK = 32
CMP_STRIDE = 16
CMP_MLP_HIDDEN = 128
SLC_BLOCK = 64
SLC_TOPK = 16
WINDOW = 512
FORCE_BONUS = 1000.0
DIFF_HEADS = 8
DIFF_HEAD_DIM = D_MODEL // (2 * DIFF_HEADS)
DIFF_V_DIM = 2 * DIFF_HEAD_DIM
ROPE_THETA = 500000.0
ROPE_FRACTION = 4
D_FF = 2816
DEEPNORM_ALPHA = (2 * DEPTH) ** 0.25
NORM_EPS = 1e-5
NEG = -1e30

LANES = 128
ROW_TILE = 512
FFN_COL_CHUNK = 512
ATT_Q_TILE = 256
DIFF_Q_TILE = 256
ATT_KEY_BLOCK = 256
DIFF_HEADS_PER_STEP = 8
SUM_ROWS = 16
LOG2E = 1.4426950408889634
VMEM_LIMIT_BYTES = 56 * 1024 * 1024
N_CHUNKS = D_MODEL // LANES


def _cparams(n_axes):
    return pltpu.CompilerParams(dimension_semantics=("parallel",) * n_axes,
                                vmem_limit_bytes=VMEM_LIMIT_BYTES)


def _resident(shape):
    zeros = (0,) * len(shape)
    return pl.BlockSpec(shape, lambda *_: zeros, pipeline_mode=pl.Buffered(1))


def _layer_norm_rows(r, g, b):
    mu = jnp.mean(r, axis=-1, keepdims=True)
    xc = r - mu
    var = jnp.mean(xc * xc, axis=-1, keepdims=True)
    return xc * lax.rsqrt(var + NORM_EPS) * g + b


def _ffn_ln_rows(x, wg_ref, wu_ref, wd_ref, g_ref, b_ref, h_ref):
    xb = x.astype(BF16)
    d_ff = wg_ref.shape[1]
    for c0 in range(0, d_ff, FFN_COL_CHUNK):
        c1 = min(c0 + FFN_COL_CHUNK, d_ff)
        gate = jnp.dot(xb, wg_ref[:, c0:c1], preferred_element_type=F32)
        up = jnp.dot(xb, wu_ref[:, c0:c1], preferred_element_type=F32)
        silu = gate * (1.0 / (1.0 + jnp.exp(-gate)))
        h_ref[:, c0:c1] = (silu * up).astype(BF16)
    y = jnp.dot(h_ref[...], wd_ref[...], preferred_element_type=F32)
    return _layer_norm_rows(DEEPNORM_ALPHA * x + 0.5 * y, g_ref[...], b_ref[...])


def _rope_tables(seq):
    r = NSA_HEAD_DIM // ROPE_FRACTION
    half = r // 2
    inv = ROPE_THETA ** (-jnp.arange(half, dtype=F32) * 2.0 / r)
    ang = jnp.arange(seq).astype(F32)[:, None] * inv[None, :]
    cos, sin = jnp.cos(ang), jnp.sin(ang)
    pad = jnp.zeros((seq, NSA_HEAD_DIM - r), F32)
    zero = jnp.zeros_like(sin)
    c_head = jnp.concatenate([cos, cos, pad + 1.0], axis=1)
    s_lo_head = jnp.concatenate([-sin, zero, pad], axis=1)
    s_hi_head = jnp.concatenate([zero, sin, pad], axis=1)
    rep = lambda t: jnp.concatenate([t, t], axis=1)
    return rep(c_head), rep(s_lo_head), rep(s_hi_head)


def _rope_chunk(x, c, s_lo, s_hi):
    half = NSA_HEAD_DIM // ROPE_FRACTION // 2
    return x * c + pltpu.roll(x, LANES - half, 1) * s_lo + pltpu.roll(x, half, 1) * s_hi


def _nsa_proj_kernel(x_ref, wg_ref, wu_ref, wd_ref, g_ref, b_ref, w_ref, c_ref, sl_ref, sh_ref,
                     xo_ref, qc_ref, qr_ref, kc_ref, vc_ref, ks_ref, kw_ref, vst_ref, vwt_ref, gt_ref, h_ref):
    x = _ffn_ln_rows(x_ref[...], wg_ref, wu_ref, wd_ref, g_ref, b_ref, h_ref)
    xo_ref[...] = x
    proj = jnp.dot(x.astype(BF16), w_ref[...], preferred_element_type=F32)
    c, sl, sh = c_ref[...], sl_ref[...], sh_ref[...]
    col = lambda j: proj[:, j * LANES:(j + 1) * LANES]
    for j in range(N_CHUNKS):
        q = col(j) * (NSA_HEAD_DIM ** -0.5 * LOG2E)
        qc_ref[j] = q.astype(BF16)
        qr_ref[j] = _rope_chunk(q, c, sl, sh).astype(BF16)
    base = N_CHUNKS
    kc_ref[...] = col(base)
    vc_ref[...] = col(base + 1)
    for g in range(NSA_KV_GROUPS):
        ks_ref[g] = _rope_chunk(col(base + 2 + g), c, sl, sh).astype(BF16)
        kw_ref[g] = _rope_chunk(col(base + 4 + g), c, sl, sh).astype(BF16)
    n_kb = x_ref.shape[0] // ATT_KEY_BLOCK
    ones = jnp.ones((SUM_ROWS, ATT_KEY_BLOCK), BF16)
    for src, dst in ((base + 6, vst_ref), (base + 7, vwt_ref)):
        vt = col(src).T.astype(BF16)
        for g in range(NSA_KV_GROUPS):
            for kb in range(n_kb):
                dst[g, kb, :NSA_HEAD_DIM, :] = vt[g * NSA_HEAD_DIM:(g + 1) * NSA_HEAD_DIM,
                                                  kb * ATT_KEY_BLOCK:(kb + 1) * ATT_KEY_BLOCK]
                dst[g, kb, NSA_HEAD_DIM:, :] = ones
    gates = col(base + 8)
    gt_ref[...] = (1.0 / (1.0 + jnp.exp(-gates))).T


def _nsa_proj_weight(w_in):
    d = D_MODEL
    kvw = NSA_KV_WIDTH
    q, kc, vc, ks, vs, kw, vw, gates = jnp.split(
        w_in, [int(c) for c in np.cumsum([d] + [kvw] * 6)], axis=1)
    dh = NSA_HEAD_DIM
    dup = lambda t: jnp.concatenate([t[:, :dh], t[:, :dh], t[:, dh:], t[:, dh:]], axis=1)
    gates = jnp.pad(gates, ((0, 0), (0, LANES - NSA_GATE_WIDTH)))
    return jnp.concatenate([q, kc, vc, dup(ks), dup(kw), vs, vw, gates], axis=1).astype(BF16)


def _nsa_proj(x, ffn, w, tables):
    b, s, d = x.shape
    d_ff = ffn[0].shape[1]
    ncol = w.shape[1]
    nkb = s // ATT_KEY_BLOCK
    kb_tile = ROW_TILE // ATT_KEY_BLOCK
    dh = NSA_HEAD_DIM + SUM_ROWS
    tab_spec = pl.BlockSpec((ROW_TILE, LANES), lambda bi, i: (i, 0))
    chunk_spec = lambda n: pl.BlockSpec((None, n, ROW_TILE, LANES), lambda bi, i: (bi, 0, i, 0))
    row_spec = pl.BlockSpec((None, ROW_TILE, LANES), lambda bi, i: (bi, i, 0))
    vt_spec = pl.BlockSpec((None, NSA_KV_GROUPS, kb_tile, dh, ATT_KEY_BLOCK), lambda bi, i: (bi, 0, i, 0, 0))
    x_spec = pl.BlockSpec((None, ROW_TILE, d), lambda bi, i: (bi, i, 0))
    return pl.pallas_call(
        _nsa_proj_kernel,
        grid=(b, s // ROW_TILE),
        in_specs=[x_spec, _resident((d, d_ff)), _resident((d, d_ff)), _resident((d_ff, d)),
                  _resident((1, d)), _resident((1, d)),
                  _resident((d, ncol)), tab_spec, tab_spec, tab_spec],
        out_specs=[x_spec, chunk_spec(N_CHUNKS), chunk_spec(N_CHUNKS), row_spec, row_spec,
                   chunk_spec(NSA_KV_GROUPS), chunk_spec(NSA_KV_GROUPS), vt_spec, vt_spec,
                   pl.BlockSpec((None, LANES, ROW_TILE), lambda bi, i: (bi, 0, i))],
        out_shape=[jax.ShapeDtypeStruct((b, s, d), F32),
                   jax.ShapeDtypeStruct((b, N_CHUNKS, s, LANES), BF16),
                   jax.ShapeDtypeStruct((b, N_CHUNKS, s, LANES), BF16),
                   jax.ShapeDtypeStruct((b, s, LANES), F32),
                   jax.ShapeDtypeStruct((b, s, LANES), F32),
                   jax.ShapeDtypeStruct((b, NSA_KV_GROUPS, s, LANES), BF16),
                   jax.ShapeDtypeStruct((b, NSA_KV_GROUPS, s, LANES), BF16),
                   jax.ShapeDtypeStruct((b, NSA_KV_GROUPS, nkb, dh, ATT_KEY_BLOCK), BF16),
                   jax.ShapeDtypeStruct((b, NSA_KV_GROUPS, nkb, dh, ATT_KEY_BLOCK), BF16),
                   jax.ShapeDtypeStruct((b, LANES, s), F32)],
        scratch_shapes=[pltpu.VMEM((ROW_TILE, d_ff), BF16)],
        compiler_params=_cparams(2),
        name="ffn_nsa_proj",
    )(x, *ffn, w, *tables)


def _gelu_tanh(x):
    return 0.5 * x * (1.0 + jnp.tanh(np.sqrt(2.0 / np.pi).astype(np.float32) * (x + 0.044715 * (x * x * x))))


def _compress_kernel(kch_ref, vch_ref, pk_ref, pv_ref, w1k_ref, w1v_ref, b1k_ref, b1v_ref, w2k_ref, w2v_ref,
                     kcd_ref, vct_ref):
    n_rows = kch_ref.shape[0]

    def mlp(ch_ref, pos_ref, w1_ref, b1_ref, w2_ref):
        ch = ch_ref[...]
        lo = jnp.dot((ch + pos_ref[0:1, :]).astype(BF16), w1_ref[0], preferred_element_type=F32)
        hi = jnp.dot((ch + pos_ref[1:2, :]).astype(BF16), w1_ref[1], preferred_element_type=F32)
        hid = lo + pltpu.roll(hi, n_rows - 1, 0) + b1_ref[...]
        return jnp.dot(_gelu_tanh(hid).astype(BF16), w2_ref[...], preferred_element_type=F32)

    kcd = mlp(kch_ref, pk_ref, w1k_ref, b1k_ref, w2k_ref)
    for g in range(NSA_KV_GROUPS):
        kcd_ref[g] = kcd[:, g * LANES:(g + 1) * LANES].astype(BF16)
    vct = mlp(vch_ref, pv_ref, w1v_ref, b1v_ref, w2v_ref).T
    for g in range(NSA_KV_GROUPS):
        vct_ref[g] = vct[g * NSA_HEAD_DIM:(g + 1) * NSA_HEAD_DIM, :].astype(BF16)


def _compress_weights(pos, w1, b1, w2, dup_out):
    g_n, dh, hid = NSA_KV_GROUPS, NSA_HEAD_DIM, CMP_MLP_HIDDEN
    w1r = w1.reshape(2, CMP_STRIDE, dh, hid)
    eye = jnp.eye(g_n, dtype=w1.dtype)
    w1x = jnp.einsum('hldm,gk->hlgdkm', w1r, eye).reshape(2, CMP_STRIDE * g_n * dh, g_n * hid)
    posx = jnp.broadcast_to(pos.reshape(2, CMP_STRIDE, 1, dh), (2, CMP_STRIDE, g_n, dh)).reshape(2, -1)
    b1x = jnp.tile(b1, g_n)[None, :]
    reps = 2 if dup_out else 1
    w2x = jnp.einsum('md,gk->gmkd', w2, eye)
    w2x = jnp.broadcast_to(w2x[:, :, :, None, :], (g_n, hid, g_n, reps, dh)).reshape(g_n * hid, g_n * reps * dh)
    return posx, w1x.astype(BF16), b1x, w2x.astype(BF16)


def _compress(kc, vc, wk, wv):
    b, s, _ = kc.shape
    n_rows = s // CMP_STRIDE
    width = CMP_STRIDE * LANES
    kch = kc.reshape(b, n_rows, width)
    vch = vc.reshape(b, n_rows, width)
    pk, w1k, b1k, w2k = wk
    pv, w1v, b1v, w2v = wv
    ch_spec = pl.BlockSpec((None, n_rows, width), lambda bi: (bi, 0, 0))
    return pl.pallas_call(
        _compress_kernel,
        grid=(b,),
        in_specs=[ch_spec, ch_spec, _resident(pk.shape), _resident(pv.shape), _resident(w1k.shape),
                  _resident(w1v.shape), _resident(b1k.shape), _resident(b1v.shape), _resident(w2k.shape),
                  _resident(w2v.shape)],
        out_specs=[pl.BlockSpec((None, NSA_KV_GROUPS, n_rows, LANES), lambda bi: (bi, 0, 0, 0)),
                   pl.BlockSpec((None, NSA_KV_GROUPS, NSA_HEAD_DIM, n_rows), lambda bi: (bi, 0, 0, 0))],
        out_shape=[jax.ShapeDtypeStruct((b, NSA_KV_GROUPS, n_rows, LANES), BF16),
                   jax.ShapeDtypeStruct((b, NSA_KV_GROUPS, NSA_HEAD_DIM, n_rows), BF16)],
        compiler_params=_cparams(1),
        name="nsa_compress",
    )(kch, vch, pk, pv, w1k, w1v, b1k, b1v, w2k, w2v)


def _split_heads_rhs(q):
    lane = lax.broadcasted_iota(jnp.int32, q.shape, 1)
    zero = jnp.zeros_like(q)
    return jnp.concatenate([jnp.where(lane < LANES // 2, q, zero), jnp.where(lane >= LANES // 2, q, zero)], axis=0)


def _scores_t(k, rhs):
    return lax.dot_general(k, rhs, (((1,), (1,)), ((), ())), preferred_element_type=F32)


def _two(row):
    return jnp.concatenate([row, row], axis=1)


def _flash_update(s, vt_blk, m, acc_ref, idx):
    m_new = jnp.maximum(m, jnp.max(s, axis=0, keepdims=True))
    alpha = jnp.exp2(m - m_new)
    p = jnp.exp2(s - m_new).astype(BF16)
    acc_ref[idx] = alpha * acc_ref[idx] + jnp.dot(vt_blk, p, preferred_element_type=F32)
    return m_new


def _flash_update_block_bias(src_ref, idx, bias_rows, vt_blk, m, acc_ref):
    rows = src_ref.shape[1] // len(bias_rows)
    piece = lambda j: src_ref[idx, j * rows:(j + 1) * rows, :]
    m_new = m
    for j, b in enumerate(bias_rows):
        m_new = jnp.maximum(m_new, jnp.max(piece(j), axis=0, keepdims=True) + b)
    alpha = jnp.exp2(m - m_new)
    p = jnp.concatenate([jnp.exp2(piece(j) + (b - m_new)).astype(BF16) for j, b in enumerate(bias_rows)], axis=0)
    acc_ref[idx] = alpha * acc_ref[idx] + jnp.dot(vt_blk, p, preferred_element_type=F32)
    return m_new


def _normalized(acc, dv):
    return acc[:dv] * (1.0 / acc[dv:dv + 1])


def _nsa_attn_kernel(qc_ref, qr_ref, kcd_ref, vct_ref, ks_ref, kw_ref, vst_ref, vwt_ref, gt_ref, ov_ref,
                     o_ref, og_ref, acc_ref, selb_ref, sa_ref, sb_ref):
    qi = pl.program_id(1)
    tq = ATT_Q_TILE
    t0 = qi * tq
    n_pairs = qc_ref.shape[0]
    n_groups = kcd_ref.shape[0]
    group_of = lambda c: c // (n_pairs // n_groups)
    n_cmp = kcd_ref.shape[1]
    seq = ks_ref.shape[1]
    n_sel = seq // SLC_BLOCK
    tok1 = t0 + lax.broadcasted_iota(jnp.int32, (1, tq), 1)
    tok2 = _two(tok1)

    kb_sz = ATT_KEY_BLOCK

    def gate(j, c):
        return jnp.concatenate([gt_ref[j * NSA_HEADS + 2 * c:j * NSA_HEADS + 2 * c + 1, :],
                                gt_ref[j * NSA_HEADS + 2 * c + 1:j * NSA_HEADS + 2 * c + 2, :]], axis=1)

    rhs = [_split_heads_rhs(qr_ref[c]) for c in range(n_pairs)]

    def sel_scores(kb, dst_ref):
        k0 = pl.multiple_of(kb * kb_sz, kb_sz)
        for c in range(n_pairs):
            dst_ref[c] = _scores_t(ks_ref[group_of(c), pl.ds(k0, kb_sz), :], rhs[c])

    win_blocks = WINDOW // kb_sz + 1
    win_start = jnp.minimum(jnp.maximum(t0 - WINDOW, 0) // kb_sz, seq // kb_sz - win_blocks)
    w0 = pl.multiple_of(win_start * kb_sz, kb_sz)

    cmp_scores = [_scores_t(kcd_ref[group_of(c)], _split_heads_rhs(qc_ref[c])) for c in range(n_pairs)]
    win_scores = [_scores_t(kw_ref[group_of(c), pl.ds(w0, win_blocks * kb_sz), :], rhs[c]) for c in range(n_pairs)]

    cmp_end = lax.broadcasted_iota(jnp.int32, (n_cmp, 1), 0) * CMP_STRIDE + (CMP_BLOCK - 1)
    vis = cmp_end <= tok2
    any_vis = (tok2 >= CMP_BLOCK - 1).astype(F32)
    psum = [jnp.zeros((n_cmp, tq), F32) for _ in range(n_groups)]
    for c in range(n_pairs):
        s = jnp.where(vis, cmp_scores[c], NEG)
        m = jnp.max(s, axis=0, keepdims=True)
        e = jnp.exp2(s - m)
        p = e * (any_vis / jnp.sum(e, axis=0, keepdims=True))
        psum[group_of(c)] += p[:, :tq] + p[:, tq:]
        og_ref[c] = gate(0, c) * jnp.dot(vct_ref[group_of(c)], p.astype(BF16), preferred_element_type=F32)

    imp_raw = [jnp.dot(ov_ref[...], ps, precision=lax.Precision.HIGHEST, preferred_element_type=F32)
               for ps in psum]
    sel_scores(0, sa_ref)

    wrow = w0 + lax.broadcasted_iota(jnp.int32, (win_blocks * kb_sz, 1), 0)
    in_window = (wrow <= tok2) & (wrow > tok2 - WINDOW)
    for c in range(n_pairs):
        s = jnp.where(in_window, win_scores[c], NEG)
        m = jnp.max(s, axis=0, keepdims=True)
        p = jnp.exp2(s - m).astype(BF16)
        win = jnp.zeros((NSA_HEAD_DIM + SUM_ROWS, 2 * tq), F32)
        for j in range(win_blocks):
            win += jnp.dot(vwt_ref[group_of(c), win_start + j], p[j * kb_sz:(j + 1) * kb_sz, :],
                           preferred_element_type=F32)
        og_ref[c] += gate(2, c) * _normalized(win, NSA_HEAD_DIM)

    blk = lax.broadcasted_iota(jnp.int32, (n_sel, 1), 0)
    cur = tok1 // SLC_BLOCK
    forced = (blk == 0) | (blk == cur) | (blk == cur - 1)
    valid = blk <= cur
    rows8 = 8
    sub = lax.broadcasted_iota(jnp.int32, (rows8, 1), 0)
    for gi in range(n_groups):
        imp = jnp.where(valid, imp_raw[gi] + jnp.where(forced, FORCE_BONUS, 0.0), NEG)
        grp = [imp[rows8 * j:rows8 * (j + 1), :] for j in range(n_sel // rows8)]
        rank = [jnp.zeros((rows8, tq), jnp.int32) for _ in grp]
        for m in range(n_sel):
            row = imp[m:m + 1, :]
            for j in range(len(grp)):
                ge = lambda: (row >= grp[j]).astype(jnp.int32)
                gt = lambda: (row > grp[j]).astype(jnp.int32)
                if rows8 * j > m:
                    beats = ge()
                elif rows8 * j + rows8 - 1 <= m:
                    beats = gt()
                else:
                    beats = jnp.where(sub + rows8 * j > m, ge(), gt())
                rank[j] = rank[j] + beats
        rank = jnp.concatenate(rank, axis=0)
        selb_ref[gi] = jnp.where((rank < SLC_TOPK) & valid, 0.0, NEG)

    last = (t0 + tq - 1) // kb_sz
    blocks_per_kb = kb_sz // SLC_BLOCK
    krow = lax.broadcasted_iota(jnp.int32, (kb_sz, 1), 0)
    acc_ref[...] = jnp.zeros_like(acc_ref)

    def sel_bias_rows(kb):
        return [[_two(selb_ref[gi, pl.ds(kb * blocks_per_kb + j, 1), :]) for j in range(blocks_per_kb)]
                for gi in range(n_groups)]

    def sel_below_diagonal(kb, src_ref, st):
        rows = sel_bias_rows(kb)
        return tuple(_flash_update_block_bias(src_ref, c, rows[group_of(c)], vst_ref[group_of(c), kb], st[c], acc_ref)
                     for c in range(n_pairs))

    def sel_diagonal(kb, src_ref, st):
        causal = kb * kb_sz + krow <= tok2
        bias = [jnp.where(causal, jnp.concatenate([jnp.broadcast_to(r, (SLC_BLOCK, 2 * tq)) for r in rows], axis=0),
                          NEG) for rows in sel_bias_rows(kb)]
        return tuple(_flash_update(src_ref[c] + bias[group_of(c)], vst_ref[group_of(c), kb], st[c], acc_ref, c)
                     for c in range(n_pairs))

    def sel_two_blocks(i, st):
        kb = 2 * i
        sel_scores(kb + 1, sb_ref)
        st = sel_below_diagonal(kb, sa_ref, st)
        sel_scores(kb + 2, sa_ref)
        return sel_below_diagonal(kb + 1, sb_ref, st)

    def sel_tail_odd(st):
        sel_scores(last, sb_ref)
        st = sel_below_diagonal(last - 1, sa_ref, st)
        return sel_diagonal(last, sb_ref, st)

    init = tuple(jnp.full((1, 2 * tq), NEG, F32) for _ in range(n_pairs))
    st = lax.fori_loop(0, last // 2, sel_two_blocks, init)
    st = lax.cond(last % 2 == 0, lambda st: sel_diagonal(last, sa_ref, st), sel_tail_odd, st)

    for c in range(n_pairs):
        o_t = og_ref[c] + gate(1, c) * _normalized(acc_ref[c], NSA_HEAD_DIM)
        o_ref[c] = jnp.concatenate([o_t[:, :tq], o_t[:, tq:]], axis=0).T.astype(BF16)


def _cmp_to_slc_t(n_cmp_rows, n_sel):
    c0 = np.arange(n_cmp_rows)[None, :] * CMP_STRIDE
    s0 = np.arange(n_sel)[:, None] * SLC_BLOCK
    ov = np.minimum(c0 + CMP_BLOCK, s0 + SLC_BLOCK) - np.maximum(c0, s0)
    return (np.clip(ov, 0, None) / CMP_BLOCK).astype(np.float32)


def _nsa_attn(qc, qr, kcd, vct, ksd, kwd, vst, vwt, gt):
    b, pairs, s, _ = qc.shape
    groups = NSA_KV_GROUPS
    n_cmp = kcd.shape[2]
    nkb = s // ATT_KEY_BLOCK
    dh = NSA_HEAD_DIM
    tq = ATT_Q_TILE
    n_sel = s // SLC_BLOCK
    ov_t = jnp.asarray(_cmp_to_slc_t(n_cmp, n_sel))
    q_spec = pl.BlockSpec((None, pairs, tq, LANES), lambda bi, qi: (bi, 0, qi, 0))
    per_batch = lambda *shape: pl.BlockSpec((None,) + shape, lambda bi, qi: (bi,) + (0,) * len(shape))
    return pl.pallas_call(
        _nsa_attn_kernel,
        grid=(b, s // tq),
        in_specs=[q_spec, q_spec, per_batch(groups, n_cmp, LANES), per_batch(groups, dh, n_cmp),
                  per_batch(groups, s, LANES), per_batch(groups, s, LANES),
                  per_batch(groups, nkb, dh + SUM_ROWS, ATT_KEY_BLOCK),
                  per_batch(groups, nkb, dh + SUM_ROWS, ATT_KEY_BLOCK),
                  pl.BlockSpec((None, LANES, tq), lambda bi, qi: (bi, 0, qi)),
                  _resident(ov_t.shape)],
        out_specs=q_spec,
        out_shape=jax.ShapeDtypeStruct((b, pairs, s, LANES), BF16),
        scratch_shapes=[pltpu.VMEM((pairs, dh, 2 * tq), F32), pltpu.VMEM((pairs, dh + SUM_ROWS, 2 * tq), F32),
                        pltpu.VMEM((groups, n_sel, tq), F32),
                        pltpu.VMEM((pairs, ATT_KEY_BLOCK, 2 * tq), F32), pltpu.VMEM((pairs, ATT_KEY_BLOCK, 2 * tq), F32)],
        compiler_params=_cparams(2),
        name="nsa_attn",
    )(qc, qr, kcd, vct, ksd, kwd, vst, vwt, gt, ov_t)


def _out_ffn_kernel(o_ref, x_ref, wo_ref, g1_ref, b1_ref, wg_ref, wu_ref, wd_ref, g2_ref, b2_ref, y_ref, h_ref):
    o = jnp.concatenate([o_ref[j] for j in range(o_ref.shape[0])], axis=1)
    mix = jnp.dot(o, wo_ref[...], preferred_element_type=F32)
    x1 = _layer_norm_rows(DEEPNORM_ALPHA * x_ref[...] + mix, g1_ref[...], b1_ref[...])
    y_ref[...] = _ffn_ln_rows(x1, wg_ref, wu_ref, wd_ref, g2_ref, b2_ref, h_ref)


def _out_ffn(o, x, wo, g1, b1, wg, wu, wd, g2, b2):
    bsz, n_chunks, s, _ = o.shape
    d = x.shape[-1]
    d_ff = wg.shape[1]
    x_spec = pl.BlockSpec((None, ROW_TILE, d), lambda bi, i: (bi, i, 0))
    return pl.pallas_call(
        _out_ffn_kernel,
        grid=(bsz, s // ROW_TILE),
        in_specs=[pl.BlockSpec((None, n_chunks, ROW_TILE, LANES), lambda bi, i: (bi, 0, i, 0)), x_spec,
                  _resident((d, d)), _resident((1, d)), _resident((1, d)),
                  _resident((d, d_ff)), _resident((d, d_ff)), _resident((d_ff, d)),
                  _resident((1, d)), _resident((1, d))],
        out_specs=x_spec,
        out_shape=jax.ShapeDtypeStruct(x.shape, F32),
        scratch_shapes=[pltpu.VMEM((ROW_TILE, d_ff), BF16)],
        compiler_params=_cparams(2),
        name="mixer_out_ffn",
    )(o, x, wo, g1, b1, wg, wu, wd, g2, b2)


def _pair_maps_columns(w):
    d = w.shape[0]
    return w.reshape(d, 2, DIFF_HEADS, DIFF_HEAD_DIM).transpose(0, 2, 1, 3).reshape(d, -1)


def _kv_proj_kernel(x_ref, w_ref, c_ref, sl_ref, sh_ref, k_ref, vt_ref):
    proj = jnp.dot(x_ref[...].astype(BF16), w_ref[...], preferred_element_type=F32)
    c, sl, sh = c_ref[...], sl_ref[...], sh_ref[...]
    n_kb = x_ref.shape[0] // ATT_KEY_BLOCK
    ones = jnp.ones((SUM_ROWS, ATT_KEY_BLOCK), BF16)
    for h in range(DIFF_HEADS):
        k_ref[h] = _rope_chunk(proj[:, h * LANES:(h + 1) * LANES], c, sl, sh).astype(BF16)
        vt = proj[:, D_MODEL + h * DIFF_V_DIM:D_MODEL + (h + 1) * DIFF_V_DIM].T.astype(BF16)
        for kb in range(n_kb):
            vt_ref[h, kb, :DIFF_V_DIM, :] = vt[:, kb * ATT_KEY_BLOCK:(kb + 1) * ATT_KEY_BLOCK]
            vt_ref[h, kb, DIFF_V_DIM:, :] = ones


def _kv_proj(x, w, tables):
    b, s, d = x.shape
    nkb = s // ATT_KEY_BLOCK
    kb_tile = ROW_TILE // ATT_KEY_BLOCK
    tab_spec = pl.BlockSpec((ROW_TILE, LANES), lambda bi, i: (i, 0))
    return pl.pallas_call(
        _kv_proj_kernel,
        grid=(b, s // ROW_TILE),
        in_specs=[pl.BlockSpec((None, ROW_TILE, d), lambda bi, i: (bi, i, 0)), _resident(w.shape),
                  tab_spec, tab_spec, tab_spec],
        out_specs=[pl.BlockSpec((None, DIFF_HEADS, ROW_TILE, LANES), lambda bi, i: (bi, 0, i, 0)),
                   pl.BlockSpec((None, DIFF_HEADS, kb_tile, DIFF_V_DIM + SUM_ROWS, ATT_KEY_BLOCK),
                                lambda bi, i: (bi, 0, i, 0, 0))],
        out_shape=[jax.ShapeDtypeStruct((b, DIFF_HEADS, s, LANES), BF16),
                   jax.ShapeDtypeStruct((b, DIFF_HEADS, nkb, DIFF_V_DIM + SUM_ROWS, ATT_KEY_BLOCK), BF16)],
        compiler_params=_cparams(2),
        name="diff_kv_proj",
    )(x, w, *tables)


def _dq_proj_kernel(x_ref, wg_ref, wu_ref, wd_ref, g_ref, b_ref, w_ref, c_ref, sl_ref, sh_ref,
                    xo_ref, q_ref, h_ref):
    x = _ffn_ln_rows(x_ref[...], wg_ref, wu_ref, wd_ref, g_ref, b_ref, h_ref)
    xo_ref[...] = x
    proj = jnp.dot(x.astype(BF16), w_ref[...], preferred_element_type=F32)
    c, sl, sh = c_ref[...], sl_ref[...], sh_ref[...]
    for h in range(DIFF_HEADS):
        q = proj[:, h * LANES:(h + 1) * LANES] * (DIFF_HEAD_DIM ** -0.5 * LOG2E)
        q_ref[h] = _rope_chunk(q, c, sl, sh).astype(BF16)


def _dq_proj(x, ffn, w, tables):
    b, s, d = x.shape
    d_ff = ffn[0].shape[1]
    tab_spec = pl.BlockSpec((ROW_TILE, LANES), lambda bi, i: (i, 0))
    x_spec = pl.BlockSpec((None, ROW_TILE, d), lambda bi, i: (bi, i, 0))
    return pl.pallas_call(
        _dq_proj_kernel,
        grid=(b, s // ROW_TILE),
        in_specs=[x_spec, _resident((d, d_ff)), _resident((d, d_ff)), _resident((d_ff, d)),
                  _resident((1, d)), _resident((1, d)), _resident(w.shape), tab_spec, tab_spec, tab_spec],
        out_specs=[x_spec, pl.BlockSpec((None, DIFF_HEADS, ROW_TILE, LANES), lambda bi, i: (bi, 0, i, 0))],
        out_shape=[jax.ShapeDtypeStruct((b, s, d), F32), jax.ShapeDtypeStruct((b, DIFF_HEADS, s, LANES), BF16)],
        scratch_shapes=[pltpu.VMEM((ROW_TILE, d_ff), BF16)],
        compiler_params=_cparams(2),
        name="ffn_diff_q_proj",
    )(x, *ffn, w, *tables)


def _diff_attn_kernel(q_ref, k_ref, vt_ref, lam_ref, gain_ref, o_ref, acc_ref, sa_ref, sb_ref, *, lambda_init):
    qi = pl.program_id(2)
    n_heads = q_ref.shape[0]
    tq = DIFF_Q_TILE
    t0 = qi * tq
    kb_sz = ATT_KEY_BLOCK
    tok2 = _two(t0 + lax.broadcasted_iota(jnp.int32, (1, tq), 1))
    krow = lax.broadcasted_iota(jnp.int32, (kb_sz, 1), 0)
    rhs = [_split_heads_rhs(q_ref[h]) for h in range(n_heads)]
    last = (t0 + tq - 1) // kb_sz
    acc_ref[...] = jnp.zeros_like(acc_ref)

    def scores_into(kb, dst_ref):
        k0 = pl.multiple_of(kb * kb_sz, kb_sz)
        for h in range(n_heads):
            dst_ref[h] = _scores_t(k_ref[h, pl.ds(k0, kb_sz), :], rhs[h])

    def consume(kb, src_ref, st, maybe_diagonal):
        if maybe_diagonal:
            visible = kb * kb_sz + krow <= tok2
        out = []
        for h in range(n_heads):
            s = src_ref[h]
            if maybe_diagonal:
                s = jnp.where(visible, s, NEG)
            out.append(_flash_update(s, vt_ref[h, kb], st[h], acc_ref, h))
        return tuple(out)

    def two_blocks(i, st):
        kb = 2 * i
        scores_into(kb + 1, sb_ref)
        st = consume(kb, sa_ref, st, False)
        scores_into(kb + 2, sa_ref)
        return consume(kb + 1, sb_ref, st, False)

    def tail_odd(st):
        scores_into(last, sb_ref)
        st = consume(last - 1, sa_ref, st, False)
        return consume(last, sb_ref, st, True)

    init = tuple(jnp.full((1, 2 * tq), NEG, F32) for _ in range(n_heads))
    scores_into(0, sa_ref)
    st = lax.fori_loop(0, last // 2, two_blocks, init)
    st = lax.cond(last % 2 == 0, lambda st: consume(last, sa_ref, st, True), tail_odd, st)

    lp = lam_ref[...]
    lam = (jnp.exp(jnp.sum(lp[0:1] * lp[1:2], axis=1, keepdims=True))
           - jnp.exp(jnp.sum(lp[2:3] * lp[3:4], axis=1, keepdims=True)) + lambda_init)
    for h in range(n_heads):
        a = _normalized(acc_ref[h], DIFF_V_DIM)
        o = a[:, :tq] - lam * a[:, tq:]
        ms = jnp.mean(o * o, axis=0, keepdims=True)
        o = o * lax.rsqrt(ms + NORM_EPS) * gain_ref[...] * (1.0 - lambda_init)
        o_ref[h] = o.T.astype(BF16)


def _diff_attn(q, k, vt, lam_params, gain, lambda_init):
    b, h, s, _ = q.shape
    nkb = s // ATT_KEY_BLOCK
    tq = DIFF_Q_TILE
    hps = DIFF_HEADS_PER_STEP
    per_head = lambda *shape: pl.BlockSpec((None, hps) + shape, lambda bi, hi, qi: (bi, hi) + (0,) * len(shape))
    q_spec = pl.BlockSpec((None, hps, tq, LANES), lambda bi, hi, qi: (bi, hi, qi, 0))
    return pl.pallas_call(
        functools.partial(_diff_attn_kernel, lambda_init=lambda_init),
        grid=(b, h // hps, s // tq),
        in_specs=[q_spec, per_head(s, LANES), per_head(nkb, DIFF_V_DIM + SUM_ROWS, ATT_KEY_BLOCK),
                  _resident(lam_params.shape), _resident(gain.shape)],
        out_specs=q_spec,
        out_shape=jax.ShapeDtypeStruct((b, h, s, LANES), BF16),
        scratch_shapes=[pltpu.VMEM((hps, DIFF_V_DIM + SUM_ROWS, 2 * tq), F32),
                        pltpu.VMEM((hps, ATT_KEY_BLOCK, 2 * tq), F32), pltpu.VMEM((hps, ATT_KEY_BLOCK, 2 * tq), F32)],
        compiler_params=_cparams(3),
        name="diff_attn",
    )(q, k, vt, lam_params, gain)


def kernel(x, ffn1_w_gate, ffn1_w_up, ffn1_w_down, ffn2_w_gate, ffn2_w_up, ffn2_w_down, ln_g, ln_b, a_w_in,
           a_cmp_pos_k, a_cmp_w1_k, a_cmp_b1_k, a_cmp_w2_k, a_cmp_pos_v, a_cmp_w1_v, a_cmp_b1_v, a_cmp_w2_v,
           a_w_out, b_w_kv_shared, b_w_q, b_lambda_q1, b_lambda_k1, b_lambda_q2, b_lambda_k2, b_subln_g, b_w_out):
    bsz, seq, d = x.shape
    assert d == D_MODEL and seq % ROW_TILE == 0 and seq >= WINDOW + 2 * ATT_KEY_BLOCK
    tables = _rope_tables(seq)
    bf = lambda w: w.astype(BF16)
    row = lambda v: v[None, :]

    k12 = vt = None
    for layer in range(DEPTH):
        ffn1 = (bf(ffn1_w_gate[layer]), bf(ffn1_w_up[layer]), bf(ffn1_w_down[layer]),
                row(ln_g[layer, 0]), row(ln_b[layer, 0]))
        if layer < N_A_LAYERS:
            x, qc, qr, kc, vc, ksd, kwd, vst, vwt, gt = _nsa_proj(x, ffn1, _nsa_proj_weight(a_w_in[layer]), tables)
            wk = _compress_weights(a_cmp_pos_k[layer], a_cmp_w1_k[layer], a_cmp_b1_k[layer], a_cmp_w2_k[layer], True)
            wv = _compress_weights(a_cmp_pos_v[layer], a_cmp_w1_v[layer], a_cmp_b1_v[layer], a_cmp_w2_v[layer], False)
            kcd, vct = _compress(kc, vc, wk, wv)
            o = _nsa_attn(qc, qr, kcd, vct, ksd, kwd, vst, vwt, gt)
            w_out = a_w_out[layer]
        else:
            j = layer - N_A_LAYERS
            lambda_init = 0.8 - 0.6 * float(np.exp(-0.3 * layer))
            x, q = _dq_proj(x, ffn1, bf(_pair_maps_columns(b_w_q[j])), tables)
            lam_params = jnp.pad(jnp.stack([b_lambda_q1[j], b_lambda_k1[j], b_lambda_q2[j], b_lambda_k2[j]]),
                                 ((0, 4), (0, LANES - DIFF_HEAD_DIM)))
            gain = jnp.broadcast_to(b_subln_g[j][:, None], (DIFF_V_DIM, DIFF_Q_TILE))
            o = _diff_attn(q, k12, vt, lam_params, gain, lambda_init)
            w_out = b_w_out[j]
        x = _out_ffn(o, x, bf(w_out), row(ln_g[layer, 1]), row(ln_b[layer, 1]),
                     bf(ffn2_w_gate[layer]), bf(ffn2_w_up[layer]), bf(ffn2_w_down[layer]),
                     row(ln_g[layer, 2]), row(ln_b[layer, 2]))
        if layer == N_A_LAYERS - 1:
            w_kv = jnp.concatenate([_pair_maps_columns(b_w_kv_shared[:, :D_MODEL]), b_w_kv_shared[:, D_MODEL:]], axis=1)
            k12, vt = _kv_proj(x, bf(w_kv), tables)
    return x
```

```python
import functools

import numpy as np
import jax
import jax.numpy as jnp
from jax import lax
from jax.experimental import pallas as pl
from jax.experimental.pallas import tpu as pltpu

F32 = jnp.float32
BF16 = jnp.bfloat16

D_MODEL = 1024
DEPTH = 4
N_A_LAYERS = DEPTH // 2
NSA_HEADS = 16
NSA_HEAD_DIM = D_MODEL // NSA_HEADS
NSA_KV_GROUPS = 2
NSA_KV_WIDTH = NSA_KV_GROUPS * NSA_HEAD_DIM
NSA_GATE_WIDTH = 3 * NSA_HEADS
CMP_BLOCK = 32
CMP_STRIDE = 16
CMP_MLP_HIDDEN = 128
SLC_BLOCK = 64
SLC_TOPK = 16
WINDOW = 512
FORCE_BONUS = 1000.0
DIFF_HEADS = 8
DIFF_HEAD_DIM = D_MODEL // (2 * DIFF_HEADS)
DIFF_V_DIM = 2 * DIFF_HEAD_DIM
ROPE_THETA = 500000.0
ROPE_FRACTION = 4
D_FF = 2816
DEEPNORM_ALPHA = (2 * DEPTH) ** 0.25
NORM_EPS = 1e-5
NEG = -1e30

LANES = 128
ROW_TILE = 512
FFN_COL_CHUNK = 512
ATT_Q_TILE = 256
DIFF_Q_TILE = 256
ATT_KEY_BLOCK = 256
DIFF_HEADS_PER_STEP = 8
SUM_ROWS = 16
LOG2E = 1.4426950408889634
VMEM_LIMIT_BYTES = 56 * 1024 * 1024
N_CHUNKS = D_MODEL // LANES


def _cparams(n_axes):
    return pltpu.CompilerParams(dimension_semantics=("parallel",) * n_axes,
                                vmem_limit_bytes=VMEM_LIMIT_BYTES)


def _resident(shape):
    zeros = (0,) * len(shape)
    return pl.BlockSpec(shape, lambda *_: zeros, pipeline_mode=pl.Buffered(1))


def _resident_layer(shape, layer):
    zeros = (0,) * len(shape)
    return pl.BlockSpec((None,) + shape, lambda *_: (layer,) + zeros, pipeline_mode=pl.Buffered(1))


def _layer_norm_rows(r, g, b):
    mu = jnp.mean(r, axis=-1, keepdims=True)
    xc = r - mu
    var = jnp.mean(xc * xc, axis=-1, keepdims=True)
    return xc * lax.rsqrt(var + NORM_EPS) * g + b


def _ffn_ln_rows(x, wg_ref, wu_ref, wd_ref, g_ref, b_ref, h_ref):
    xb = x.astype(BF16)
    d_ff = wg_ref.shape[1]
    for c0 in range(0, d_ff, FFN_COL_CHUNK):
        c1 = min(c0 + FFN_COL_CHUNK, d_ff)
        gate = jnp.dot(xb, wg_ref[:, c0:c1], preferred_element_type=F32)
        up = jnp.dot(xb, wu_ref[:, c0:c1], preferred_element_type=F32)
        silu = gate * (1.0 / (1.0 + jnp.exp(-gate)))
        h_ref[:, c0:c1] = (silu * up).astype(BF16)
    y = jnp.dot(h_ref[...], wd_ref[...], preferred_element_type=F32)
    return _layer_norm_rows(DEEPNORM_ALPHA * x + 0.5 * y, g_ref[...], b_ref[...])


def _rope_tables(seq):
    r = NSA_HEAD_DIM // ROPE_FRACTION
    half = r // 2
    inv = ROPE_THETA ** (-jnp.arange(half, dtype=F32) * 2.0 / r)
    ang = jnp.arange(seq).astype(F32)[:, None] * inv[None, :]
    cos, sin = jnp.cos(ang), jnp.sin(ang)
    pad = jnp.zeros((seq, NSA_HEAD_DIM - r), F32)
    zero = jnp.zeros_like(sin)
    c_head = jnp.concatenate([cos, cos, pad + 1.0], axis=1)
    s_lo_head = jnp.concatenate([-sin, zero, pad], axis=1)
    s_hi_head = jnp.concatenate([zero, sin, pad], axis=1)
    rep = lambda t: jnp.concatenate([t, t], axis=1)
    return rep(c_head), rep(s_lo_head), rep(s_hi_head)


def _rope_chunk(x, c, s_lo, s_hi):
    half = NSA_HEAD_DIM // ROPE_FRACTION // 2
    return x * c + pltpu.roll(x, LANES - half, 1) * s_lo + pltpu.roll(x, half, 1) * s_hi


def _nsa_proj_kernel(x_ref, wg_ref, wu_ref, wd_ref, g_ref, b_ref, w_ref, c_ref, sl_ref, sh_ref,
                     xo_ref, qc_ref, qr_ref, kc_ref, vc_ref, ks_ref, kw_ref, vst_ref, vwt_ref, gt_ref, h_ref):
    x = _ffn_ln_rows(x_ref[...], wg_ref, wu_ref, wd_ref, g_ref, b_ref, h_ref)
    xo_ref[...] = x
    proj = jnp.dot(x.astype(BF16), w_ref[...], preferred_element_type=F32)
    c, sl, sh = c_ref[...], sl_ref[...], sh_ref[...]
    col = lambda j: proj[:, j * LANES:(j + 1) * LANES]
    for j in range(N_CHUNKS):
        q = col(j) * (NSA_HEAD_DIM ** -0.5 * LOG2E)
        qc_ref[j] = q.astype(BF16)
        qr_ref[j] = _rope_chunk(q, c, sl, sh).astype(BF16)
    base = N_CHUNKS
    kc_ref[...] = col(base)
    vc_ref[...] = col(base + 1)
    for g in range(NSA_KV_GROUPS):
        ks_ref[g] = _rope_chunk(col(base + 2 + g), c, sl, sh).astype(BF16)
        kw_ref[g] = _rope_chunk(col(base + 4 + g), c, sl, sh).astype(BF16)
    n_kb = x_ref.shape[0] // ATT_KEY_BLOCK
    ones = jnp.ones((SUM_ROWS, ATT_KEY_BLOCK), BF16)
    for src, dst in ((base + 6, vst_ref), (base + 7, vwt_ref)):
        vt = col(src).T.astype(BF16)
        for g in range(NSA_KV_GROUPS):
            for kb in range(n_kb):
                dst[g, kb, :NSA_HEAD_DIM, :] = vt[g * NSA_HEAD_DIM:(g + 1) * NSA_HEAD_DIM,
                                                  kb * ATT_KEY_BLOCK:(kb + 1) * ATT_KEY_BLOCK]
                dst[g, kb, NSA_HEAD_DIM:, :] = ones
    gates = col(base + 8)
    gt_ref[...] = (1.0 / (1.0 + jnp.exp(-gates))).T


def _nsa_proj_weight(w_in):
    d = D_MODEL
    kvw = NSA_KV_WIDTH
    q, kc, vc, ks, vs, kw, vw, gates = jnp.split(
        w_in, [int(c) for c in np.cumsum([d] + [kvw] * 6)], axis=1)
    dh = NSA_HEAD_DIM
    dup = lambda t: jnp.concatenate([t[:, :dh], t[:, :dh], t[:, dh:], t[:, dh:]], axis=1)
    gates = jnp.pad(gates, ((0, 0), (0, LANES - NSA_GATE_WIDTH)))
    return jnp.concatenate([q, kc, vc, dup(ks), dup(kw), vs, vw, gates], axis=1).astype(BF16)


def _nsa_proj(x, ffn, w, tables):
    b, s, d = x.shape
    wg, wu, wd, layer, ln_g, ln_b = ffn
    d_ff = wg.shape[-1]
    ncol = w.shape[1]
    nkb = s // ATT_KEY_BLOCK
    kb_tile = ROW_TILE // ATT_KEY_BLOCK
    dh = NSA_HEAD_DIM + SUM_ROWS
    tab_spec = pl.BlockSpec((ROW_TILE, LANES), lambda bi, i: (i, 0))
    chunk_spec = lambda n: pl.BlockSpec((None, n, ROW_TILE, LANES), lambda bi, i: (bi, 0, i, 0))
    row_spec = pl.BlockSpec((None, ROW_TILE, LANES), lambda bi, i: (bi, i, 0))
    vt_spec = pl.BlockSpec((None, NSA_KV_GROUPS, kb_tile, dh, ATT_KEY_BLOCK), lambda bi, i: (bi, 0, i, 0, 0))
    x_spec = pl.BlockSpec((None, ROW_TILE, d), lambda bi, i: (bi, i, 0))
    return pl.pallas_call(
        _nsa_proj_kernel,
        grid=(b, s // ROW_TILE),
        in_specs=[x_spec, _resident_layer((d, d_ff), layer), _resident_layer((d, d_ff), layer),
                  _resident_layer((d_ff, d), layer), _resident((1, d)), _resident((1, d)),
                  _resident((d, ncol)), tab_spec, tab_spec, tab_spec],
        out_specs=[x_spec, chunk_spec(N_CHUNKS), chunk_spec(N_CHUNKS), row_spec, row_spec,
                   chunk_spec(NSA_KV_GROUPS), chunk_spec(NSA_KV_GROUPS), vt_spec, vt_spec,
                   pl.BlockSpec((None, LANES, ROW_TILE), lambda bi, i: (bi, 0, i))],
        out_shape=[jax.ShapeDtypeStruct((b, s, d), F32),
                   jax.ShapeDtypeStruct((b, N_CHUNKS, s, LANES), BF16),
                   jax.ShapeDtypeStruct((b, N_CHUNKS, s, LANES), BF16),
                   jax.ShapeDtypeStruct((b, s, LANES), F32),
                   jax.ShapeDtypeStruct((b, s, LANES), F32),
                   jax.ShapeDtypeStruct((b, NSA_KV_GROUPS, s, LANES), BF16),
                   jax.ShapeDtypeStruct((b, NSA_KV_GROUPS, s, LANES), BF16),
                   jax.ShapeDtypeStruct((b, NSA_KV_GROUPS, nkb, dh, ATT_KEY_BLOCK), BF16),
                   jax.ShapeDtypeStruct((b, NSA_KV_GROUPS, nkb, dh, ATT_KEY_BLOCK), BF16),
                   jax.ShapeDtypeStruct((b, LANES, s), F32)],
        scratch_shapes=[pltpu.VMEM((ROW_TILE, d_ff), BF16)],
        compiler_params=_cparams(2),
        name="ffn_nsa_proj",
    )(x, wg, wu, wd, ln_g, ln_b, w, *tables)


def _gelu_tanh(x):
    return 0.5 * x * (1.0 + jnp.tanh(np.sqrt(2.0 / np.pi).astype(np.float32) * (x + 0.044715 * (x * x * x))))


def _compress_kernel(kch_ref, vch_ref, pk_ref, pv_ref, w1k_ref, w1v_ref, b1k_ref, b1v_ref, w2k_ref, w2v_ref,
                     kcd_ref, vct_ref):
    n_rows = kch_ref.shape[0]

    def mlp(ch_ref, pos_ref, w1_ref, b1_ref, w2_ref):
        ch = ch_ref[...]
        lo = jnp.dot((ch + pos_ref[0:1, :]).astype(BF16), w1_ref[0], preferred_element_type=F32)
        hi = jnp.dot((ch + pos_ref[1:2, :]).astype(BF16), w1_ref[1], preferred_element_type=F32)
        hid = lo + pltpu.roll(hi, n_rows - 1, 0) + b1_ref[...]
        return jnp.dot(_gelu_tanh(hid).astype(BF16), w2_ref[...], preferred_element_type=F32)

    kcd = mlp(kch_ref, pk_ref, w1k_ref, b1k_ref, w2k_ref)
    for g in range(NSA_KV_GROUPS):
        kcd_ref[g] = kcd[:, g * LANES:(g + 1) * LANES].astype(BF16)
    vct = mlp(vch_ref, pv_ref, w1v_ref, b1v_ref, w2v_ref).T
    for g in range(NSA_KV_GROUPS):
        vct_ref[g] = vct[g * NSA_HEAD_DIM:(g + 1) * NSA_HEAD_DIM, :].astype(BF16)


def _compress_weights(pos, w1, b1, w2, dup_out):
    g_n, dh, hid = NSA_KV_GROUPS, NSA_HEAD_DIM, CMP_MLP_HIDDEN
    w1r = w1.reshape(2, CMP_STRIDE, dh, hid)
    eye = jnp.eye(g_n, dtype=w1.dtype)
    w1x = jnp.einsum('hldm,gk->hlgdkm', w1r, eye).reshape(2, CMP_STRIDE * g_n * dh, g_n * hid)
    posx = jnp.broadcast_to(pos.reshape(2, CMP_STRIDE, 1, dh), (2, CMP_STRIDE, g_n, dh)).reshape(2, -1)
    b1x = jnp.tile(b1, g_n)[None, :]
    reps = 2 if dup_out else 1
    w2x = jnp.einsum('md,gk->gmkd', w2, eye)
    w2x = jnp.broadcast_to(w2x[:, :, :, None, :], (g_n, hid, g_n, reps, dh)).reshape(g_n * hid, g_n * reps * dh)
    return posx, w1x.astype(BF16), b1x, w2x.astype(BF16)


def _compress(kc, vc, wk, wv):
    b, s, _ = kc.shape
    n_rows = s // CMP_STRIDE
    width = CMP_STRIDE * LANES
    kch = kc.reshape(b, n_rows, width)
    vch = vc.reshape(b, n_rows, width)
    pk, w1k, b1k, w2k = wk
    pv, w1v, b1v, w2v = wv
    ch_spec = pl.BlockSpec((None, n_rows, width), lambda bi: (bi, 0, 0))
    return pl.pallas_call(
        _compress_kernel,
        grid=(b,),
        in_specs=[ch_spec, ch_spec, _resident(pk.shape), _resident(pv.shape), _resident(w1k.shape),
                  _resident(w1v.shape), _resident(b1k.shape), _resident(b1v.shape), _resident(w2k.shape),
                  _resident(w2v.shape)],
        out_specs=[pl.BlockSpec((None, NSA_KV_GROUPS, n_rows, LANES), lambda bi: (bi, 0, 0, 0)),
                   pl.BlockSpec((None, NSA_KV_GROUPS, NSA_HEAD_DIM, n_rows), lambda bi: (bi, 0, 0, 0))],
        out_shape=[jax.ShapeDtypeStruct((b, NSA_KV_GROUPS, n_rows, LANES), BF16),
                   jax.ShapeDtypeStruct((b, NSA_KV_GROUPS, NSA_HEAD_DIM, n_rows), BF16)],
        compiler_params=_cparams(1),
        name="nsa_compress",
    )(kch, vch, pk, pv, w1k, w1v, b1k, b1v, w2k, w2v)


def _split_heads_rhs(q):
    lane = lax.broadcasted_iota(jnp.int32, q.shape, 1)
    zero = jnp.zeros_like(q)
    return jnp.concatenate([jnp.where(lane < LANES // 2, q, zero), jnp.where(lane >= LANES // 2, q, zero)], axis=0)


def _scores_t(k, rhs):
    return lax.dot_general(k, rhs, (((1,), (1,)), ((), ())), preferred_element_type=F32)


def _two(row):
    return jnp.concatenate([row, row], axis=1)


def _flash_update(s, vt_blk, m, acc_ref, idx):
    m_new = jnp.maximum(m, jnp.max(s, axis=0, keepdims=True))
    alpha = jnp.exp2(m - m_new)
    p = jnp.exp2(s - m_new).astype(BF16)
    acc_ref[idx] = alpha * acc_ref[idx] + jnp.dot(vt_blk, p, preferred_element_type=F32)
    return m_new


def _flash_update_block_bias(src_ref, idx, bias_rows, vt_blk, m, acc_ref):
    rows = src_ref.shape[1] // len(bias_rows)
    piece = lambda j: src_ref[idx, j * rows:(j + 1) * rows, :]
    m_new = m
    for j, b in enumerate(bias_rows):
        m_new = jnp.maximum(m_new, jnp.max(piece(j), axis=0, keepdims=True) + b)
    alpha = jnp.exp2(m - m_new)
    p = jnp.concatenate([jnp.exp2(piece(j) + (b - m_new)).astype(BF16) for j, b in enumerate(bias_rows)], axis=0)
    acc_ref[idx] = alpha * acc_ref[idx] + jnp.dot(vt_blk, p, preferred_element_type=F32)
    return m_new


def _normalized(acc, dv):
    return acc[:dv] * (1.0 / acc[dv:dv + 1])


def _nsa_attn_kernel(qc_ref, qr_ref, kcd_ref, vct_ref, ks_ref, kw_ref, vst_ref, vwt_ref, gt_ref, ov_ref,
                     o_ref, og_ref, acc_ref, selb_ref, sa_ref, sb_ref):
    qi = pl.program_id(1)
    tq = ATT_Q_TILE
    t0 = qi * tq
    n_pairs = qc_ref.shape[0]
    n_groups = kcd_ref.shape[0]
    group_of = lambda c: c // (n_pairs // n_groups)
    n_cmp = kcd_ref.shape[1]
    seq = ks_ref.shape[1]
    n_sel = seq // SLC_BLOCK
    tok1 = t0 + lax.broadcasted_iota(jnp.int32, (1, tq), 1)
    tok2 = _two(tok1)

    kb_sz = ATT_KEY_BLOCK

    def gate(j, c):
        return jnp.concatenate([gt_ref[j * NSA_HEADS + 2 * c:j * NSA_HEADS + 2 * c + 1, :],
                                gt_ref[j * NSA_HEADS + 2 * c + 1:j * NSA_HEADS + 2 * c + 2, :]], axis=1)

    rhs = [_split_heads_rhs(qr_ref[c]) for c in range(n_pairs)]

    def sel_scores(kb, dst_ref):
        k0 = pl.multiple_of(kb * kb_sz, kb_sz)
        for c in range(n_pairs):
            dst_ref[c] = _scores_t(ks_ref[group_of(c), pl.ds(k0, kb_sz), :], rhs[c])

    win_blocks = WINDOW // kb_sz + 1
    win_start = jnp.minimum(jnp.maximum(t0 - WINDOW, 0) // kb_sz, seq // kb_sz - win_blocks)
    w0 = pl.multiple_of(win_start * kb_sz, kb_sz)

    cmp_scores = [_scores_t(kcd_ref[group_of(c)], _split_heads_rhs(qc_ref[c])) for c in range(n_pairs)]
    win_scores = [_scores_t(kw_ref[group_of(c), pl.ds(w0, win_blocks * kb_sz), :], rhs[c]) for c in range(n_pairs)]

    cmp_end = lax.broadcasted_iota(jnp.int32, (n_cmp, 1), 0) * CMP_STRIDE + (CMP_BLOCK - 1)
    vis = cmp_end <= tok2
    any_vis = (tok2 >= CMP_BLOCK - 1).astype(F32)
    psum = [jnp.zeros((n_cmp, tq), F32) for _ in range(n_groups)]
    for c in range(n_pairs):
        s = jnp.where(vis, cmp_scores[c], NEG)
        m = jnp.max(s, axis=0, keepdims=True)
        e = jnp.exp2(s - m)
        p = e * (any_vis / jnp.sum(e, axis=0, keepdims=True))
        psum[group_of(c)] += p[:, :tq] + p[:, tq:]
        og_ref[c] = gate(0, c) * jnp.dot(vct_ref[group_of(c)], p.astype(BF16), preferred_element_type=F32)

    imp_raw = [jnp.dot(ov_ref[...], ps, precision=lax.Precision.HIGHEST, preferred_element_type=F32)
               for ps in psum]
    sel_scores(0, sa_ref)

    wrow = w0 + lax.broadcasted_iota(jnp.int32, (win_blocks * kb_sz, 1), 0)
    in_window = (wrow <= tok2) & (wrow > tok2 - WINDOW)
    for c in range(n_pairs):
        s = jnp.where(in_window, win_scores[c], NEG)
        m = jnp.max(s, axis=0, keepdims=True)
        p = jnp.exp2(s - m).astype(BF16)
        win = jnp.zeros((NSA_HEAD_DIM + SUM_ROWS, 2 * tq), F32)
        for j in range(win_blocks):
            win += jnp.dot(vwt_ref[group_of(c), win_start + j], p[j * kb_sz:(j + 1) * kb_sz, :],
                           preferred_element_type=F32)
        og_ref[c] += gate(2, c) * _normalized(win, NSA_HEAD_DIM)

    blk = lax.broadcasted_iota(jnp.int32, (n_sel, 1), 0)
    cur = tok1 // SLC_BLOCK
    forced = (blk == 0) | (blk == cur) | (blk == cur - 1)
    valid = blk <= cur
    rows8 = 8
    sub = lax.broadcasted_iota(jnp.int32, (rows8, 1), 0)
    for gi in range(n_groups):
        imp = jnp.where(valid, imp_raw[gi] + jnp.where(forced, FORCE_BONUS, 0.0), NEG)
        grp = [imp[rows8 * j:rows8 * (j + 1), :] for j in range(n_sel // rows8)]
        rank = [jnp.zeros((rows8, tq), jnp.int32) for _ in grp]
        for m in range(n_sel):
            row = imp[m:m + 1, :]
            for j in range(len(grp)):
                ge = lambda: (row >= grp[j]).astype(jnp.int32)
                gt = lambda: (row > grp[j]).astype(jnp.int32)
                if rows8 * j > m:
                    beats = ge()
                elif rows8 * j + rows8 - 1 <= m:
                    beats = gt()
                else:
                    beats = jnp.where(sub + rows8 * j > m, ge(), gt())
                rank[j] = rank[j] + beats
        rank = jnp.concatenate(rank, axis=0)
        selb_ref[gi] = jnp.where((rank < SLC_TOPK) & valid, 0.0, NEG)

    last = (t0 + tq - 1) // kb_sz
    blocks_per_kb = kb_sz // SLC_BLOCK
    krow = lax.broadcasted_iota(jnp.int32, (kb_sz, 1), 0)
    acc_ref[...] = jnp.zeros_like(acc_ref)

    def sel_bias_rows(kb):
        return [[_two(selb_ref[gi, pl.ds(kb * blocks_per_kb + j, 1), :]) for j in range(blocks_per_kb)]
                for gi in range(n_groups)]

    def sel_below_diagonal(kb, src_ref, st):
        rows = sel_bias_rows(kb)
        return tuple(_flash_update_block_bias(src_ref, c, rows[group_of(c)], vst_ref[group_of(c), kb], st[c], acc_ref)
                     for c in range(n_pairs))

    def sel_diagonal(kb, src_ref, st):
        causal = kb * kb_sz + krow <= tok2
        bias = [jnp.where(causal, jnp.concatenate([jnp.broadcast_to(r, (SLC_BLOCK, 2 * tq)) for r in rows], axis=0),
                          NEG) for rows in sel_bias_rows(kb)]
        return tuple(_flash_update(src_ref[c] + bias[group_of(c)], vst_ref[group_of(c), kb], st[c], acc_ref, c)
                     for c in range(n_pairs))

    def sel_two_blocks(i, st):
        kb = 2 * i
        sel_scores(kb + 1, sb_ref)
        st = sel_below_diagonal(kb, sa_ref, st)
        sel_scores(kb + 2, sa_ref)
        return sel_below_diagonal(kb + 1, sb_ref, st)

    def sel_tail_odd(st):
        sel_scores(last, sb_ref)
        st = sel_below_diagonal(last - 1, sa_ref, st)
        return sel_diagonal(last, sb_ref, st)

    init = tuple(jnp.full((1, 2 * tq), NEG, F32) for _ in range(n_pairs))
    st = lax.fori_loop(0, last // 2, sel_two_blocks, init)
    st = lax.cond(last % 2 == 0, lambda st: sel_diagonal(last, sa_ref, st), sel_tail_odd, st)

    for c in range(n_pairs):
        o_t = og_ref[c] + gate(1, c) * _normalized(acc_ref[c], NSA_HEAD_DIM)
        o_ref[c] = jnp.concatenate([o_t[:, :tq], o_t[:, tq:]], axis=0).T.astype(BF16)


def _cmp_to_slc_t(n_cmp_rows, n_sel):
    c0 = np.arange(n_cmp_rows)[None, :] * CMP_STRIDE
    s0 = np.arange(n_sel)[:, None] * SLC_BLOCK
    ov = np.minimum(c0 + CMP_BLOCK, s0 + SLC_BLOCK) - np.maximum(c0, s0)
    return (np.clip(ov, 0, None) / CMP_BLOCK).astype(np.float32)


def _nsa_attn(qc, qr, kcd, vct, ksd, kwd, vst, vwt, gt):
    b, pairs, s, _ = qc.shape
    groups = NSA_KV_GROUPS
    n_cmp = kcd.shape[2]
    nkb = s // ATT_KEY_BLOCK
    dh = NSA_HEAD_DIM
    tq = ATT_Q_TILE
    n_sel = s // SLC_BLOCK
    ov_t = jnp.asarray(_cmp_to_slc_t(n_cmp, n_sel))
    q_spec = pl.BlockSpec((None, pairs, tq, LANES), lambda bi, qi: (bi, 0, qi, 0))
    per_batch = lambda *shape: pl.BlockSpec((None,) + shape, lambda bi, qi: (bi,) + (0,) * len(shape))
    return pl.pallas_call(
        _nsa_attn_kernel,
        grid=(b, s // tq),
        in_specs=[q_spec, q_spec, per_batch(groups, n_cmp, LANES), per_batch(groups, dh, n_cmp),
                  per_batch(groups, s, LANES), per_batch(groups, s, LANES),
                  per_batch(groups, nkb, dh + SUM_ROWS, ATT_KEY_BLOCK),
                  per_batch(groups, nkb, dh + SUM_ROWS, ATT_KEY_BLOCK),
                  pl.BlockSpec((None, LANES, tq), lambda bi, qi: (bi, 0, qi)),
                  _resident(ov_t.shape)],
        out_specs=q_spec,
        out_shape=jax.ShapeDtypeStruct((b, pairs, s, LANES), BF16),
        scratch_shapes=[pltpu.VMEM((pairs, dh, 2 * tq), F32), pltpu.VMEM((pairs, dh + SUM_ROWS, 2 * tq), F32),
                        pltpu.VMEM((groups, n_sel, tq), F32),
                        pltpu.VMEM((pairs, ATT_KEY_BLOCK, 2 * tq), F32), pltpu.VMEM((pairs, ATT_KEY_BLOCK, 2 * tq), F32)],
        compiler_params=_cparams(2),
        name="nsa_attn",
    )(qc, qr, kcd, vct, ksd, kwd, vst, vwt, gt, ov_t)


def _out_ffn_kernel(o_ref, x_ref, wo_ref, g1_ref, b1_ref, wg_ref, wu_ref, wd_ref, g2_ref, b2_ref, y_ref, h_ref):
    o = jnp.concatenate([o_ref[j] for j in range(o_ref.shape[0])], axis=1)
    mix = jnp.dot(o, wo_ref[...], preferred_element_type=F32)
    x1 = _layer_norm_rows(DEEPNORM_ALPHA * x_ref[...] + mix, g1_ref[...], b1_ref[...])
    y_ref[...] = _ffn_ln_rows(x1, wg_ref, wu_ref, wd_ref, g2_ref, b2_ref, h_ref)


def _out_ffn(o, x, wo, g1, b1, ffn):
    bsz, n_chunks, s, _ = o.shape
    d = x.shape[-1]
    wg, wu, wd, layer, g2, b2 = ffn
    d_ff = wg.shape[-1]
    x_spec = pl.BlockSpec((None, ROW_TILE, d), lambda bi, i: (bi, i, 0))
    return pl.pallas_call(
        _out_ffn_kernel,
        grid=(bsz, s // ROW_TILE),
        in_specs=[pl.BlockSpec((None, n_chunks, ROW_TILE, LANES), lambda bi, i: (bi, 0, i, 0)), x_spec,
                  _resident((d, d)), _resident((1, d)), _resident((1, d)),
                  _resident_layer((d, d_ff), layer), _resident_layer((d, d_ff), layer),
                  _resident_layer((d_ff, d), layer), _resident((1, d)), _resident((1, d))],
        out_specs=x_spec,
        out_shape=jax.ShapeDtypeStruct(x.shape, F32),
        scratch_shapes=[pltpu.VMEM((ROW_TILE, d_ff), BF16)],
        compiler_params=_cparams(2),
        name="mixer_out_ffn",
    )(o, x, wo, g1, b1, wg, wu, wd, g2, b2)


def _pair_maps_columns(w):
    d = w.shape[0]
    return w.reshape(d, 2, DIFF_HEADS, DIFF_HEAD_DIM).transpose(0, 2, 1, 3).reshape(d, -1)


def _kv_proj_kernel(x_ref, w_ref, c_ref, sl_ref, sh_ref, k_ref, vt_ref):
    proj = jnp.dot(x_ref[...].astype(BF16), w_ref[...], preferred_element_type=F32)
    c, sl, sh = c_ref[...], sl_ref[...], sh_ref[...]
    n_kb = x_ref.shape[0] // ATT_KEY_BLOCK
    ones = jnp.ones((SUM_ROWS, ATT_KEY_BLOCK), BF16)
    for h in range(DIFF_HEADS):
        k_ref[h] = _rope_chunk(proj[:, h * LANES:(h + 1) * LANES], c, sl, sh).astype(BF16)
        vt = proj[:, D_MODEL + h * DIFF_V_DIM:D_MODEL + (h + 1) * DIFF_V_DIM].T.astype(BF16)
        for kb in range(n_kb):
            vt_ref[h, kb, :DIFF_V_DIM, :] = vt[:, kb * ATT_KEY_BLOCK:(kb + 1) * ATT_KEY_BLOCK]
            vt_ref[h, kb, DIFF_V_DIM:, :] = ones


def _kv_proj(x, w, tables):
    b, s, d = x.shape
    nkb = s // ATT_KEY_BLOCK
    kb_tile = ROW_TILE // ATT_KEY_BLOCK
    tab_spec = pl.BlockSpec((ROW_TILE, LANES), lambda bi, i: (i, 0))
    return pl.pallas_call(
        _kv_proj_kernel,
        grid=(b, s // ROW_TILE),
        in_specs=[pl.BlockSpec((None, ROW_TILE, d), lambda bi, i: (bi, i, 0)), _resident(w.shape),
                  tab_spec, tab_spec, tab_spec],
        out_specs=[pl.BlockSpec((None, DIFF_HEADS, ROW_TILE, LANES), lambda bi, i: (bi, 0, i, 0)),
                   pl.BlockSpec((None, DIFF_HEADS, kb_tile, DIFF_V_DIM + SUM_ROWS, ATT_KEY_BLOCK),
                                lambda bi, i: (bi, 0, i, 0, 0))],
        out_shape=[jax.ShapeDtypeStruct((b, DIFF_HEADS, s, LANES), BF16),
                   jax.ShapeDtypeStruct((b, DIFF_HEADS, nkb, DIFF_V_DIM + SUM_ROWS, ATT_KEY_BLOCK), BF16)],
        compiler_params=_cparams(2),
        name="diff_kv_proj",
    )(x, w, *tables)


def _dq_proj_kernel(x_ref, wg_ref, wu_ref, wd_ref, g_ref, b_ref, w_ref, c_ref, sl_ref, sh_ref,
                    xo_ref, q_ref, h_ref):
    x = _ffn_ln_rows(x_ref[...], wg_ref, wu_ref, wd_ref, g_ref, b_ref, h_ref)
    xo_ref[...] = x
    proj = jnp.dot(x.astype(BF16), w_ref[...], preferred_element_type=F32)
    c, sl, sh = c_ref[...], sl_ref[...], sh_ref[...]
    for h in range(DIFF_HEADS):
        q = proj[:, h * LANES:(h + 1) * LANES] * (DIFF_HEAD_DIM ** -0.5 * LOG2E)
        q_ref[h] = _rope_chunk(q, c, sl, sh).astype(BF16)


def _dq_proj(x, ffn, w, tables):
    b, s, d = x.shape
    wg, wu, wd, layer, ln_g, ln_b = ffn
    d_ff = wg.shape[-1]
    tab_spec = pl.BlockSpec((ROW_TILE, LANES), lambda bi, i: (i, 0))
    x_spec = pl.BlockSpec((None, ROW_TILE, d), lambda bi, i: (bi, i, 0))
    return pl.pallas_call(
        _dq_proj_kernel,
        grid=(b, s // ROW_TILE),
        in_specs=[x_spec, _resident_layer((d, d_ff), layer), _resident_layer((d, d_ff), layer),
                  _resident_layer((d_ff, d), layer),
                  _resident((1, d)), _resident((1, d)), _resident(w.shape), tab_spec, tab_spec, tab_spec],
        out_specs=[x_spec, pl.BlockSpec((None, DIFF_HEADS, ROW_TILE, LANES), lambda bi, i: (bi, 0, i, 0))],
        out_shape=[jax.ShapeDtypeStruct((b, s, d), F32), jax.ShapeDtypeStruct((b, DIFF_HEADS, s, LANES), BF16)],
        scratch_shapes=[pltpu.VMEM((ROW_TILE, d_ff), BF16)],
        compiler_params=_cparams(2),
        name="ffn_diff_q_proj",
    )(x, wg, wu, wd, ln_g, ln_b, w, *tables)


def _diff_attn_kernel(q_ref, k_ref, vt_ref, lam_ref, gain_ref, o_ref, acc_ref, sa_ref, sb_ref, *, lambda_init):
    qi = pl.program_id(2)
    n_heads = q_ref.shape[0]
    tq = DIFF_Q_TILE
    t0 = qi * tq
    kb_sz = ATT_KEY_BLOCK
    tok2 = _two(t0 + lax.broadcasted_iota(jnp.int32, (1, tq), 1))
    krow = lax.broadcasted_iota(jnp.int32, (kb_sz, 1), 0)
    rhs = [_split_heads_rhs(q_ref[h]) for h in range(n_heads)]
    last = (t0 + tq - 1) // kb_sz
    acc_ref[...] = jnp.zeros_like(acc_ref)

    def scores_into(kb, dst_ref):
        k0 = pl.multiple_of(kb * kb_sz, kb_sz)
        for h in range(n_heads):
            dst_ref[h] = _scores_t(k_ref[h, pl.ds(k0, kb_sz), :], rhs[h])

    def consume(kb, src_ref, st, maybe_diagonal):
        if maybe_diagonal:
            visible = kb * kb_sz + krow <= tok2
        out = []
        for h in range(n_heads):
            s = src_ref[h]
            if maybe_diagonal:
                s = jnp.where(visible, s, NEG)
            out.append(_flash_update(s, vt_ref[h, kb], st[h], acc_ref, h))
        return tuple(out)

    def two_blocks(i, st):
        kb = 2 * i
        scores_into(kb + 1, sb_ref)
        st = consume(kb, sa_ref, st, False)
        scores_into(kb + 2, sa_ref)
        return consume(kb + 1, sb_ref, st, False)

    def tail_odd(st):
        scores_into(last, sb_ref)
        st = consume(last - 1, sa_ref, st, False)
        return consume(last, sb_ref, st, True)

    init = tuple(jnp.full((1, 2 * tq), NEG, F32) for _ in range(n_heads))
    scores_into(0, sa_ref)
    st = lax.fori_loop(0, last // 2, two_blocks, init)
    st = lax.cond(last % 2 == 0, lambda st: consume(last, sa_ref, st, True), tail_odd, st)

    lp = lam_ref[...]
    lam = (jnp.exp(jnp.sum(lp[0:1] * lp[1:2], axis=1, keepdims=True))
           - jnp.exp(jnp.sum(lp[2:3] * lp[3:4], axis=1, keepdims=True)) + lambda_init)
    for h in range(n_heads):
        a = _normalized(acc_ref[h], DIFF_V_DIM)
        o = a[:, :tq] - lam * a[:, tq:]
        ms = jnp.mean(o * o, axis=0, keepdims=True)
        o = o * lax.rsqrt(ms + NORM_EPS) * gain_ref[...] * (1.0 - lambda_init)
        o_ref[h] = o.T.astype(BF16)


def _diff_attn(q, k, vt, lam_params, gain, lambda_init):
    b, h, s, _ = q.shape
    nkb = s // ATT_KEY_BLOCK
    tq = DIFF_Q_TILE
    hps = DIFF_HEADS_PER_STEP
    per_head = lambda *shape: pl.BlockSpec((None, hps) + shape, lambda bi, hi, qi: (bi, hi) + (0,) * len(shape))
    q_spec = pl.BlockSpec((None, hps, tq, LANES), lambda bi, hi, qi: (bi, hi, qi, 0))
    return pl.pallas_call(
        functools.partial(_diff_attn_kernel, lambda_init=lambda_init),
        grid=(b, h // hps, s // tq),
        in_specs=[q_spec, per_head(s, LANES), per_head(nkb, DIFF_V_DIM + SUM_ROWS, ATT_KEY_BLOCK),
                  _resident(lam_params.shape), _resident(gain.shape)],
        out_specs=q_spec,
        out_shape=jax.ShapeDtypeStruct((b, h, s, LANES), BF16),
        scratch_shapes=[pltpu.VMEM((hps, DIFF_V_DIM + SUM_ROWS, 2 * tq), F32),
                        pltpu.VMEM((hps, ATT_KEY_BLOCK, 2 * tq), F32), pltpu.VMEM((hps, ATT_KEY_BLOCK, 2 * tq), F32)],
        compiler_params=_cparams(3),
        name="diff_attn",
    )(q, k, vt, lam_params, gain)


def kernel(x, ffn1_w_gate, ffn1_w_up, ffn1_w_down, ffn2_w_gate, ffn2_w_up, ffn2_w_down, ln_g, ln_b, a_w_in,
           a_cmp_pos_k, a_cmp_w1_k, a_cmp_b1_k, a_cmp_w2_k, a_cmp_pos_v, a_cmp_w1_v, a_cmp_b1_v, a_cmp_w2_v,
           a_w_out, b_w_kv_shared, b_w_q, b_lambda_q1, b_lambda_k1, b_lambda_q2, b_lambda_k2, b_subln_g, b_w_out):
    bsz, seq, d = x.shape
    assert d == D_MODEL and seq % ROW_TILE == 0 and seq >= WINDOW + 2 * ATT_KEY_BLOCK
    tables = _rope_tables(seq)
    bf = lambda w: w.astype(BF16)
    row = lambda v: v[None, :]

    w1 = (bf(ffn1_w_gate), bf(ffn1_w_up), bf(ffn1_w_down))
    w2 = (bf(ffn2_w_gate), bf(ffn2_w_up), bf(ffn2_w_down))
    k12 = vt = None
    for layer in range(DEPTH):
        ffn1 = w1 + (layer, row(ln_g[layer, 0]), row(ln_b[layer, 0]))
        if layer < N_A_LAYERS:
            x, qc, qr, kc, vc, ksd, kwd, vst, vwt, gt = _nsa_proj(x, ffn1, _nsa_proj_weight(a_w_in[layer]), tables)
            wk = _compress_weights(a_cmp_pos_k[layer], a_cmp_w1_k[layer], a_cmp_b1_k[layer], a_cmp_w2_k[layer], True)
            wv = _compress_weights(a_cmp_pos_v[layer], a_cmp_w1_v[layer], a_cmp_b1_v[layer], a_cmp_w2_v[layer], False)
            kcd, vct = _compress(kc, vc, wk, wv)
            o = _nsa_attn(qc, qr, kcd, vct, ksd, kwd, vst, vwt, gt)
            w_out = a_w_out[layer]
        else:
            j = layer - N_A_LAYERS
            lambda_init = 0.8 - 0.6 * float(np.exp(-0.3 * layer))
            x, q = _dq_proj(x, ffn1, bf(_pair_maps_columns(b_w_q[j])), tables)
            lam_params = jnp.pad(jnp.stack([b_lambda_q1[j], b_lambda_k1[j], b_lambda_q2[j], b_lambda_k2[j]]),
                                 ((0, 4), (0, LANES - DIFF_HEAD_DIM)))
            gain = jnp.broadcast_to(b_subln_g[j][:, None], (DIFF_V_DIM, DIFF_Q_TILE))
            o = _diff_attn(q, k12, vt, lam_params, gain, lambda_init)
            w_out = b_w_out[j]
        x = _out_ffn(o, x, bf(w_out), row(ln_g[layer, 1]), row(ln_b[layer, 1]),
                     w2 + (layer, row(ln_g[layer, 2]), row(ln_b[layer, 2])))
        if layer == N_A_LAYERS - 1:
            w_kv = jnp.concatenate([_pair_maps_columns(b_w_kv_shared[:, :D_MODEL]), b_w_kv_shared[:, D_MODEL:]], axis=1)
            k12, vt = _kv_proj(x, bf(w_kv), tables)
    return x
```

```python
import functools

import numpy as np
import jax
import jax.numpy as jnp
from jax import lax
from jax.experimental import pallas as pl
from jax.experimental.pallas import tpu as pltpu

F32 = jnp.float32
BF16 = jnp.bfloat16

D_MODEL = 1024
DEPTH = 4
N_A_LAYERS = DEPTH // 2
NSA_HEADS = 16
NSA_HEAD_DIM = D_MODEL // NSA_HEADS
NSA_KV_GROUPS = 2
NSA_KV_WIDTH = NSA_KV_GROUPS * NSA_HEAD_DIM
NSA_GATE_WIDTH = 3 * NSA_HEADS
CMP_BLOCK = 32
CMP_STRIDE = 16
CMP_MLP_HIDDEN = 128
SLC_BLOCK = 64
SLC_TOPK = 16
WINDOW = 512
FORCE_BONUS = 1000.0
DIFF_HEADS = 8
DIFF_HEAD_DIM = D_MODEL // (2 * DIFF_HEADS)
DIFF_V_DIM = 2 * DIFF_HEAD_DIM
ROPE_THETA = 500000.0
ROPE_FRACTION = 4
D_FF = 2816
DEEPNORM_ALPHA = (2 * DEPTH) ** 0.25
NORM_EPS = 1e-5
NEG = -1e30

LANES = 128
ROW_TILE = 512
FFN_COL_CHUNK = 512
ATT_Q_TILE = 256
DIFF_Q_TILE = 256
ATT_KEY_BLOCK = 256
DIFF_HEADS_PER_STEP = 8
SUM_ROWS = 16
LOG2E = 1.4426950408889634
VMEM_LIMIT_BYTES = 56 * 1024 * 1024
N_CHUNKS = D_MODEL // LANES


def _cparams(n_axes):
    return pltpu.CompilerParams(dimension_semantics=("parallel",) * n_axes,
                                vmem_limit_bytes=VMEM_LIMIT_BYTES)


def _resident(shape):
    zeros = (0,) * len(shape)
    return pl.BlockSpec(shape, lambda *_: zeros, pipeline_mode=pl.Buffered(1))


def _resident_layer(shape, layer):
    zeros = (0,) * len(shape)
    return pl.BlockSpec((None,) + shape, lambda *_: (layer,) + zeros, pipeline_mode=pl.Buffered(1))


def _layer_norm_rows(r, g, b):
    mu = jnp.mean(r, axis=-1, keepdims=True)
    xc = r - mu
    var = jnp.mean(xc * xc, axis=-1, keepdims=True)
    return xc * lax.rsqrt(var + NORM_EPS) * g + b


def _ffn_ln_rows(x, wg_ref, wu_ref, wd_ref, g_ref, b_ref, h_ref):
    xb = x.astype(BF16)
    d_ff = wg_ref.shape[1]
    for c0 in range(0, d_ff, FFN_COL_CHUNK):
        c1 = min(c0 + FFN_COL_CHUNK, d_ff)
        gate = jnp.dot(xb, wg_ref[:, c0:c1], preferred_element_type=F32)
        up = jnp.dot(xb, wu_ref[:, c0:c1], preferred_element_type=F32)
        silu = gate * (1.0 / (1.0 + jnp.exp(-gate)))
        h_ref[:, c0:c1] = (silu * up).astype(BF16)
    y = jnp.dot(h_ref[...], wd_ref[...], preferred_element_type=F32)
    return _layer_norm_rows(DEEPNORM_ALPHA * x + 0.5 * y, g_ref[...], b_ref[...])


def _rope_tables(seq):
    r = NSA_HEAD_DIM // ROPE_FRACTION
    half = r // 2
    inv = ROPE_THETA ** (-jnp.arange(half, dtype=F32) * 2.0 / r)
    ang = jnp.arange(seq).astype(F32)[:, None] * inv[None, :]
    cos, sin = jnp.cos(ang), jnp.sin(ang)
    pad = jnp.zeros((seq, NSA_HEAD_DIM - r), F32)
    zero = jnp.zeros_like(sin)
    c_head = jnp.concatenate([cos, cos, pad + 1.0], axis=1)
    s_lo_head = jnp.concatenate([-sin, zero, pad], axis=1)
    s_hi_head = jnp.concatenate([zero, sin, pad], axis=1)
    rep = lambda t: jnp.concatenate([t, t], axis=1)
    return rep(c_head), rep(s_lo_head), rep(s_hi_head)


def _rope_chunk(x, c, s_lo, s_hi):
    half = NSA_HEAD_DIM // ROPE_FRACTION // 2
    return x * c + pltpu.roll(x, LANES - half, 1) * s_lo + pltpu.roll(x, half, 1) * s_hi


def _nsa_proj_kernel(x_ref, wg_ref, wu_ref, wd_ref, g_ref, b_ref, w_ref, c_ref, sl_ref, sh_ref,
                     xo_ref, qc_ref, qr_ref, kc_ref, vc_ref, ks_ref, kw_ref, vst_ref, vwt_ref, gt_ref, h_ref):
    x = _ffn_ln_rows(x_ref[...], wg_ref, wu_ref, wd_ref, g_ref, b_ref, h_ref)
    xo_ref[...] = x
    proj = jnp.dot(x.astype(BF16), w_ref[...], preferred_element_type=F32)
    c, sl, sh = c_ref[...], sl_ref[...], sh_ref[...]
    col = lambda j: proj[:, j * LANES:(j + 1) * LANES]
    for j in range(N_CHUNKS):
        q = col(j) * (NSA_HEAD_DIM ** -0.5 * LOG2E)
        qc_ref[j] = q.astype(BF16)
        qr_ref[j] = _rope_chunk(q, c, sl, sh).astype(BF16)
    base = N_CHUNKS
    kc_ref[...] = col(base)
    vc_ref[...] = col(base + 1)
    for g in range(NSA_KV_GROUPS):
        ks_ref[g] = _rope_chunk(col(base + 2 + g), c, sl, sh).astype(BF16)
        kw_ref[g] = _rope_chunk(col(base + 4 + g), c, sl, sh).astype(BF16)
    n_kb = x_ref.shape[0] // ATT_KEY_BLOCK
    ones = jnp.ones((SUM_ROWS, ATT_KEY_BLOCK), BF16)
    for src, dst in ((base + 6, vst_ref), (base + 7, vwt_ref)):
        vt = col(src).T.astype(BF16)
        for g in range(NSA_KV_GROUPS):
            for kb in range(n_kb):
                dst[g, kb, :NSA_HEAD_DIM, :] = vt[g * NSA_HEAD_DIM:(g + 1) * NSA_HEAD_DIM,
                                                  kb * ATT_KEY_BLOCK:(kb + 1) * ATT_KEY_BLOCK]
                dst[g, kb, NSA_HEAD_DIM:, :] = ones
    gates = col(base + 8)
    gt_ref[...] = (1.0 / (1.0 + jnp.exp(-gates))).T


def _nsa_proj_weight(w_in):
    d = D_MODEL
    kvw = NSA_KV_WIDTH
    q, kc, vc, ks, vs, kw, vw, gates = jnp.split(
        w_in, [int(c) for c in np.cumsum([d] + [kvw] * 6)], axis=1)
    dh = NSA_HEAD_DIM
    dup = lambda t: jnp.concatenate([t[:, :dh], t[:, :dh], t[:, dh:], t[:, dh:]], axis=1)
    gates = jnp.pad(gates, ((0, 0), (0, LANES - NSA_GATE_WIDTH)))
    return jnp.concatenate([q, kc, vc, dup(ks), dup(kw), vs, vw, gates], axis=1).astype(BF16)


def _nsa_proj(x, ffn, w, tables):
    b, s, d = x.shape
    wg, wu, wd, layer, ln_g, ln_b = ffn
    d_ff = wg.shape[-1]
    ncol = w.shape[1]
    nkb = s // ATT_KEY_BLOCK
    kb_tile = ROW_TILE // ATT_KEY_BLOCK
    dh = NSA_HEAD_DIM + SUM_ROWS
    tab_spec = pl.BlockSpec((ROW_TILE, LANES), lambda bi, i: (i, 0))
    chunk_spec = lambda n: pl.BlockSpec((None, n, ROW_TILE, LANES), lambda bi, i: (bi, 0, i, 0))
    row_spec = pl.BlockSpec((None, ROW_TILE, LANES), lambda bi, i: (bi, i, 0))
    vt_spec = pl.BlockSpec((None, NSA_KV_GROUPS, kb_tile, dh, ATT_KEY_BLOCK), lambda bi, i: (bi, 0, i, 0, 0))
    x_spec = pl.BlockSpec((None, ROW_TILE, d), lambda bi, i: (bi, i, 0))
    return pl.pallas_call(
        _nsa_proj_kernel,
        grid=(b, s // ROW_TILE),
        in_specs=[x_spec, _resident_layer((d, d_ff), layer), _resident_layer((d, d_ff), layer),
                  _resident_layer((d_ff, d), layer), _resident((1, d)), _resident((1, d)),
                  _resident((d, ncol)), tab_spec, tab_spec, tab_spec],
        out_specs=[x_spec, chunk_spec(N_CHUNKS), chunk_spec(N_CHUNKS), row_spec, row_spec,
                   chunk_spec(NSA_KV_GROUPS), chunk_spec(NSA_KV_GROUPS), vt_spec, vt_spec,
                   pl.BlockSpec((None, LANES, ROW_TILE), lambda bi, i: (bi, 0, i))],
        out_shape=[jax.ShapeDtypeStruct((b, s, d), F32),
                   jax.ShapeDtypeStruct((b, N_CHUNKS, s, LANES), BF16),
                   jax.ShapeDtypeStruct((b, N_CHUNKS, s, LANES), BF16),
                   jax.ShapeDtypeStruct((b, s, LANES), F32),
                   jax.ShapeDtypeStruct((b, s, LANES), F32),
                   jax.ShapeDtypeStruct((b, NSA_KV_GROUPS, s, LANES), BF16),
                   jax.ShapeDtypeStruct((b, NSA_KV_GROUPS, s, LANES), BF16),
                   jax.ShapeDtypeStruct((b, NSA_KV_GROUPS, nkb, dh, ATT_KEY_BLOCK), BF16),
                   jax.ShapeDtypeStruct((b, NSA_KV_GROUPS, nkb, dh, ATT_KEY_BLOCK), BF16),
                   jax.ShapeDtypeStruct((b, LANES, s), F32)],
        scratch_shapes=[pltpu.VMEM((ROW_TILE, d_ff), BF16)],
        compiler_params=_cparams(2),
        name="ffn_nsa_proj",
    )(x, wg, wu, wd, ln_g, ln_b, w, *tables)


def _gelu_tanh(x):
    return 0.5 * x * (1.0 + jnp.tanh(np.sqrt(2.0 / np.pi).astype(np.float32) * (x + 0.044715 * (x * x * x))))


def _compress_kernel(kch_ref, vch_ref, pk_ref, pv_ref, w1k_ref, w1v_ref, b1k_ref, b1v_ref, w2k_ref, w2v_ref,
                     kcd_ref, vct_ref):
    n_rows = kch_ref.shape[0] // CMP_STRIDE

    def mlp(ch_ref, pos_ref, w1_ref, b1_ref, w2_ref):
        ch = jnp.concatenate([ch_ref[pl.ds(l, n_rows, stride=CMP_STRIDE), :] for l in range(CMP_STRIDE)], axis=1)
        lo = jnp.dot((ch + pos_ref[0:1, :]).astype(BF16), w1_ref[0], preferred_element_type=F32)
        hi = jnp.dot((ch + pos_ref[1:2, :]).astype(BF16), w1_ref[1], preferred_element_type=F32)
        hid = lo + pltpu.roll(hi, n_rows - 1, 0) + b1_ref[...]
        return jnp.dot(_gelu_tanh(hid).astype(BF16), w2_ref[...], preferred_element_type=F32)

    kcd = mlp(kch_ref, pk_ref, w1k_ref, b1k_ref, w2k_ref)
    for g in range(NSA_KV_GROUPS):
        kcd_ref[g] = kcd[:, g * LANES:(g + 1) * LANES].astype(BF16)
    vct = mlp(vch_ref, pv_ref, w1v_ref, b1v_ref, w2v_ref).T
    for g in range(NSA_KV_GROUPS):
        vct_ref[g] = vct[g * NSA_HEAD_DIM:(g + 1) * NSA_HEAD_DIM, :].astype(BF16)


def _compress_weights(pos, w1, b1, w2, dup_out):
    g_n, dh, hid = NSA_KV_GROUPS, NSA_HEAD_DIM, CMP_MLP_HIDDEN
    w1r = w1.reshape(2, CMP_STRIDE, dh, hid)
    eye = jnp.eye(g_n, dtype=w1.dtype)
    w1x = jnp.einsum('hldm,gk->hlgdkm', w1r, eye).reshape(2, CMP_STRIDE * g_n * dh, g_n * hid)
    posx = jnp.broadcast_to(pos.reshape(2, CMP_STRIDE, 1, dh), (2, CMP_STRIDE, g_n, dh)).reshape(2, -1)
    b1x = jnp.tile(b1, g_n)[None, :]
    reps = 2 if dup_out else 1
    w2x = jnp.einsum('md,gk->gmkd', w2, eye)
    w2x = jnp.broadcast_to(w2x[:, :, :, None, :], (g_n, hid, g_n, reps, dh)).reshape(g_n * hid, g_n * reps * dh)
    return posx, w1x.astype(BF16), b1x, w2x.astype(BF16)


def _compress(kc, vc, wk, wv):
    b, s, _ = kc.shape
    n_rows = s // CMP_STRIDE
    kch, vch = kc, vc
    pk, w1k, b1k, w2k = wk
    pv, w1v, b1v, w2v = wv
    ch_spec = pl.BlockSpec((None, s, LANES), lambda bi: (bi, 0, 0))
    return pl.pallas_call(
        _compress_kernel,
        grid=(b,),
        in_specs=[ch_spec, ch_spec, _resident(pk.shape), _resident(pv.shape), _resident(w1k.shape),
                  _resident(w1v.shape), _resident(b1k.shape), _resident(b1v.shape), _resident(w2k.shape),
                  _resident(w2v.shape)],
        out_specs=[pl.BlockSpec((None, NSA_KV_GROUPS, n_rows, LANES), lambda bi: (bi, 0, 0, 0)),
                   pl.BlockSpec((None, NSA_KV_GROUPS, NSA_HEAD_DIM, n_rows), lambda bi: (bi, 0, 0, 0))],
        out_shape=[jax.ShapeDtypeStruct((b, NSA_KV_GROUPS, n_rows, LANES), BF16),
                   jax.ShapeDtypeStruct((b, NSA_KV_GROUPS, NSA_HEAD_DIM, n_rows), BF16)],
        compiler_params=_cparams(1),
        name="nsa_compress",
    )(kch, vch, pk, pv, w1k, w1v, b1k, b1v, w2k, w2v)


def _split_heads_rhs(q):
    lane = lax.broadcasted_iota(jnp.int32, q.shape, 1)
    zero = jnp.zeros_like(q)
    return jnp.concatenate([jnp.where(lane < LANES // 2, q, zero), jnp.where(lane >= LANES // 2, q, zero)], axis=0)


def _scores_t(k, rhs):
    return lax.dot_general(k, rhs, (((1,), (1,)), ((), ())), preferred_element_type=F32)


def _two(row):
    return jnp.concatenate([row, row], axis=1)


def _flash_update(s, vt_blk, m, acc_ref, idx):
    m_new = jnp.maximum(m, jnp.max(s, axis=0, keepdims=True))
    alpha = jnp.exp2(m - m_new)
    p = jnp.exp2(s - m_new).astype(BF16)
    acc_ref[idx] = alpha * acc_ref[idx] + jnp.dot(vt_blk, p, preferred_element_type=F32)
    return m_new


def _flash_update_block_bias(src_ref, idx, bias_rows, vt_blk, m, acc_ref):
    rows = src_ref.shape[1] // len(bias_rows)
    piece = lambda j: src_ref[idx, j * rows:(j + 1) * rows, :]
    m_new = m
    for j, b in enumerate(bias_rows):
        m_new = jnp.maximum(m_new, jnp.max(piece(j), axis=0, keepdims=True) + b)
    alpha = jnp.exp2(m - m_new)
    p = jnp.concatenate([jnp.exp2(piece(j) + (b - m_new)).astype(BF16) for j, b in enumerate(bias_rows)], axis=0)
    acc_ref[idx] = alpha * acc_ref[idx] + jnp.dot(vt_blk, p, preferred_element_type=F32)
    return m_new


def _normalized(acc, dv):
    return acc[:dv] * (1.0 / acc[dv:dv + 1])


def _nsa_attn_kernel(qc_ref, qr_ref, kcd_ref, vct_ref, ks_ref, kw_ref, vst_ref, vwt_ref, gt_ref, ov_ref,
                     o_ref, og_ref, acc_ref, selb_ref, sa_ref, sb_ref):
    qi = pl.program_id(1)
    tq = ATT_Q_TILE
    t0 = qi * tq
    n_pairs = qc_ref.shape[0]
    n_groups = kcd_ref.shape[0]
    group_of = lambda c: c // (n_pairs // n_groups)
    n_cmp = kcd_ref.shape[1]
    seq = ks_ref.shape[1]
    n_sel = seq // SLC_BLOCK
    tok1 = t0 + lax.broadcasted_iota(jnp.int32, (1, tq), 1)
    tok2 = _two(tok1)

    kb_sz = ATT_KEY_BLOCK

    def gate(j, c):
        return jnp.concatenate([gt_ref[j * NSA_HEADS + 2 * c:j * NSA_HEADS + 2 * c + 1, :],
                                gt_ref[j * NSA_HEADS + 2 * c + 1:j * NSA_HEADS + 2 * c + 2, :]], axis=1)

    rhs = [_split_heads_rhs(qr_ref[c]) for c in range(n_pairs)]

    def sel_scores(kb, dst_ref):
        k0 = pl.multiple_of(kb * kb_sz, kb_sz)
        for c in range(n_pairs):
            dst_ref[c] = _scores_t(ks_ref[group_of(c), pl.ds(k0, kb_sz), :], rhs[c])

    win_blocks = WINDOW // kb_sz + 1
    win_start = jnp.minimum(jnp.maximum(t0 - WINDOW, 0) // kb_sz, seq // kb_sz - win_blocks)
    w0 = pl.multiple_of(win_start * kb_sz, kb_sz)

    cmp_scores = [_scores_t(kcd_ref[group_of(c)], _split_heads_rhs(qc_ref[c])) for c in range(n_pairs)]
    win_scores = [_scores_t(kw_ref[group_of(c), pl.ds(w0, win_blocks * kb_sz), :], rhs[c]) for c in range(n_pairs)]

    cmp_end = lax.broadcasted_iota(jnp.int32, (n_cmp, 1), 0) * CMP_STRIDE + (CMP_BLOCK - 1)
    vis = cmp_end <= tok2
    any_vis = (tok2 >= CMP_BLOCK - 1).astype(F32)
    psum = [jnp.zeros((n_cmp, tq), F32) for _ in range(n_groups)]
    for c in range(n_pairs):
        s = jnp.where(vis, cmp_scores[c], NEG)
        m = jnp.max(s, axis=0, keepdims=True)
        e = jnp.exp2(s - m)
        p = e * (any_vis / jnp.sum(e, axis=0, keepdims=True))
        psum[group_of(c)] += p[:, :tq] + p[:, tq:]
        og_ref[c] = gate(0, c) * jnp.dot(vct_ref[group_of(c)], p.astype(BF16), preferred_element_type=F32)

    imp_raw = [jnp.dot(ov_ref[...], ps, precision=lax.Precision.HIGHEST, preferred_element_type=F32)
               for ps in psum]
    sel_scores(0, sa_ref)

    wrow = w0 + lax.broadcasted_iota(jnp.int32, (win_blocks * kb_sz, 1), 0)
    in_window = (wrow <= tok2) & (wrow > tok2 - WINDOW)
    for c in range(n_pairs):
        s = jnp.where(in_window, win_scores[c], NEG)
        m = jnp.max(s, axis=0, keepdims=True)
        p = jnp.exp2(s - m).astype(BF16)
        win = jnp.zeros((NSA_HEAD_DIM + SUM_ROWS, 2 * tq), F32)
        for j in range(win_blocks):
            win += jnp.dot(vwt_ref[group_of(c), win_start + j], p[j * kb_sz:(j + 1) * kb_sz, :],
                           preferred_element_type=F32)
        og_ref[c] += gate(2, c) * _normalized(win, NSA_HEAD_DIM)

    blk = lax.broadcasted_iota(jnp.int32, (n_sel, 1), 0)
    cur = tok1 // SLC_BLOCK
    forced = (blk == 0) | (blk == cur) | (blk == cur - 1)
    valid = blk <= cur
    rows8 = 8
    sub = lax.broadcasted_iota(jnp.int32, (rows8, 1), 0)
    for gi in range(n_groups):
        imp = jnp.where(valid, imp_raw[gi] + jnp.where(forced, FORCE_BONUS, 0.0), NEG)
        grp = [imp[rows8 * j:rows8 * (j + 1), :] for j in range(n_sel // rows8)]
        rank = [jnp.zeros((rows8, tq), jnp.int32) for _ in grp]
        for m in range(n_sel):
            row = imp[m:m + 1, :]
            for j in range(len(grp)):
                ge = lambda: (row >= grp[j]).astype(jnp.int32)
                gt = lambda: (row > grp[j]).astype(jnp.int32)
                if rows8 * j > m:
                    beats = ge()
                elif rows8 * j + rows8 - 1 <= m:
                    beats = gt()
                else:
                    beats = jnp.where(sub + rows8 * j > m, ge(), gt())
                rank[j] = rank[j] + beats
        rank = jnp.concatenate(rank, axis=0)
        selb_ref[gi] = jnp.where((rank < SLC_TOPK) & valid, 0.0, NEG)

    last = (t0 + tq - 1) // kb_sz
    blocks_per_kb = kb_sz // SLC_BLOCK
    krow = lax.broadcasted_iota(jnp.int32, (kb_sz, 1), 0)
    acc_ref[...] = jnp.zeros_like(acc_ref)

    def sel_bias_rows(kb):
        return [[_two(selb_ref[gi, pl.ds(kb * blocks_per_kb + j, 1), :]) for j in range(blocks_per_kb)]
                for gi in range(n_groups)]

    def sel_below_diagonal(kb, src_ref, st):
        rows = sel_bias_rows(kb)
        return tuple(_flash_update_block_bias(src_ref, c, rows[group_of(c)], vst_ref[group_of(c), kb], st[c], acc_ref)
                     for c in range(n_pairs))

    def sel_diagonal(kb, src_ref, st):
        causal = kb * kb_sz + krow <= tok2
        bias = [jnp.where(causal, jnp.concatenate([jnp.broadcast_to(r, (SLC_BLOCK, 2 * tq)) for r in rows], axis=0),
                          NEG) for rows in sel_bias_rows(kb)]
        return tuple(_flash_update(src_ref[c] + bias[group_of(c)], vst_ref[group_of(c), kb], st[c], acc_ref, c)
                     for c in range(n_pairs))

    def sel_two_blocks(i, st):
        kb = 2 * i
        sel_scores(kb + 1, sb_ref)
        st = sel_below_diagonal(kb, sa_ref, st)
        sel_scores(kb + 2, sa_ref)
        return sel_below_diagonal(kb + 1, sb_ref, st)

    def sel_tail_odd(st):
        sel_scores(last, sb_ref)
        st = sel_below_diagonal(last - 1, sa_ref, st)
        return sel_diagonal(last, sb_ref, st)

    init = tuple(jnp.full((1, 2 * tq), NEG, F32) for _ in range(n_pairs))
    st = lax.fori_loop(0, last // 2, sel_two_blocks, init)
    st = lax.cond(last % 2 == 0, lambda st: sel_diagonal(last, sa_ref, st), sel_tail_odd, st)

    for c in range(n_pairs):
        o_t = og_ref[c] + gate(1, c) * _normalized(acc_ref[c], NSA_HEAD_DIM)
        o_ref[c] = jnp.concatenate([o_t[:, :tq], o_t[:, tq:]], axis=0).T.astype(BF16)


def _cmp_to_slc_t(n_cmp_rows, n_sel):
    c0 = np.arange(n_cmp_rows)[None, :] * CMP_STRIDE
    s0 = np.arange(n_sel)[:, None] * SLC_BLOCK
    ov = np.minimum(c0 + CMP_BLOCK, s0 + SLC_BLOCK) - np.maximum(c0, s0)
    return (np.clip(ov, 0, None) / CMP_BLOCK).astype(np.float32)


def _nsa_attn(qc, qr, kcd, vct, ksd, kwd, vst, vwt, gt):
    b, pairs, s, _ = qc.shape
    groups = NSA_KV_GROUPS
    n_cmp = kcd.shape[2]
    nkb = s // ATT_KEY_BLOCK
    dh = NSA_HEAD_DIM
    tq = ATT_Q_TILE
    n_sel = s // SLC_BLOCK
    ov_t = jnp.asarray(_cmp_to_slc_t(n_cmp, n_sel))
    q_spec = pl.BlockSpec((None, pairs, tq, LANES), lambda bi, qi: (bi, 0, qi, 0))
    per_batch = lambda *shape: pl.BlockSpec((None,) + shape, lambda bi, qi: (bi,) + (0,) * len(shape))
    return pl.pallas_call(
        _nsa_attn_kernel,
        grid=(b, s // tq),
        in_specs=[q_spec, q_spec, per_batch(groups, n_cmp, LANES), per_batch(groups, dh, n_cmp),
                  per_batch(groups, s, LANES), per_batch(groups, s, LANES),
                  per_batch(groups, nkb, dh + SUM_ROWS, ATT_KEY_BLOCK),
                  per_batch(groups, nkb, dh + SUM_ROWS, ATT_KEY_BLOCK),
                  pl.BlockSpec((None, LANES, tq), lambda bi, qi: (bi, 0, qi)),
                  _resident(ov_t.shape)],
        out_specs=q_spec,
        out_shape=jax.ShapeDtypeStruct((b, pairs, s, LANES), BF16),
        scratch_shapes=[pltpu.VMEM((pairs, dh, 2 * tq), F32), pltpu.VMEM((pairs, dh + SUM_ROWS, 2 * tq), F32),
                        pltpu.VMEM((groups, n_sel, tq), F32),
                        pltpu.VMEM((pairs, ATT_KEY_BLOCK, 2 * tq), F32), pltpu.VMEM((pairs, ATT_KEY_BLOCK, 2 * tq), F32)],
        compiler_params=_cparams(2),
        name="nsa_attn",
    )(qc, qr, kcd, vct, ksd, kwd, vst, vwt, gt, ov_t)


def _out_ffn_kernel(o_ref, x_ref, wo_ref, g1_ref, b1_ref, wg_ref, wu_ref, wd_ref, g2_ref, b2_ref, y_ref, h_ref):
    o = jnp.concatenate([o_ref[j] for j in range(o_ref.shape[0])], axis=1)
    mix = jnp.dot(o, wo_ref[...], preferred_element_type=F32)
    x1 = _layer_norm_rows(DEEPNORM_ALPHA * x_ref[...] + mix, g1_ref[...], b1_ref[...])
    y_ref[...] = _ffn_ln_rows(x1, wg_ref, wu_ref, wd_ref, g2_ref, b2_ref, h_ref)


def _out_ffn(o, x, wo, g1, b1, ffn):
    bsz, n_chunks, s, _ = o.shape
    d = x.shape[-1]
    wg, wu, wd, layer, g2, b2 = ffn
    d_ff = wg.shape[-1]
    x_spec = pl.BlockSpec((None, ROW_TILE, d), lambda bi, i: (bi, i, 0))
    return pl.pallas_call(
        _out_ffn_kernel,
        grid=(bsz, s // ROW_TILE),
        in_specs=[pl.BlockSpec((None, n_chunks, ROW_TILE, LANES), lambda bi, i: (bi, 0, i, 0)), x_spec,
                  _resident((d, d)), _resident((1, d)), _resident((1, d)),
                  _resident_layer((d, d_ff), layer), _resident_layer((d, d_ff), layer),
                  _resident_layer((d_ff, d), layer), _resident((1, d)), _resident((1, d))],
        out_specs=x_spec,
        out_shape=jax.ShapeDtypeStruct(x.shape, F32),
        scratch_shapes=[pltpu.VMEM((ROW_TILE, d_ff), BF16)],
        compiler_params=_cparams(2),
        name="mixer_out_ffn",
    )(o, x, wo, g1, b1, wg, wu, wd, g2, b2)


def _pair_maps_columns(w):
    d = w.shape[0]
    return w.reshape(d, 2, DIFF_HEADS, DIFF_HEAD_DIM).transpose(0, 2, 1, 3).reshape(d, -1)


def _kv_proj_kernel(x_ref, w_ref, c_ref, sl_ref, sh_ref, k_ref, vt_ref):
    proj = jnp.dot(x_ref[...].astype(BF16), w_ref[...], preferred_element_type=F32)
    c, sl, sh = c_ref[...], sl_ref[...], sh_ref[...]
    n_kb = x_ref.shape[0] // ATT_KEY_BLOCK
    ones = jnp.ones((SUM_ROWS, ATT_KEY_BLOCK), BF16)
    for h in range(DIFF_HEADS):
        k_ref[h] = _rope_chunk(proj[:, h * LANES:(h + 1) * LANES], c, sl, sh).astype(BF16)
        vt = proj[:, D_MODEL + h * DIFF_V_DIM:D_MODEL + (h + 1) * DIFF_V_DIM].T.astype(BF16)
        for kb in range(n_kb):
            vt_ref[h, kb, :DIFF_V_DIM, :] = vt[:, kb * ATT_KEY_BLOCK:(kb + 1) * ATT_KEY_BLOCK]
            vt_ref[h, kb, DIFF_V_DIM:, :] = ones


def _kv_proj(x, w, tables):
    b, s, d = x.shape
    nkb = s // ATT_KEY_BLOCK
    kb_tile = ROW_TILE // ATT_KEY_BLOCK
    tab_spec = pl.BlockSpec((ROW_TILE, LANES), lambda bi, i: (i, 0))
    return pl.pallas_call(
        _kv_proj_kernel,
        grid=(b, s // ROW_TILE),
        in_specs=[pl.BlockSpec((None, ROW_TILE, d), lambda bi, i: (bi, i, 0)), _resident(w.shape),
                  tab_spec, tab_spec, tab_spec],
        out_specs=[pl.BlockSpec((None, DIFF_HEADS, ROW_TILE, LANES), lambda bi, i: (bi, 0, i, 0)),
                   pl.BlockSpec((None, DIFF_HEADS, kb_tile, DIFF_V_DIM + SUM_ROWS, ATT_KEY_BLOCK),
                                lambda bi, i: (bi, 0, i, 0, 0))],
        out_shape=[jax.ShapeDtypeStruct((b, DIFF_HEADS, s, LANES), BF16),
                   jax.ShapeDtypeStruct((b, DIFF_HEADS, nkb, DIFF_V_DIM + SUM_ROWS, ATT_KEY_BLOCK), BF16)],
        compiler_params=_cparams(2),
        name="diff_kv_proj",
    )(x, w, *tables)


def _dq_proj_kernel(x_ref, wg_ref, wu_ref, wd_ref, g_ref, b_ref, w_ref, c_ref, sl_ref, sh_ref,
                    xo_ref, q_ref, h_ref):
    x = _ffn_ln_rows(x_ref[...], wg_ref, wu_ref, wd_ref, g_ref, b_ref, h_ref)
    xo_ref[...] = x
    proj = jnp.dot(x.astype(BF16), w_ref[...], preferred_element_type=F32)
    c, sl, sh = c_ref[...], sl_ref[...], sh_ref[...]
    for h in range(DIFF_HEADS):
        q = proj[:, h * LANES:(h + 1) * LANES] * (DIFF_HEAD_DIM ** -0.5 * LOG2E)
        q_ref[h] = _rope_chunk(q, c, sl, sh).astype(BF16)


def _dq_proj(x, ffn, w, tables):
    b, s, d = x.shape
    wg, wu, wd, layer, ln_g, ln_b = ffn
    d_ff = wg.shape[-1]
    tab_spec = pl.BlockSpec((ROW_TILE, LANES), lambda bi, i: (i, 0))
    x_spec = pl.BlockSpec((None, ROW_TILE, d), lambda bi, i: (bi, i, 0))
    return pl.pallas_call(
        _dq_proj_kernel,
        grid=(b, s // ROW_TILE),
        in_specs=[x_spec, _resident_layer((d, d_ff), layer), _resident_layer((d, d_ff), layer),
                  _resident_layer((d_ff, d), layer),
                  _resident((1, d)), _resident((1, d)), _resident(w.shape), tab_spec, tab_spec, tab_spec],
        out_specs=[x_spec, pl.BlockSpec((None, DIFF_HEADS, ROW_TILE, LANES), lambda bi, i: (bi, 0, i, 0))],
        out_shape=[jax.ShapeDtypeStruct((b, s, d), F32), jax.ShapeDtypeStruct((b, DIFF_HEADS, s, LANES), BF16)],
        scratch_shapes=[pltpu.VMEM((ROW_TILE, d_ff), BF16)],
        compiler_params=_cparams(2),
        name="ffn_diff_q_proj",
    )(x, wg, wu, wd, ln_g, ln_b, w, *tables)


def _diff_attn_kernel(q_ref, k_ref, vt_ref, lam_ref, gain_ref, o_ref, acc_ref, sa_ref, sb_ref, *, lambda_init):
    qi = pl.program_id(2)
    n_heads = q_ref.shape[0]
    tq = DIFF_Q_TILE
    t0 = qi * tq
    kb_sz = ATT_KEY_BLOCK
    tok2 = _two(t0 + lax.broadcasted_iota(jnp.int32, (1, tq), 1))
    krow = lax.broadcasted_iota(jnp.int32, (kb_sz, 1), 0)
    rhs = [_split_heads_rhs(q_ref[h]) for h in range(n_heads)]
    last = (t0 + tq - 1) // kb_sz
    acc_ref[...] = jnp.zeros_like(acc_ref)

    def scores_into(kb, dst_ref):
        k0 = pl.multiple_of(kb * kb_sz, kb_sz)
        for h in range(n_heads):
            dst_ref[h] = _scores_t(k_ref[h, pl.ds(k0, kb_sz), :], rhs[h])

    def consume(kb, src_ref, st, maybe_diagonal):
        if maybe_diagonal:
            visible = kb * kb_sz + krow <= tok2
        out = []
        for h in range(n_heads):
            s = src_ref[h]
            if maybe_diagonal:
                s = jnp.where(visible, s, NEG)
            out.append(_flash_update(s, vt_ref[h, kb], st[h], acc_ref, h))
        return tuple(out)

    def two_blocks(i, st):
        kb = 2 * i
        scores_into(kb + 1, sb_ref)
        st = consume(kb, sa_ref, st, False)
        scores_into(kb + 2, sa_ref)
        return consume(kb + 1, sb_ref, st, False)

    def tail_odd(st):
        scores_into(last, sb_ref)
        st = consume(last - 1, sa_ref, st, False)
        return consume(last, sb_ref, st, True)

    init = tuple(jnp.full((1, 2 * tq), NEG, F32) for _ in range(n_heads))
    scores_into(0, sa_ref)
    st = lax.fori_loop(0, last // 2, two_blocks, init)
    st = lax.cond(last % 2 == 0, lambda st: consume(last, sa_ref, st, True), tail_odd, st)

    lp = lam_ref[...]
    lam = (jnp.exp(jnp.sum(lp[0:1] * lp[1:2], axis=1, keepdims=True))
           - jnp.exp(jnp.sum(lp[2:3] * lp[3:4], axis=1, keepdims=True)) + lambda_init)
    for h in range(n_heads):
        a = _normalized(acc_ref[h], DIFF_V_DIM)
        o = a[:, :tq] - lam * a[:, tq:]
        ms = jnp.mean(o * o, axis=0, keepdims=True)
        o = o * lax.rsqrt(ms + NORM_EPS) * gain_ref[...] * (1.0 - lambda_init)
        o_ref[h] = o.T.astype(BF16)


def _diff_attn(q, k, vt, lam_params, gain, lambda_init):
    b, h, s, _ = q.shape
    nkb = s // ATT_KEY_BLOCK
    tq = DIFF_Q_TILE
    hps = DIFF_HEADS_PER_STEP
    per_head = lambda *shape: pl.BlockSpec((None, hps) + shape, lambda bi, hi, qi: (bi, hi) + (0,) * len(shape))
    q_spec = pl.BlockSpec((None, hps, tq, LANES), lambda bi, hi, qi: (bi, hi, qi, 0))
    return pl.pallas_call(
        functools.partial(_diff_attn_kernel, lambda_init=lambda_init),
        grid=(b, h // hps, s // tq),
        in_specs=[q_spec, per_head(s, LANES), per_head(nkb, DIFF_V_DIM + SUM_ROWS, ATT_KEY_BLOCK),
                  _resident(lam_params.shape), _resident(gain.shape)],
        out_specs=q_spec,
        out_shape=jax.ShapeDtypeStruct((b, h, s, LANES), BF16),
        scratch_shapes=[pltpu.VMEM((hps, DIFF_V_DIM + SUM_ROWS, 2 * tq), F32),
                        pltpu.VMEM((hps, ATT_KEY_BLOCK, 2 * tq), F32), pltpu.VMEM((hps, ATT_KEY_BLOCK, 2 * tq), F32)],
        compiler_params=_cparams(3),
        name="diff_attn",
    )(q, k, vt, lam_params, gain)


def kernel(x, ffn1_w_gate, ffn1_w_up, ffn1_w_down, ffn2_w_gate, ffn2_w_up, ffn2_w_down, ln_g, ln_b, a_w_in,
           a_cmp_pos_k, a_cmp_w1_k, a_cmp_b1_k, a_cmp_w2_k, a_cmp_pos_v, a_cmp_w1_v, a_cmp_b1_v, a_cmp_w2_v,
           a_w_out, b_w_kv_shared, b_w_q, b_lambda_q1, b_lambda_k1, b_lambda_q2, b_lambda_k2, b_subln_g, b_w_out):
    bsz, seq, d = x.shape
    assert d == D_MODEL and seq % ROW_TILE == 0 and seq >= WINDOW + 2 * ATT_KEY_BLOCK
    tables = _rope_tables(seq)
    bf = lambda w: w.astype(BF16)
    row = lambda v: v[None, :]

    w1 = (bf(ffn1_w_gate), bf(ffn1_w_up), bf(ffn1_w_down))
    w2 = (bf(ffn2_w_gate), bf(ffn2_w_up), bf(ffn2_w_down))
    k12 = vt = None
    for layer in range(DEPTH):
        ffn1 = w1 + (layer, row(ln_g[layer, 0]), row(ln_b[layer, 0]))
        if layer < N_A_LAYERS:
            x, qc, qr, kc, vc, ksd, kwd, vst, vwt, gt = _nsa_proj(x, ffn1, _nsa_proj_weight(a_w_in[layer]), tables)
            wk = _compress_weights(a_cmp_pos_k[layer], a_cmp_w1_k[layer], a_cmp_b1_k[layer], a_cmp_w2_k[layer], True)
            wv = _compress_weights(a_cmp_pos_v[layer], a_cmp_w1_v[layer], a_cmp_b1_v[layer], a_cmp_w2_v[layer], False)
            kcd, vct = _compress(kc, vc, wk, wv)
            o = _nsa_attn(qc, qr, kcd, vct, ksd, kwd, vst, vwt, gt)
            w_out = a_w_out[layer]
        else:
            j = layer - N_A_LAYERS
            lambda_init = 0.8 - 0.6 * float(np.exp(-0.3 * layer))
            x, q = _dq_proj(x, ffn1, bf(_pair_maps_columns(b_w_q[j])), tables)
            lam_params = jnp.pad(jnp.stack([b_lambda_q1[j], b_lambda_k1[j], b_lambda_q2[j], b_lambda_k2[j]]),
                                 ((0, 4), (0, LANES - DIFF_HEAD_DIM)))
            gain = jnp.broadcast_to(b_subln_g[j][:, None], (DIFF_V_DIM, DIFF_Q_TILE))
            o = _diff_attn(q, k12, vt, lam_params, gain, lambda_init)
            w_out = b_w_out[j]
        x = _out_ffn(o, x, bf(w_out), row(ln_g[layer, 1]), row(ln_b[layer, 1]),
                     w2 + (layer, row(ln_g[layer, 2]), row(ln_b[layer, 2])))
        if layer == N_A_LAYERS - 1:
            w_kv = jnp.concatenate([_pair_maps_columns(b_w_kv_shared[:, :D_MODEL]), b_w_kv_shared[:, D_MODEL:]], axis=1)
            k12, vt = _kv_proj(x, bf(w_kv), tables)
    return x
```

```python
import functools

import numpy as np
import jax
import jax.numpy as jnp
from jax import lax
from jax.experimental import pallas as pl
from jax.experimental.pallas import tpu as pltpu

F32 = jnp.float32
BF16 = jnp.bfloat16

D_MODEL = 1024
DEPTH = 4
N_A_LAYERS = DEPTH // 2
NSA_HEADS = 16
NSA_HEAD_DIM = D_MODEL // NSA_HEADS
NSA_KV_GROUPS = 2
NSA_KV_WIDTH = NSA_KV_GROUPS * NSA_HEAD_DIM
NSA_GATE_WIDTH = 3 * NSA_HEADS
CMP_BLOCK = 32
CMP_STRIDE = 16
CMP_MLP_HIDDEN = 128
SLC_BLOCK = 64
SLC_TOPK = 16
WINDOW = 512
FORCE_BONUS = 1000.0
DIFF_HEADS = 8
DIFF_HEAD_DIM = D_MODEL // (2 * DIFF_HEADS)
DIFF_V_DIM = 2 * DIFF_HEAD_DIM
ROPE_THETA = 500000.0
ROPE_FRACTION = 4
D_FF = 2816
DEEPNORM_ALPHA = (2 * DEPTH) ** 0.25
NORM_EPS = 1e-5
NEG = -1e30

LANES = 128
ROW_TILE = 512
FFN_COL_CHUNK = 512
ATT_Q_TILE = 256
DIFF_Q_TILE = 256
ATT_KEY_BLOCK = 256
DIFF_HEADS_PER_STEP = 8
SUM_ROWS = 16
LOG2E = 1.4426950408889634
VMEM_LIMIT_BYTES = 56 * 1024 * 1024
N_CHUNKS = D_MODEL // LANES


def _cparams(n_axes):
    return pltpu.CompilerParams(dimension_semantics=("parallel",) * n_axes,
                                vmem_limit_bytes=VMEM_LIMIT_BYTES)


def _resident(shape):
    zeros = (0,) * len(shape)
    return pl.BlockSpec(shape, lambda *_: zeros, pipeline_mode=pl.Buffered(1))


def _resident_layer(shape, layer):
    zeros = (0,) * len(shape)
    return pl.BlockSpec((None,) + shape, lambda *_: (layer,) + zeros, pipeline_mode=pl.Buffered(1))


def _layer_norm_rows(r, g, b):
    mu = jnp.mean(r, axis=-1, keepdims=True)
    xc = r - mu
    var = jnp.mean(xc * xc, axis=-1, keepdims=True)
    return xc * lax.rsqrt(var + NORM_EPS) * g + b


def _ffn_ln_rows(x, wg_ref, wu_ref, wd_ref, g_ref, b_ref, h_ref):
    xb = x.astype(BF16)
    d_ff = wg_ref.shape[1]
    for c0 in range(0, d_ff, FFN_COL_CHUNK):
        c1 = min(c0 + FFN_COL_CHUNK, d_ff)
        gate = jnp.dot(xb, wg_ref[:, c0:c1], preferred_element_type=F32)
        up = jnp.dot(xb, wu_ref[:, c0:c1], preferred_element_type=F32)
        silu = gate * (1.0 / (1.0 + jnp.exp(-gate)))
        h_ref[:, c0:c1] = (silu * up).astype(BF16)
    y = jnp.dot(h_ref[...], wd_ref[...], preferred_element_type=F32)
    return _layer_norm_rows(DEEPNORM_ALPHA * x + 0.5 * y, g_ref[...], b_ref[...])


def _rope_tables(seq):
    r = NSA_HEAD_DIM // ROPE_FRACTION
    half = r // 2
    inv = ROPE_THETA ** (-jnp.arange(half, dtype=F32) * 2.0 / r)
    ang = jnp.arange(seq).astype(F32)[:, None] * inv[None, :]
    cos, sin = jnp.cos(ang), jnp.sin(ang)
    pad = jnp.zeros((seq, NSA_HEAD_DIM - r), F32)
    zero = jnp.zeros_like(sin)
    c_head = jnp.concatenate([cos, cos, pad + 1.0], axis=1)
    s_lo_head = jnp.concatenate([-sin, zero, pad], axis=1)
    s_hi_head = jnp.concatenate([zero, sin, pad], axis=1)
    rep = lambda t: jnp.concatenate([t, t], axis=1)
    return rep(c_head), rep(s_lo_head), rep(s_hi_head)


def _rope_chunk(x, c, s_lo, s_hi):
    half = NSA_HEAD_DIM // ROPE_FRACTION // 2
    return x * c + pltpu.roll(x, LANES - half, 1) * s_lo + pltpu.roll(x, half, 1) * s_hi


def _nsa_proj_kernel(x_ref, wg_ref, wu_ref, wd_ref, g_ref, b_ref, w_ref, c_ref, sl_ref, sh_ref,
                     xo_ref, qc_ref, qr_ref, kc_ref, vc_ref, ks_ref, kw_ref, vst_ref, vwt_ref, gt_ref, h_ref):
    x = _ffn_ln_rows(x_ref[...], wg_ref, wu_ref, wd_ref, g_ref, b_ref, h_ref)
    xo_ref[...] = x
    proj = jnp.dot(x.astype(BF16), w_ref[...], preferred_element_type=F32)
    c, sl, sh = c_ref[...], sl_ref[...], sh_ref[...]
    col = lambda j: proj[:, j * LANES:(j + 1) * LANES]
    for j in range(N_CHUNKS):
        q = col(j) * (NSA_HEAD_DIM ** -0.5 * LOG2E)
        qc_ref[j] = q.astype(BF16)
        qr_ref[j] = _rope_chunk(q, c, sl, sh).astype(BF16)
    base = N_CHUNKS
    kc_ref[...] = col(base)
    vc_ref[...] = col(base + 1)
    for g in range(NSA_KV_GROUPS):
        ks_ref[g] = _rope_chunk(col(base + 2 + g), c, sl, sh).astype(BF16)
        kw_ref[g] = _rope_chunk(col(base + 4 + g), c, sl, sh).astype(BF16)
    n_kb = x_ref.shape[0] // ATT_KEY_BLOCK
    ones = jnp.ones((SUM_ROWS, ATT_KEY_BLOCK), BF16)
    for src, dst in ((base + 6, vst_ref), (base + 7, vwt_ref)):
        vt = col(src).T.astype(BF16)
        for g in range(NSA_KV_GROUPS):
            for kb in range(n_kb):
                dst[g, kb, :NSA_HEAD_DIM, :] = vt[g * NSA_HEAD_DIM:(g + 1) * NSA_HEAD_DIM,
                                                  kb * ATT_KEY_BLOCK:(kb + 1) * ATT_KEY_BLOCK]
                dst[g, kb, NSA_HEAD_DIM:, :] = ones
    gates = col(base + 8)
    gt_ref[...] = (1.0 / (1.0 + jnp.exp(-gates))).T


def _nsa_proj_weight(w_in):
    d = D_MODEL
    kvw = NSA_KV_WIDTH
    q, kc, vc, ks, vs, kw, vw, gates = jnp.split(
        w_in, [int(c) for c in np.cumsum([d] + [kvw] * 6)], axis=1)
    dh = NSA_HEAD_DIM
    dup = lambda t: jnp.concatenate([t[:, :dh], t[:, :dh], t[:, dh:], t[:, dh:]], axis=1)
    gates = jnp.pad(gates, ((0, 0), (0, LANES - NSA_GATE_WIDTH)))
    return jnp.concatenate([q, kc, vc, dup(ks), dup(kw), vs, vw, gates], axis=1).astype(BF16)


def _nsa_proj(x, ffn, w, tables):
    b, s, d = x.shape
    wg, wu, wd, layer, ln_g, ln_b = ffn
    d_ff = wg.shape[-1]
    ncol = w.shape[1]
    nkb = s // ATT_KEY_BLOCK
    kb_tile = ROW_TILE // ATT_KEY_BLOCK
    dh = NSA_HEAD_DIM + SUM_ROWS
    tab_spec = pl.BlockSpec((ROW_TILE, LANES), lambda bi, i: (i, 0))
    chunk_spec = lambda n: pl.BlockSpec((None, n, ROW_TILE, LANES), lambda bi, i: (bi, 0, i, 0))
    row_spec = pl.BlockSpec((None, ROW_TILE, LANES), lambda bi, i: (bi, i, 0))
    vt_spec = pl.BlockSpec((None, NSA_KV_GROUPS, kb_tile, dh, ATT_KEY_BLOCK), lambda bi, i: (bi, 0, i, 0, 0))
    x_spec = pl.BlockSpec((None, ROW_TILE, d), lambda bi, i: (bi, i, 0))
    return pl.pallas_call(
        _nsa_proj_kernel,
        grid=(b, s // ROW_TILE),
        in_specs=[x_spec, _resident_layer((d, d_ff), layer), _resident_layer((d, d_ff), layer),
                  _resident_layer((d_ff, d), layer), _resident((1, d)), _resident((1, d)),
                  _resident((d, ncol)), tab_spec, tab_spec, tab_spec],
        out_specs=[x_spec, chunk_spec(N_CHUNKS), chunk_spec(N_CHUNKS), row_spec, row_spec,
                   chunk_spec(NSA_KV_GROUPS), chunk_spec(NSA_KV_GROUPS), vt_spec, vt_spec,
                   pl.BlockSpec((None, LANES, ROW_TILE), lambda bi, i: (bi, 0, i))],
        out_shape=[jax.ShapeDtypeStruct((b, s, d), F32),
                   jax.ShapeDtypeStruct((b, N_CHUNKS, s, LANES), BF16),
                   jax.ShapeDtypeStruct((b, N_CHUNKS, s, LANES), BF16),
                   jax.ShapeDtypeStruct((b, s, LANES), F32),
                   jax.ShapeDtypeStruct((b, s, LANES), F32),
                   jax.ShapeDtypeStruct((b, NSA_KV_GROUPS, s, LANES), BF16),
                   jax.ShapeDtypeStruct((b, NSA_KV_GROUPS, s, LANES), BF16),
                   jax.ShapeDtypeStruct((b, NSA_KV_GROUPS, nkb, dh, ATT_KEY_BLOCK), BF16),
                   jax.ShapeDtypeStruct((b, NSA_KV_GROUPS, nkb, dh, ATT_KEY_BLOCK), BF16),
                   jax.ShapeDtypeStruct((b, LANES, s), F32)],
        scratch_shapes=[pltpu.VMEM((ROW_TILE, d_ff), BF16)],
        compiler_params=_cparams(2),
        name="ffn_nsa_proj",
    )(x, wg, wu, wd, ln_g, ln_b, w, *tables)


def _gelu_tanh(x):
    return 0.5 * x * (1.0 + jnp.tanh(np.sqrt(2.0 / np.pi).astype(np.float32) * (x + 0.044715 * (x * x * x))))


def _compress_kernel(kch_ref, vch_ref, pk_ref, pv_ref, w1k_ref, w1v_ref, b1k_ref, b1v_ref, w2k_ref, w2v_ref,
                     kcd_ref, vct_ref):
    n_rows = kch_ref.shape[0] // CMP_STRIDE

    def mlp(ch_ref, pos_ref, w1_ref, b1_ref, w2_ref):
        ch = jnp.concatenate([ch_ref[pl.ds(l, n_rows, stride=CMP_STRIDE), :] for l in range(CMP_STRIDE)], axis=1)
        lo = jnp.dot((ch + pos_ref[0:1, :]).astype(BF16), w1_ref[0], preferred_element_type=F32)
        hi = jnp.dot((ch + pos_ref[1:2, :]).astype(BF16), w1_ref[1], preferred_element_type=F32)
        hid = lo + pltpu.roll(hi, n_rows - 1, 0) + b1_ref[...]
        return jnp.dot(_gelu_tanh(hid).astype(BF16), w2_ref[...], preferred_element_type=F32)

    kcd = mlp(kch_ref, pk_ref, w1k_ref, b1k_ref, w2k_ref)
    for g in range(NSA_KV_GROUPS):
        kcd_ref[g] = kcd[:, g * LANES:(g + 1) * LANES].astype(BF16)
    vct = mlp(vch_ref, pv_ref, w1v_ref, b1v_ref, w2v_ref).T
    for g in range(NSA_KV_GROUPS):
        vct_ref[g] = vct[g * NSA_HEAD_DIM:(g + 1) * NSA_HEAD_DIM, :].astype(BF16)


def _compress_weights(pos, w1, b1, w2, dup_out):
    g_n, dh, hid = NSA_KV_GROUPS, NSA_HEAD_DIM, CMP_MLP_HIDDEN
    w1r = w1.reshape(2, CMP_STRIDE, dh, hid)
    eye = jnp.eye(g_n, dtype=w1.dtype)
    w1x = jnp.einsum('hldm,gk->hlgdkm', w1r, eye).reshape(2, CMP_STRIDE * g_n * dh, g_n * hid)
    posx = jnp.broadcast_to(pos.reshape(2, CMP_STRIDE, 1, dh), (2, CMP_STRIDE, g_n, dh)).reshape(2, -1)
    b1x = jnp.tile(b1, g_n)[None, :]
    reps = 2 if dup_out else 1
    w2x = jnp.einsum('md,gk->gmkd', w2, eye)
    w2x = jnp.broadcast_to(w2x[:, :, :, None, :], (g_n, hid, g_n, reps, dh)).reshape(g_n * hid, g_n * reps * dh)
    return posx, w1x.astype(BF16), b1x, w2x.astype(BF16)


def _compress(kc, vc, wk, wv):
    b, s, _ = kc.shape
    n_rows = s // CMP_STRIDE
    kch, vch = kc, vc
    pk, w1k, b1k, w2k = wk
    pv, w1v, b1v, w2v = wv
    ch_spec = pl.BlockSpec((None, s, LANES), lambda bi: (bi, 0, 0))
    return pl.pallas_call(
        _compress_kernel,
        grid=(b,),
        in_specs=[ch_spec, ch_spec, _resident(pk.shape), _resident(pv.shape), _resident(w1k.shape),
                  _resident(w1v.shape), _resident(b1k.shape), _resident(b1v.shape), _resident(w2k.shape),
                  _resident(w2v.shape)],
        out_specs=[pl.BlockSpec((None, NSA_KV_GROUPS, n_rows, LANES), lambda bi: (bi, 0, 0, 0)),
                   pl.BlockSpec((None, NSA_KV_GROUPS, NSA_HEAD_DIM, n_rows), lambda bi: (bi, 0, 0, 0))],
        out_shape=[jax.ShapeDtypeStruct((b, NSA_KV_GROUPS, n_rows, LANES), BF16),
                   jax.ShapeDtypeStruct((b, NSA_KV_GROUPS, NSA_HEAD_DIM, n_rows), BF16)],
        compiler_params=_cparams(1),
        name="nsa_compress",
    )(kch, vch, pk, pv, w1k, w1v, b1k, b1v, w2k, w2v)


def _split_heads_rhs(q):
    lane = lax.broadcasted_iota(jnp.int32, q.shape, 1)
    zero = jnp.zeros_like(q)
    return jnp.concatenate([jnp.where(lane < LANES // 2, q, zero), jnp.where(lane >= LANES // 2, q, zero)], axis=0)


def _scores_t(k, rhs):
    return lax.dot_general(k, rhs, (((1,), (1,)), ((), ())), preferred_element_type=F32)


def _two(row):
    return jnp.concatenate([row, row], axis=1)


def _flash_update(s, vt_blk, m, acc_ref, idx):
    m_new = jnp.maximum(m, jnp.max(s, axis=0, keepdims=True))
    alpha = jnp.exp2(m - m_new)
    p = jnp.exp2(s - m_new).astype(BF16)
    acc_ref[idx] = alpha * acc_ref[idx] + jnp.dot(vt_blk, p, preferred_element_type=F32)
    return m_new


def _flash_update_block_bias(src_ref, idx, bias_rows, vt_blk, m, acc_ref):
    rows = src_ref.shape[1] // len(bias_rows)
    piece = lambda j: src_ref[idx, j * rows:(j + 1) * rows, :]
    m_new = m
    for j, b in enumerate(bias_rows):
        m_new = jnp.maximum(m_new, jnp.max(piece(j), axis=0, keepdims=True) + b)
    alpha = jnp.exp2(m - m_new)
    p = jnp.concatenate([jnp.exp2(piece(j) + (b - m_new)).astype(BF16) for j, b in enumerate(bias_rows)], axis=0)
    acc_ref[idx] = alpha * acc_ref[idx] + jnp.dot(vt_blk, p, preferred_element_type=F32)
    return m_new


def _normalized(acc, dv):
    return acc[:dv] * (1.0 / acc[dv:dv + 1])


def _nsa_attn_kernel(qc_ref, qr_ref, kcd_ref, vct_ref, ks_ref, kw_ref, vst_ref, vwt_ref, gt_ref, ov_ref,
                     o_ref, og_ref, acc_ref, selb_ref, sa_ref, sb_ref):
    qi = pl.program_id(1)
    tq = ATT_Q_TILE
    t0 = qi * tq
    n_pairs = qc_ref.shape[0]
    n_groups = kcd_ref.shape[0]
    group_of = lambda c: c // (n_pairs // n_groups)
    n_cmp = kcd_ref.shape[1]
    seq = ks_ref.shape[1]
    n_sel = seq // SLC_BLOCK
    tok1 = t0 + lax.broadcasted_iota(jnp.int32, (1, tq), 1)
    tok2 = _two(tok1)

    kb_sz = ATT_KEY_BLOCK

    def gate(j, c):
        return jnp.concatenate([gt_ref[j * NSA_HEADS + 2 * c:j * NSA_HEADS + 2 * c + 1, :],
                                gt_ref[j * NSA_HEADS + 2 * c + 1:j * NSA_HEADS + 2 * c + 2, :]], axis=1)

    rhs = [_split_heads_rhs(qr_ref[c]) for c in range(n_pairs)]

    def sel_scores(kb, dst_ref):
        k0 = pl.multiple_of(kb * kb_sz, kb_sz)
        for c in range(n_pairs):
            dst_ref[c] = _scores_t(ks_ref[group_of(c), pl.ds(k0, kb_sz), :], rhs[c])

    win_blocks = WINDOW // kb_sz + 1
    win_start = jnp.minimum(jnp.maximum(t0 - WINDOW, 0) // kb_sz, seq // kb_sz - win_blocks)
    w0 = pl.multiple_of(win_start * kb_sz, kb_sz)

    cmp_scores = [_scores_t(kcd_ref[group_of(c)], _split_heads_rhs(qc_ref[c])) for c in range(n_pairs)]
    win_scores = [_scores_t(kw_ref[group_of(c), pl.ds(w0, win_blocks * kb_sz), :], rhs[c]) for c in range(n_pairs)]

    cmp_end = lax.broadcasted_iota(jnp.int32, (n_cmp, 1), 0) * CMP_STRIDE + (CMP_BLOCK - 1)
    vis = cmp_end <= tok2
    any_vis = (tok2 >= CMP_BLOCK - 1).astype(F32)
    psum = [jnp.zeros((n_cmp, tq), F32) for _ in range(n_groups)]
    for c in range(n_pairs):
        s = jnp.where(vis, cmp_scores[c], NEG)
        m = jnp.max(s, axis=0, keepdims=True)
        e = jnp.exp2(s - m)
        p = e * (any_vis / jnp.sum(e, axis=0, keepdims=True))
        psum[group_of(c)] += p[:, :tq] + p[:, tq:]
        og_ref[c] = gate(0, c) * jnp.dot(vct_ref[group_of(c)], p.astype(BF16), preferred_element_type=F32)

    imp_raw = [jnp.dot(ov_ref[...], ps, precision=lax.Precision.HIGHEST, preferred_element_type=F32)
               for ps in psum]
    sel_scores(0, sa_ref)

    wrow = w0 + lax.broadcasted_iota(jnp.int32, (win_blocks * kb_sz, 1), 0)
    in_window = (wrow <= tok2) & (wrow > tok2 - WINDOW)
    for c in range(n_pairs):
        s = jnp.where(in_window, win_scores[c], NEG)
        m = jnp.max(s, axis=0, keepdims=True)
        p = jnp.exp2(s - m).astype(BF16)
        win = jnp.zeros((NSA_HEAD_DIM + SUM_ROWS, 2 * tq), F32)
        for j in range(win_blocks):
            win += jnp.dot(vwt_ref[group_of(c), win_start + j], p[j * kb_sz:(j + 1) * kb_sz, :],
                           preferred_element_type=F32)
        og_ref[c] += gate(2, c) * _normalized(win, NSA_HEAD_DIM)

    blk = lax.broadcasted_iota(jnp.int32, (n_sel, 1), 0)
    cur = tok1 // SLC_BLOCK
    forced = (blk == 0) | (blk == cur) | (blk == cur - 1)
    valid = blk <= cur
    rows8 = 8
    sub = lax.broadcasted_iota(jnp.int32, (rows8, 1), 0)
    for gi in range(n_groups):
        imp = jnp.where(valid, imp_raw[gi] + jnp.where(forced, FORCE_BONUS, 0.0), NEG)
        grp = [imp[rows8 * j:rows8 * (j + 1), :] for j in range(n_sel // rows8)]
        rank = [jnp.zeros((rows8, tq), jnp.int32) for _ in grp]
        for m in range(n_sel):
            row = imp[m:m + 1, :]
            for j in range(len(grp)):
                ge = lambda: (row >= grp[j]).astype(jnp.int32)
                gt = lambda: (row > grp[j]).astype(jnp.int32)
                if rows8 * j > m:
                    beats = ge()
                elif rows8 * j + rows8 - 1 <= m:
                    beats = gt()
                else:
                    beats = jnp.where(sub + rows8 * j > m, ge(), gt())
                rank[j] = rank[j] + beats
        rank = jnp.concatenate(rank, axis=0)
        selb_ref[gi] = jnp.where((rank < SLC_TOPK) & valid, 0.0, NEG)

    last = (t0 + tq - 1) // kb_sz
    blocks_per_kb = kb_sz // SLC_BLOCK
    krow = lax.broadcasted_iota(jnp.int32, (kb_sz, 1), 0)
    acc_ref[...] = jnp.zeros_like(acc_ref)

    def sel_bias_rows(kb):
        return [[_two(selb_ref[gi, pl.ds(kb * blocks_per_kb + j, 1), :]) for j in range(blocks_per_kb)]
                for gi in range(n_groups)]

    def sel_below_diagonal(kb, src_ref, st):
        rows = sel_bias_rows(kb)
        return tuple(_flash_update_block_bias(src_ref, c, rows[group_of(c)], vst_ref[group_of(c), kb], st[c], acc_ref)
                     for c in range(n_pairs))

    def sel_diagonal(kb, src_ref, st):
        causal = kb * kb_sz + krow <= tok2
        bias = [jnp.where(causal, jnp.concatenate([jnp.broadcast_to(r, (SLC_BLOCK, 2 * tq)) for r in rows], axis=0),
                          NEG) for rows in sel_bias_rows(kb)]
        return tuple(_flash_update(src_ref[c] + bias[group_of(c)], vst_ref[group_of(c), kb], st[c], acc_ref, c)
                     for c in range(n_pairs))

    def sel_two_blocks(i, st):
        kb = 2 * i
        sel_scores(kb + 1, sb_ref)
        st = sel_below_diagonal(kb, sa_ref, st)
        sel_scores(kb + 2, sa_ref)
        return sel_below_diagonal(kb + 1, sb_ref, st)

    def sel_tail_odd(st):
        sel_scores(last, sb_ref)
        st = sel_below_diagonal(last - 1, sa_ref, st)
        return sel_diagonal(last, sb_ref, st)

    init = tuple(jnp.full((1, 2 * tq), NEG, F32) for _ in range(n_pairs))
    st = lax.fori_loop(0, last // 2, sel_two_blocks, init)
    st = lax.cond(last % 2 == 0, lambda st: sel_diagonal(last, sa_ref, st), sel_tail_odd, st)

    for c in range(n_pairs):
        o_t = og_ref[c] + gate(1, c) * _normalized(acc_ref[c], NSA_HEAD_DIM)
        o_ref[c] = jnp.concatenate([o_t[:, :tq], o_t[:, tq:]], axis=0).T.astype(BF16)


def _cmp_to_slc_t(n_cmp_rows, n_sel):
    c0 = np.arange(n_cmp_rows)[None, :] * CMP_STRIDE
    s0 = np.arange(n_sel)[:, None] * SLC_BLOCK
    ov = np.minimum(c0 + CMP_BLOCK, s0 + SLC_BLOCK) - np.maximum(c0, s0)
    return (np.clip(ov, 0, None) / CMP_BLOCK).astype(np.float32)


def _nsa_attn(qc, qr, kcd, vct, ksd, kwd, vst, vwt, gt):
    b, pairs, s, _ = qc.shape
    groups = NSA_KV_GROUPS
    n_cmp = kcd.shape[2]
    nkb = s // ATT_KEY_BLOCK
    dh = NSA_HEAD_DIM
    tq = ATT_Q_TILE
    n_sel = s // SLC_BLOCK
    ov_t = jnp.asarray(_cmp_to_slc_t(n_cmp, n_sel))
    q_spec = pl.BlockSpec((None, pairs, tq, LANES), lambda bi, qi: (bi, 0, qi, 0))
    per_batch = lambda *shape: pl.BlockSpec((None,) + shape, lambda bi, qi: (bi,) + (0,) * len(shape))
    return pl.pallas_call(
        _nsa_attn_kernel,
        grid=(b, s // tq),
        in_specs=[q_spec, q_spec, per_batch(groups, n_cmp, LANES), per_batch(groups, dh, n_cmp),
                  per_batch(groups, s, LANES), per_batch(groups, s, LANES),
                  per_batch(groups, nkb, dh + SUM_ROWS, ATT_KEY_BLOCK),
                  per_batch(groups, nkb, dh + SUM_ROWS, ATT_KEY_BLOCK),
                  pl.BlockSpec((None, LANES, tq), lambda bi, qi: (bi, 0, qi)),
                  _resident(ov_t.shape)],
        out_specs=q_spec,
        out_shape=jax.ShapeDtypeStruct((b, pairs, s, LANES), BF16),
        scratch_shapes=[pltpu.VMEM((pairs, dh, 2 * tq), F32), pltpu.VMEM((pairs, dh + SUM_ROWS, 2 * tq), F32),
                        pltpu.VMEM((groups, n_sel, tq), F32),
                        pltpu.VMEM((pairs, ATT_KEY_BLOCK, 2 * tq), F32), pltpu.VMEM((pairs, ATT_KEY_BLOCK, 2 * tq), F32)],
        compiler_params=_cparams(2),
        name="nsa_attn",
    )(qc, qr, kcd, vct, ksd, kwd, vst, vwt, gt, ov_t)


def _out_ffn_kernel(o_ref, x_ref, wo_ref, g1_ref, b1_ref, wg_ref, wu_ref, wd_ref, g2_ref, b2_ref, y_ref, h_ref):
    o = jnp.concatenate([o_ref[j] for j in range(o_ref.shape[0])], axis=1)
    mix = jnp.dot(o, wo_ref[...], preferred_element_type=F32)
    x1 = _layer_norm_rows(DEEPNORM_ALPHA * x_ref[...] + mix, g1_ref[...], b1_ref[...])
    y_ref[...] = _ffn_ln_rows(x1, wg_ref, wu_ref, wd_ref, g2_ref, b2_ref, h_ref)


def _out_ffn(o, x, wo, g1, b1, ffn):
    bsz, n_chunks, s, _ = o.shape
    d = x.shape[-1]
    wg, wu, wd, layer, g2, b2 = ffn
    d_ff = wg.shape[-1]
    x_spec = pl.BlockSpec((None, ROW_TILE, d), lambda bi, i: (bi, i, 0))
    return pl.pallas_call(
        _out_ffn_kernel,
        grid=(bsz, s // ROW_TILE),
        in_specs=[pl.BlockSpec((None, n_chunks, ROW_TILE, LANES), lambda bi, i: (bi, 0, i, 0)), x_spec,
                  _resident((d, d)), _resident((1, d)), _resident((1, d)),
                  _resident_layer((d, d_ff), layer), _resident_layer((d, d_ff), layer),
                  _resident_layer((d_ff, d), layer), _resident((1, d)), _resident((1, d))],
        out_specs=x_spec,
        out_shape=jax.ShapeDtypeStruct(x.shape, F32),
        scratch_shapes=[pltpu.VMEM((ROW_TILE, d_ff), BF16)],
        compiler_params=_cparams(2),
        name="mixer_out_ffn",
    )(o, x, wo, g1, b1, wg, wu, wd, g2, b2)


def _pair_maps_columns(w):
    d = w.shape[0]
    return w.reshape(d, 2, DIFF_HEADS, DIFF_HEAD_DIM).transpose(0, 2, 1, 3).reshape(d, -1)


def _kv_proj_kernel(x_ref, w_ref, c_ref, sl_ref, sh_ref, k_ref, vt_ref):
    proj = jnp.dot(x_ref[...].astype(BF16), w_ref[...], preferred_element_type=F32)
    c, sl, sh = c_ref[...], sl_ref[...], sh_ref[...]
    n_kb = x_ref.shape[0] // ATT_KEY_BLOCK
    ones = jnp.ones((SUM_ROWS, ATT_KEY_BLOCK), BF16)
    for h in range(DIFF_HEADS):
        k_ref[h] = _rope_chunk(proj[:, h * LANES:(h + 1) * LANES], c, sl, sh).astype(BF16)
        vt = proj[:, D_MODEL + h * DIFF_V_DIM:D_MODEL + (h + 1) * DIFF_V_DIM].T.astype(BF16)
        for kb in range(n_kb):
            vt_ref[h, kb, :DIFF_V_DIM, :] = vt[:, kb * ATT_KEY_BLOCK:(kb + 1) * ATT_KEY_BLOCK]
            vt_ref[h, kb, DIFF_V_DIM:, :] = ones


def _kv_proj(x, w, tables):
    b, s, d = x.shape
    nkb = s // ATT_KEY_BLOCK
    kb_tile = ROW_TILE // ATT_KEY_BLOCK
    tab_spec = pl.BlockSpec((ROW_TILE, LANES), lambda bi, i: (i, 0))
    return pl.pallas_call(
        _kv_proj_kernel,
        grid=(b, s // ROW_TILE),
        in_specs=[pl.BlockSpec((None, ROW_TILE, d), lambda bi, i: (bi, i, 0)), _resident(w.shape),
                  tab_spec, tab_spec, tab_spec],
        out_specs=[pl.BlockSpec((None, DIFF_HEADS, ROW_TILE, LANES), lambda bi, i: (bi, 0, i, 0)),
                   pl.BlockSpec((None, DIFF_HEADS, kb_tile, DIFF_V_DIM + SUM_ROWS, ATT_KEY_BLOCK),
                                lambda bi, i: (bi, 0, i, 0, 0))],
        out_shape=[jax.ShapeDtypeStruct((b, DIFF_HEADS, s, LANES), BF16),
                   jax.ShapeDtypeStruct((b, DIFF_HEADS, nkb, DIFF_V_DIM + SUM_ROWS, ATT_KEY_BLOCK), BF16)],
        compiler_params=_cparams(2),
        name="diff_kv_proj",
    )(x, w, *tables)


def _dq_proj_kernel(x_ref, wg_ref, wu_ref, wd_ref, g_ref, b_ref, w_ref, c_ref, sl_ref, sh_ref,
                    xo_ref, q_ref, h_ref):
    x = _ffn_ln_rows(x_ref[...], wg_ref, wu_ref, wd_ref, g_ref, b_ref, h_ref)
    xo_ref[...] = x
    proj = jnp.dot(x.astype(BF16), w_ref[...], preferred_element_type=F32)
    c, sl, sh = c_ref[...], sl_ref[...], sh_ref[...]
    for h in range(DIFF_HEADS):
        q = proj[:, h * LANES:(h + 1) * LANES] * (DIFF_HEAD_DIM ** -0.5 * LOG2E)
        q_ref[h] = _rope_chunk(q, c, sl, sh).astype(BF16)


def _dq_proj(x, ffn, w, tables):
    b, s, d = x.shape
    wg, wu, wd, layer, ln_g, ln_b = ffn
    d_ff = wg.shape[-1]
    tab_spec = pl.BlockSpec((ROW_TILE, LANES), lambda bi, i: (i, 0))
    x_spec = pl.BlockSpec((None, ROW_TILE, d), lambda bi, i: (bi, i, 0))
    return pl.pallas_call(
        _dq_proj_kernel,
        grid=(b, s // ROW_TILE),
        in_specs=[x_spec, _resident_layer((d, d_ff), layer), _resident_layer((d, d_ff), layer),
                  _resident_layer((d_ff, d), layer),
                  _resident((1, d)), _resident((1, d)), _resident(w.shape), tab_spec, tab_spec, tab_spec],
        out_specs=[x_spec, pl.BlockSpec((None, DIFF_HEADS, ROW_TILE, LANES), lambda bi, i: (bi, 0, i, 0))],
        out_shape=[jax.ShapeDtypeStruct((b, s, d), F32), jax.ShapeDtypeStruct((b, DIFF_HEADS, s, LANES), BF16)],
        scratch_shapes=[pltpu.VMEM((ROW_TILE, d_ff), BF16)],
        compiler_params=_cparams(2),
        name="ffn_diff_q_proj",
    )(x, wg, wu, wd, ln_g, ln_b, w, *tables)


def _diff_attn_kernel(q_ref, k_ref, vt_ref, lam_ref, gain_ref, o_ref, acc_ref, sa_ref, sb_ref, *, lambda_init):
    qi = pl.program_id(2)
    n_heads = q_ref.shape[0]
    tq = DIFF_Q_TILE
    t0 = qi * tq
    kb_sz = ATT_KEY_BLOCK
    tok2 = _two(t0 + lax.broadcasted_iota(jnp.int32, (1, tq), 1))
    krow = lax.broadcasted_iota(jnp.int32, (kb_sz, 1), 0)
    rhs = [_split_heads_rhs(q_ref[h]) for h in range(n_heads)]
    last = (t0 + tq - 1) // kb_sz
    acc_ref[...] = jnp.zeros_like(acc_ref)

    def scores_into(kb, dst_ref):
        k0 = pl.multiple_of(kb * kb_sz, kb_sz)
        for h in range(n_heads):
            dst_ref[h] = _scores_t(k_ref[h, pl.ds(k0, kb_sz), :], rhs[h])

    def consume(kb, src_ref, st, maybe_diagonal):
        if maybe_diagonal:
            visible = kb * kb_sz + krow <= tok2
        out = []
        for h in range(n_heads):
            s = src_ref[h]
            if maybe_diagonal:
                s = jnp.where(visible, s, NEG)
            m, l = st[h]
            m_new = jnp.maximum(m, jnp.max(s, axis=0, keepdims=True))
            alpha = jnp.exp2(m - m_new)
            p = jnp.exp2(s - m_new)
            l = alpha * l + jnp.sum(p, axis=0, keepdims=True)
            acc_ref[h] = alpha * acc_ref[h] + jnp.dot(vt_ref[h, kb, :DIFF_V_DIM, :], p.astype(BF16),
                                                      preferred_element_type=F32)
            out.append((m_new, l))
        return tuple(out)

    def two_blocks(i, st):
        kb = 2 * i
        scores_into(kb + 1, sb_ref)
        st = consume(kb, sa_ref, st, False)
        scores_into(kb + 2, sa_ref)
        return consume(kb + 1, sb_ref, st, False)

    def tail_odd(st):
        scores_into(last, sb_ref)
        st = consume(last - 1, sa_ref, st, False)
        return consume(last, sb_ref, st, True)

    init = tuple((jnp.full((1, 2 * tq), NEG, F32), jnp.zeros((1, 2 * tq), F32)) for _ in range(n_heads))
    scores_into(0, sa_ref)
    st = lax.fori_loop(0, last // 2, two_blocks, init)
    st = lax.cond(last % 2 == 0, lambda st: consume(last, sa_ref, st, True), tail_odd, st)

    lp = lam_ref[...]
    lam = (jnp.exp(jnp.sum(lp[0:1] * lp[1:2], axis=1, keepdims=True))
           - jnp.exp(jnp.sum(lp[2:3] * lp[3:4], axis=1, keepdims=True)) + lambda_init)
    for h in range(n_heads):
        a = acc_ref[h] * (1.0 / st[h][1])
        o = a[:, :tq] - lam * a[:, tq:]
        ms = jnp.mean(o * o, axis=0, keepdims=True)
        o = o * lax.rsqrt(ms + NORM_EPS) * gain_ref[...] * (1.0 - lambda_init)
        o_ref[h] = o.T.astype(BF16)


def _diff_attn(q, k, vt, lam_params, gain, lambda_init):
    b, h, s, _ = q.shape
    nkb = s // ATT_KEY_BLOCK
    tq = DIFF_Q_TILE
    hps = DIFF_HEADS_PER_STEP
    per_head = lambda *shape: pl.BlockSpec((None, hps) + shape, lambda bi, hi, qi: (bi, hi) + (0,) * len(shape))
    q_spec = pl.BlockSpec((None, hps, tq, LANES), lambda bi, hi, qi: (bi, hi, qi, 0))
    return pl.pallas_call(
        functools.partial(_diff_attn_kernel, lambda_init=lambda_init),
        grid=(b, h // hps, s // tq),
        in_specs=[q_spec, per_head(s, LANES), per_head(nkb, DIFF_V_DIM + SUM_ROWS, ATT_KEY_BLOCK),
                  _resident(lam_params.shape), _resident(gain.shape)],
        out_specs=q_spec,
        out_shape=jax.ShapeDtypeStruct((b, h, s, LANES), BF16),
        scratch_shapes=[pltpu.VMEM((hps, DIFF_V_DIM, 2 * tq), F32),
                        pltpu.VMEM((hps, ATT_KEY_BLOCK, 2 * tq), F32), pltpu.VMEM((hps, ATT_KEY_BLOCK, 2 * tq), F32)],
        compiler_params=_cparams(3),
        name="diff_attn",
    )(q, k, vt, lam_params, gain)


def kernel(x, ffn1_w_gate, ffn1_w_up, ffn1_w_down, ffn2_w_gate, ffn2_w_up, ffn2_w_down, ln_g, ln_b, a_w_in,
           a_cmp_pos_k, a_cmp_w1_k, a_cmp_b1_k, a_cmp_w2_k, a_cmp_pos_v, a_cmp_w1_v, a_cmp_b1_v, a_cmp_w2_v,
           a_w_out, b_w_kv_shared, b_w_q, b_lambda_q1, b_lambda_k1, b_lambda_q2, b_lambda_k2, b_subln_g, b_w_out):
    bsz, seq, d = x.shape
    assert d == D_MODEL and seq % ROW_TILE == 0 and seq >= WINDOW + 2 * ATT_KEY_BLOCK
    tables = _rope_tables(seq)
    bf = lambda w: w.astype(BF16)
    row = lambda v: v[None, :]

    w1 = (bf(ffn1_w_gate), bf(ffn1_w_up), bf(ffn1_w_down))
    w2 = (bf(ffn2_w_gate), bf(ffn2_w_up), bf(ffn2_w_down))
    k12 = vt = None
    for layer in range(DEPTH):
        ffn1 = w1 + (layer, row(ln_g[layer, 0]), row(ln_b[layer, 0]))
        if layer < N_A_LAYERS:
            x, qc, qr, kc, vc, ksd, kwd, vst, vwt, gt = _nsa_proj(x, ffn1, _nsa_proj_weight(a_w_in[layer]), tables)
            wk = _compress_weights(a_cmp_pos_k[layer], a_cmp_w1_k[layer], a_cmp_b1_k[layer], a_cmp_w2_k[layer], True)
            wv = _compress_weights(a_cmp_pos_v[layer], a_cmp_w1_v[layer], a_cmp_b1_v[layer], a_cmp_w2_v[layer], False)
            kcd, vct = _compress(kc, vc, wk, wv)
            o = _nsa_attn(qc, qr, kcd, vct, ksd, kwd, vst, vwt, gt)
            w_out = a_w_out[layer]
        else:
            j = layer - N_A_LAYERS
            lambda_init = 0.8 - 0.6 * float(np.exp(-0.3 * layer))
            x, q = _dq_proj(x, ffn1, bf(_pair_maps_columns(b_w_q[j])), tables)
            lam_params = jnp.pad(jnp.stack([b_lambda_q1[j], b_lambda_k1[j], b_lambda_q2[j], b_lambda_k2[j]]),
                                 ((0, 4), (0, LANES - DIFF_HEAD_DIM)))
            gain = jnp.broadcast_to(b_subln_g[j][:, None], (DIFF_V_DIM, DIFF_Q_TILE))
            o = _diff_attn(q, k12, vt, lam_params, gain, lambda_init)
            w_out = b_w_out[j]
        x = _out_ffn(o, x, bf(w_out), row(ln_g[layer, 1]), row(ln_b[layer, 1]),
                     w2 + (layer, row(ln_g[layer, 2]), row(ln_b[layer, 2])))
        if layer == N_A_LAYERS - 1:
            w_kv = jnp.concatenate([_pair_maps_columns(b_w_kv_shared[:, :D_MODEL]), b_w_kv_shared[:, D_MODEL:]], axis=1)
            k12, vt = _kv_proj(x, bf(w_kv), tables)
    return x
```
